```python
import jax, jax.numpy as jnp
from jax import lax
import numpy as np

D_MODEL = 1024
BATCH = 8
SEQ = 2048
DEPTH = 4

GRID_W = 64
CTX_LEN = 256
EPS = 1e-6

MIX_WIDTH = D_MODEL
FOURIER_WIDTH = MIX_WIDTH // 4
N_FOURIER_HEADS = 4
FOURIER_HEAD_DIM = FOURIER_WIDTH // N_FOURIER_HEADS
POOL_WIDTH = MIX_WIDTH // 4
POOL_WINDOWS = (2, 4, 8, 16)
POOL_GROUP_DIM = POOL_WIDTH // len(POOL_WINDOWS)
ATTN_WIDTH = MIX_WIDTH // 2
HEAD_DIM = 128
N_Q_HEADS = ATTN_WIDTH // HEAD_DIM
N_KV_HEADS = 2
Q_PER_KV = N_Q_HEADS // N_KV_HEADS
KV_WIDTH = N_KV_HEADS * HEAD_DIM
ROPE_HALF = HEAD_DIM // 2
ROPE_THETA = 10000.0
Q_BLOCK = 128

F_OFF = 0
P_OFF = F_OFF + FOURIER_WIDTH
Q_OFF = P_OFF + POOL_WIDTH
K_OFF = Q_OFF + ATTN_WIDTH
V_OFF = K_OFF + KV_WIDTH
IN_WIDTH = V_OFF + KV_WIDTH

N_EXPERTS = 16
EC_CAPACITY_FACTOR = 2
EXPERT_FF = D_MODEL

kernel_name = "hybrid_fourier_pool_gqa_ec_moe_dit"


def rms_norm(x, g):
    xf = x.astype(jnp.float32)
    y = xf * lax.rsqrt(jnp.mean(xf * xf, axis=-1, keepdims=True) + EPS)
    return (y * g.astype(jnp.float32)).astype(x.dtype)


def modulate(x, g, shift, scale):
    xf = x.astype(jnp.float32)
    y = xf * lax.rsqrt(jnp.mean(xf * xf, axis=-1, keepdims=True) + EPS) * g.astype(jnp.float32)
    y = y * (1.0 + scale.astype(jnp.float32)) + shift.astype(jnp.float32)
    return y.astype(x.dtype)


def axial_angles(n):
    rows = n // GRID_W
    row = jnp.repeat(jnp.arange(rows, dtype=jnp.float32), GRID_W)
    col = jnp.tile(jnp.arange(GRID_W, dtype=jnp.float32), rows)
    inv = ROPE_THETA ** (-jnp.arange(0, ROPE_HALF, 2, dtype=jnp.float32) / ROPE_HALF)
    return row[:, None] * inv[None, :], col[:, None] * inv[None, :]


def rope_half(x, ang):
    cos = jnp.cos(ang)[None, :, None, :]
    sin = jnp.sin(ang)[None, :, None, :]
    xf = x.astype(jnp.float32)
    x1, x2 = xf[..., : ROPE_HALF // 2], xf[..., ROPE_HALF // 2:]
    return jnp.concatenate([x1 * cos - x2 * sin, x2 * cos + x1 * sin], axis=-1).astype(x.dtype)


def axial_rope(x, ang_row, ang_col):
    return jnp.concatenate([rope_half(x[..., :ROPE_HALF], ang_row),
                            rope_half(x[..., ROPE_HALF:], ang_col)], axis=-1)


def attend(q, k, v):
    s = jnp.einsum('bqhgd,bkhd->bhgqk', q, k).astype(jnp.float32) * (HEAD_DIM ** -0.5)
    p = jax.nn.softmax(s, axis=-1).astype(v.dtype)
    return jnp.einsum('bhgqk,bkhd->bqhgd', p, v)


def blocked_attention(q, k, v):
    B, L = q.shape[0], q.shape[1]
    nb = L // Q_BLOCK
    qb = q.reshape(B, nb, Q_BLOCK, N_KV_HEADS, Q_PER_KV, HEAD_DIM).transpose(1, 0, 2, 3, 4, 5)
    ob = lax.map(lambda qi: attend(qi, k, v), qb)
    return ob.transpose(1, 0, 2, 3, 4, 5).reshape(B, L, ATTN_WIDTH)


def fourier_mix(f, w):
    B, n, _ = f.shape
    fh = f.astype(jnp.float32).reshape(B, n, N_FOURIER_HEADS, FOURIER_HEAD_DIM).transpose(0, 2, 1, 3)
    fr = jnp.real(jnp.fft.fft2(fh, axes=(-2, -1), norm="ortho"))
    fr = fr.transpose(0, 2, 1, 3).reshape(B, n, FOURIER_WIDTH).astype(f.dtype)
    return fr @ w


def pool_mix(p, w, scale):
    B, n, _ = p.shape
    pf = p.astype(jnp.float32)
    cs = jnp.concatenate([jnp.zeros((B, 1, POOL_WIDTH), jnp.float32), jnp.cumsum(pf, axis=1)], axis=1)
    t = jnp.arange(n)
    outs = []
    for gi, win in enumerate(POOL_WINDOWS):
        lo = jnp.clip(t - win // 2, 0, n)
        hi = jnp.clip(t + win // 2, 0, n)
        sl = slice(gi * POOL_GROUP_DIM, (gi + 1) * POOL_GROUP_DIM)
        csg = cs[..., sl]
        s = jnp.take(csg, hi, axis=1) - jnp.take(csg, lo, axis=1)
        cnt = (hi - lo).astype(jnp.float32)[None, :, None]
        outs.append(s / cnt - pf[..., sl])
    d = jnp.stack(outs, axis=2).astype(p.dtype)
    y = jnp.einsum('bngc,gcd->bngd', d, w).reshape(B, n, POOL_WIDTH)
    return y * scale


def expert_choice_ffn(h, w_router, w_gate, w_up, w_down):
    B, n, D = h.shape
    cap = EC_CAPACITY_FACTOR * n // N_EXPERTS
    logits = jnp.einsum('bnd,de->bne', h, w_router).astype(jnp.float32)
    aff = jax.nn.softmax(logits, axis=-1)
    g, idx = lax.top_k(aff.transpose(0, 2, 1), cap)
    xs = jax.vmap(lambda hb, ib: hb[ib])(h, idx)
    a = jnp.einsum('becd,edf->becf', xs, w_gate)
    u = jnp.einsum('becd,edf->becf', xs, w_up)
    y = jnp.einsum('becf,efd->becd', jax.nn.silu(a) * u, w_down)
    y = y * g[..., None].astype(y.dtype)
    return jax.vmap(lambda yb, ib: jnp.zeros((n, D), yb.dtype).at[ib.reshape(-1)].add(yb.reshape(-1, D)))(y, idx)


def setup_inputs(seed: int = 0) -> dict:
    key = jax.random.key(seed)
    ks = jax.random.split(key, 20)
    nrm = jax.random.normal
    f32 = jnp.float32
    D = D_MODEL
    return {
        "x": nrm(ks[0], (BATCH, SEQ, D), f32),
        "c": nrm(ks[1], (BATCH, D), f32),
        "ctx": nrm(ks[2], (BATCH, CTX_LEN, D), f32),
        "c_ctx": nrm(ks[3], (D,), f32),
        "ada_w": nrm(ks[4], (DEPTH, D, 6 * D), f32) * (0.5 * D ** -0.5),
        "ada_b": nrm(ks[5], (DEPTH, 6 * D), f32) * 0.02,
        "norm1_g": 1.0 + 0.02 * nrm(ks[6], (DEPTH, D), f32),
        "norm2_g": 1.0 + 0.02 * nrm(ks[7], (DEPTH, D), f32),
        "w_in": nrm(ks[8], (DEPTH, D, IN_WIDTH), f32) * D ** -0.5,
        "w_fourier": nrm(ks[9], (DEPTH, FOURIER_WIDTH, FOURIER_WIDTH), f32) * FOURIER_WIDTH ** -0.5,
        "w_pool": nrm(ks[10], (DEPTH, len(POOL_WINDOWS), POOL_GROUP_DIM, POOL_GROUP_DIM), f32) * POOL_GROUP_DIM ** -0.5,
        "pool_scale": 1.0 + 0.1 * nrm(ks[11], (DEPTH, POOL_WIDTH), f32),
        "q_norm_g": 1.0 + 0.02 * nrm(ks[12], (DEPTH, HEAD_DIM), f32),
        "k_norm_g": 1.0 + 0.02 * nrm(ks[13], (DEPTH, HEAD_DIM), f32),
        "w_out": nrm(ks[14], (DEPTH, MIX_WIDTH, D), f32) * MIX_WIDTH ** -0.5,
        "w_router": nrm(ks[15], (DEPTH, D, N_EXPERTS), f32) * D ** -0.5,
        "w_gate": nrm(ks[16], (DEPTH, N_EXPERTS, D, EXPERT_FF), f32) * D ** -0.5,
        "w_up": nrm(ks[17], (DEPTH, N_EXPERTS, D, EXPERT_FF), f32) * D ** -0.5,
        "w_down": nrm(ks[18], (DEPTH, N_EXPERTS, EXPERT_FF, D), f32) * EXPERT_FF ** -0.5,
    }


def reference(x, c, ctx, c_ctx, ada_w, ada_b, norm1_g, norm2_g, w_in, w_fourier, w_pool, pool_scale,
              q_norm_g, k_norm_g, w_out, w_router, w_gate, w_up, w_down):
    B, L, D = x.shape
    Lc = ctx.shape[1]
    ang_row, ang_col = axial_angles(L)
    sc = jax.nn.silu(c)
    scc = jax.nn.silu(c_ctx)
    for i in range(DEPTH):
        last = i == DEPTH - 1
        sh1, sc1, g1, sh2, sc2, g2 = jnp.split((sc @ ada_w[i] + ada_b[i])[:, None, :], 6, axis=-1)
        csh1, csc1, cg1, csh2, csc2, cg2 = jnp.split((scc @ ada_w[i] + ada_b[i])[None, None, :], 6, axis=-1)

        hx = modulate(x, norm1_g[i], sh1, sc1)
        hc = modulate(ctx, norm1_g[i], csh1, csc1)
        px_all = hx @ w_in[i]
        kvc = hc @ w_in[i][:, K_OFF:]
        kc = rms_norm(kvc[..., :KV_WIDTH].reshape(B, Lc, N_KV_HEADS, HEAD_DIM), k_norm_g[i])
        vc = kvc[..., KV_WIDTH:].reshape(B, Lc, N_KV_HEADS, HEAD_DIM)

        qx = rms_norm(px_all[..., Q_OFF:K_OFF].reshape(B, L, N_Q_HEADS, HEAD_DIM), q_norm_g[i])
        kx = rms_norm(px_all[..., K_OFF:V_OFF].reshape(B, L, N_KV_HEADS, HEAD_DIM), k_norm_g[i])
        vx = px_all[..., V_OFF:].reshape(B, L, N_KV_HEADS, HEAD_DIM)
        qx = axial_rope(qx, ang_row, ang_col)
        kx = axial_rope(kx, ang_row, ang_col)
        k_all = jnp.concatenate([kx, kc], axis=1)
        v_all = jnp.concatenate([vx, vc], axis=1)
        ax = blocked_attention(qx, k_all, v_all)
        ox = jnp.concatenate([fourier_mix(px_all[..., F_OFF:P_OFF], w_fourier[i]),
                              pool_mix(px_all[..., P_OFF:Q_OFF], w_pool[i], pool_scale[i]),
                              ax], axis=-1) @ w_out[i]
        x_new = x + g1 * ox

        x_new = x_new + g2 * expert_choice_ffn(modulate(x_new, norm2_g[i], sh2, sc2),
                                               w_router[i], w_gate[i], w_up[i], w_down[i])

        if not last:
            pc_all = hc @ w_in[i][:, :K_OFF]
            qc = rms_norm(pc_all[..., Q_OFF:K_OFF].reshape(B, Lc, N_KV_HEADS, Q_PER_KV, HEAD_DIM), q_norm_g[i])
            ac = attend(qc, kc, vc).reshape(B, Lc, ATTN_WIDTH)
            oc = jnp.concatenate([fourier_mix(pc_all[..., F_OFF:P_OFF], w_fourier[i]),
                                  pool_mix(pc_all[..., P_OFF:Q_OFF], w_pool[i], pool_scale[i]),
                                  ac], axis=-1) @ w_out[i]
            ctx = ctx + cg1 * oc
            ctx = ctx + cg2 * expert_choice_ffn(modulate(ctx, norm2_g[i], csh2, csc2),
                                                w_router[i], w_gate[i], w_up[i], w_down[i])
        x = x_new
    return x
```

```python
import functools

import jax
import jax.numpy as jnp
from jax import lax
from jax.experimental import pallas as pl
from jax.experimental.pallas import tpu as pltpu

GRID_W = 64
EPS = 1e-6
N_FOURIER_HEADS = 4
POOL_WINDOWS = (2, 4, 8, 16)
HEAD_DIM = 128
N_KV_HEADS = 2
ROPE_HALF = HEAD_DIM // 2
ROPE_THETA = 10000.0
N_EXPERTS = 16
EC_CAPACITY_FACTOR = 2

LANES = 128
SUBLANES = 8
VMEM_LIMIT_BYTES = 56 * 1024 * 1024

F32 = jnp.float32
BF16 = jnp.bfloat16
HIGHEST = lax.Precision.HIGHEST


def _cparams(semantics):
    return pltpu.CompilerParams(dimension_semantics=semantics, vmem_limit_bytes=VMEM_LIMIT_BYTES)


def _row_tile(n, target):
    t = min(n, target)
    assert n % t == 0
    return t


def _ada_kernel(c_ref, w_ref, b_ref, o_ref):
    c = c_ref[...]
    s = c * jax.nn.sigmoid(c)
    o_ref[0] = jnp.dot(s, w_ref[0], preferred_element_type=F32, precision=HIGHEST) + b_ref[0]


def _ada_mod(cond, ada_w, ada_b):
    depth, d, n6 = ada_w.shape
    r = cond.shape[0]
    tn = _row_tile(n6, 1536)
    return pl.pallas_call(
        _ada_kernel,
        grid=(depth, n6 // tn),
        in_specs=[
            pl.BlockSpec((r, d), lambda i, j: (0, 0)),
            pl.BlockSpec((1, d, tn), lambda i, j: (i, 0, j)),
            pl.BlockSpec((1, 1, tn), lambda i, j: (i, 0, j)),
        ],
        out_specs=pl.BlockSpec((1, r, tn), lambda i, j: (i, 0, j)),
        out_shape=jax.ShapeDtypeStruct((depth, r, n6), F32),
        compiler_params=_cparams(("arbitrary", "arbitrary")),
        name="ada_mod",
    )(cond, ada_w, ada_b.reshape(depth, 1, n6))


def _modulate(xf, g, shift, scale):
    ms = jnp.mean(xf * xf, axis=-1, keepdims=True)
    y = xf * lax.rsqrt(ms + EPS) * g
    return y * (1.0 + scale) + shift


def _head_rms(xh, g):
    ms = jnp.mean(xh * xh, axis=-1, keepdims=True)
    return xh * lax.rsqrt(ms + EPS) * g


def _rope(xh, cos, sin_signed, lo_half):
    partner = jnp.where(lo_half, pltpu.roll(xh, HEAD_DIM - ROPE_HALF // 2, 1), pltpu.roll(xh, ROPE_HALF // 2, 1))
    return xh * cos + partner * sin_signed


def _inproj_kernel(x_ref, mod_ref, g_ref, w_ref, qg_ref, kg_ref, cos_ref, sin_ref,
                   f_ref, p_ref, q_ref, k_ref, v_ref, *, d, fw, pw, aw, kvw, use_rope):
    xf = x_ref[0]
    mod = mod_ref[0]
    h = _modulate(xf, g_ref[0], mod[:, 0:d], mod[:, d:2 * d]).astype(BF16)
    pr = jnp.dot(h, w_ref[0], preferred_element_type=F32)
    f_ref[0] = pr[:, 0:fw].astype(BF16)
    p_ref[0] = pr[:, fw:fw + pw]
    q_off = fw + pw
    k_off = q_off + aw
    v_off = k_off + kvw
    if use_rope:
        cos = cos_ref[...]
        sin = sin_ref[...]
        lane = lax.broadcasted_iota(jnp.int32, cos.shape, 1)
        lo_half = (lane % ROPE_HALF) < (ROPE_HALF // 2)
    q_scale = HEAD_DIM ** -0.5
    for j in range(aw // HEAD_DIM):
        qh = _head_rms(pr[:, q_off + j * HEAD_DIM:q_off + (j + 1) * HEAD_DIM], qg_ref[0])
        if use_rope:
            qh = _rope(qh, cos, sin, lo_half)
        q_ref[0, :, j * HEAD_DIM:(j + 1) * HEAD_DIM] = (qh * q_scale).astype(BF16)
    for j in range(kvw // HEAD_DIM):
        kh = _head_rms(pr[:, k_off + j * HEAD_DIM:k_off + (j + 1) * HEAD_DIM], kg_ref[0])
        if use_rope:
            kh = _rope(kh, cos, sin, lo_half)
        k_ref[0, :, j * HEAD_DIM:(j + 1) * HEAD_DIM] = kh.astype(BF16)
    v_ref[0] = pr[:, v_off:v_off + kvw].astype(BF16)


def _inproj(x, mod3, mod_row, layer, norm_g, w_in_bf, qg, kg, cos_t, sin_t, use_rope):
    b, n, d = x.shape
    in_w = w_in_bf.shape[-1]
    fw = d // 4
    pw = d // 4
    aw = d // 2
    kvw = N_KV_HEADS * HEAD_DIM
    tm = _row_tile(n, 512)
    n6 = mod3.shape[-1]
    kern = functools.partial(_inproj_kernel, d=d, fw=fw, pw=pw, aw=aw, kvw=kvw, use_rope=use_rope)
    out_shapes = (
        jax.ShapeDtypeStruct((b, n, fw), BF16),
        jax.ShapeDtypeStruct((b, n, pw), F32),
        jax.ShapeDtypeStruct((b, n, aw), BF16),
        jax.ShapeDtypeStruct((b, n, kvw), BF16),
        jax.ShapeDtypeStruct((b, n, kvw), BF16),
    )

    def row_spec(w):
        return pl.BlockSpec((1, tm, w), lambda bi, ti: (bi, ti, 0))

    return pl.pallas_call(
        kern,
        grid=(b, n // tm),
        in_specs=[
            pl.BlockSpec((1, tm, d), lambda bi, ti: (bi, ti, 0)),
            pl.BlockSpec((1, 1, n6), lambda bi, ti: (mod_row(bi), 0, 0)),
            pl.BlockSpec((1, 1, d), lambda bi, ti: (layer, 0, 0)),
            pl.BlockSpec((1, d, in_w), lambda bi, ti: (layer, 0, 0)),
            pl.BlockSpec((1, 1, HEAD_DIM), lambda bi, ti: (layer, 0, 0)),
            pl.BlockSpec((1, 1, HEAD_DIM), lambda bi, ti: (layer, 0, 0)),
            pl.BlockSpec((tm, HEAD_DIM), lambda bi, ti: (ti, 0)),
            pl.BlockSpec((tm, HEAD_DIM), lambda bi, ti: (ti, 0)),
        ],
        out_specs=(row_spec(fw), row_spec(pw), row_spec(aw), row_spec(kvw), row_spec(kvw)),
        out_shape=out_shapes,
        compiler_params=_cparams(("parallel", "arbitrary")),
        name="inproj",
    )(x, mod3, norm_g, w_in_bf, qg, kg, cos_t, sin_t)


def _attn_kernel(*refs, n_src, group):
    q_ref = refs[0]
    k_refs = refs[1:1 + n_src]
    v_refs = refs[1 + n_src:1 + 2 * n_src]
    o_ref = refs[1 + 2 * n_src]
    dn = (((1,), (1,)), ((), ()))
    for gq in range(group):
        q = q_ref[0, :, gq * HEAD_DIM:(gq + 1) * HEAD_DIM]
        s = [lax.dot_general(q, k_ref[0], dn, preferred_element_type=F32) for k_ref in k_refs]
        m = s[0].max(axis=-1, keepdims=True)
        for si in s[1:]:
            m = jnp.maximum(m, si.max(axis=-1, keepdims=True))
        p = [jnp.exp(si - m) for si in s]
        l = p[0].sum(axis=-1, keepdims=True)
        for pi in p[1:]:
            l = l + pi.sum(axis=-1, keepdims=True)
        o = jnp.dot(p[0].astype(BF16), v_refs[0][0], preferred_element_type=F32)
        for pi, v_ref in zip(p[1:], v_refs[1:]):
            o = o + jnp.dot(pi.astype(BF16), v_ref[0], preferred_element_type=F32)
        o_ref[0, :, gq * HEAD_DIM:(gq + 1) * HEAD_DIM] = (o / l).astype(BF16)


def _attention(q, ks, vs):
    b, n, aw = q.shape
    group = aw // HEAD_DIM // N_KV_HEADS
    gw = group * HEAD_DIM
    tq = _row_tile(n, 512)
    n_src = len(ks)

    def kv_spec(a):
        return pl.BlockSpec((1, a.shape[1], HEAD_DIM), lambda bi, hi, qi: (bi, 0, hi))

    return pl.pallas_call(
        functools.partial(_attn_kernel, n_src=n_src, group=group),
        grid=(b, N_KV_HEADS, n // tq),
        in_specs=[pl.BlockSpec((1, tq, gw), lambda bi, hi, qi: (bi, qi, hi))]
        + [kv_spec(a) for a in ks] + [kv_spec(a) for a in vs],
        out_specs=pl.BlockSpec((1, tq, gw), lambda bi, hi, qi: (bi, qi, hi)),
        out_shape=jax.ShapeDtypeStruct((b, n, aw), BF16),
        compiler_params=_cparams(("parallel", "parallel", "arbitrary")),
        name="attention",
    )(q, *ks, *vs)


def _fourier_kernel(f_ref, cc_ref, sc_ref, cn_ref, sn_ref, w_ref, o_ref, xc_ref, xs_ref, *, scale):
    @pl.when(pl.program_id(1) == 0)
    def _():
        f = f_ref[0]
        xc_ref[...] = jnp.dot(f, cc_ref[...], preferred_element_type=F32).astype(BF16)
        xs_ref[...] = jnp.dot(f, sc_ref[...], preferred_element_type=F32).astype(BF16)

    fr = (jnp.dot(cn_ref[...], xc_ref[...], preferred_element_type=F32)
          - jnp.dot(sn_ref[...], xs_ref[...], preferred_element_type=F32)) * scale
    o_ref[0] = jnp.dot(fr.astype(BF16), w_ref[0], preferred_element_type=F32).astype(BF16)


def _dft_tables(n):
    k = jnp.arange(n, dtype=jnp.int32)
    ang = ((k[:, None] * k[None, :]) % n).astype(F32) * (2.0 * jnp.pi / n)
    return jnp.cos(ang), jnp.sin(ang)


def _fourier(f, layer, w_fourier_bf, cn, sn, cc_bd, sc_bd):
    b, n, fw = f.shape
    tm = _row_tile(n, 512)
    scale = float((n * (fw // N_FOURIER_HEADS)) ** -0.5)
    return pl.pallas_call(
        functools.partial(_fourier_kernel, scale=scale),
        grid=(b, n // tm),
        in_specs=[
            pl.BlockSpec((1, n, fw), lambda bi, ti: (bi, 0, 0)),
            pl.BlockSpec((fw, fw), lambda bi, ti: (0, 0)),
            pl.BlockSpec((fw, fw), lambda bi, ti: (0, 0)),
            pl.BlockSpec((tm, n), lambda bi, ti: (ti, 0)),
            pl.BlockSpec((tm, n), lambda bi, ti: (ti, 0)),
            pl.BlockSpec((1, fw, fw), lambda bi, ti: (layer, 0, 0)),
        ],
        out_specs=pl.BlockSpec((1, tm, fw), lambda bi, ti: (bi, ti, 0)),
        out_shape=jax.ShapeDtypeStruct((b, n, fw), BF16),
        scratch_shapes=[pltpu.VMEM((n, fw), BF16), pltpu.VMEM((n, fw), BF16)],
        compiler_params=_cparams(("parallel", "arbitrary")),
        name="fourier",
    )(f, cc_bd, sc_bd, cn, sn, w_fourier_bf)


POOL_HALO = 16
POOL_CHUNK = 128


def _pool_kernel(p_ref, w_ref, s_ref, o_ref, buf_ref, *, n, pw):
    gdim = pw // len(POOL_WINDOWS)
    zeros = jnp.zeros((POOL_HALO, pw), F32)
    buf_ref[0:POOL_HALO, :] = zeros
    buf_ref[POOL_HALO + n:POOL_HALO + n + POOL_HALO, :] = zeros
    buf_ref[POOL_HALO:POOL_HALO + n, :] = p_ref[0]

    lane = lax.broadcasted_iota(jnp.int32, (1, pw), 1)
    half = jnp.zeros((1, pw), jnp.int32)
    for gi, win in enumerate(POOL_WINDOWS):
        half = jnp.where((lane >= gi * gdim) & (lane < (gi + 1) * gdim), win // 2, half)
    max_half = max(POOL_WINDOWS) // 2
    ch = min(POOL_CHUNK, n)
    for c0 in range(0, n, ch):
        acc = jnp.zeros((ch, pw), F32)
        for s in range(-max_half, max_half):
            term = buf_ref[POOL_HALO + c0 + s:POOL_HALO + c0 + s + ch, :]
            inside = (s >= -half) & (s < half)
            acc = acc + jnp.where(inside, term, 0.0)
        t = c0 + lax.broadcasted_iota(jnp.int32, (ch, pw), 0)
        cnt = (jnp.minimum(t + half, n) - jnp.maximum(t - half, 0)).astype(F32)
        dlt = acc / cnt - buf_ref[POOL_HALO + c0:POOL_HALO + c0 + ch, :]
        y = jnp.dot(dlt.astype(BF16), w_ref[0], preferred_element_type=F32) * s_ref[0]
        o_ref[0, c0:c0 + ch, :] = y.astype(BF16)


def _pool(p, layer, w_pool_bd, pool_scale):
    b, n, pw = p.shape
    return pl.pallas_call(
        functools.partial(_pool_kernel, n=n, pw=pw),
        grid=(b,),
        in_specs=[
            pl.BlockSpec((1, n, pw), lambda bi: (bi, 0, 0)),
            pl.BlockSpec((1, pw, pw), lambda bi: (layer, 0, 0)),
            pl.BlockSpec((1, 1, pw), lambda bi: (layer, 0, 0)),
        ],
        out_specs=pl.BlockSpec((1, n, pw), lambda bi: (bi, 0, 0)),
        out_shape=jax.ShapeDtypeStruct((b, n, pw), BF16),
        scratch_shapes=[pltpu.VMEM((n + 2 * POOL_HALO, pw), F32)],
        compiler_params=_cparams(("parallel",)),
        name="pool",
    )(p, w_pool_bd, pool_scale)


def _outproj_kernel(fo_ref, po_ref, ao_ref, x_ref, mod_ref, g_ref, w_ref, wr_ref,
                    xn_ref, h_ref, aff_ref, *, d, fw, pw):
    w = w_ref[0]
    ox = jnp.dot(fo_ref[0], w[0:fw], preferred_element_type=F32)
    ox = ox + jnp.dot(po_ref[0], w[fw:fw + pw], preferred_element_type=F32)
    ox = ox + jnp.dot(ao_ref[0], w[fw + pw:], preferred_element_type=F32)
    mod = mod_ref[0]
    xn = x_ref[0] + mod[:, 2 * d:3 * d] * ox
    xn_ref[0] = xn
    h = _modulate(xn, g_ref[0], mod[:, 3 * d:4 * d], mod[:, 4 * d:5 * d])
    h_ref[0] = h
    logits = jnp.dot(h.astype(BF16), wr_ref[0], preferred_element_type=F32)
    lane = lax.broadcasted_iota(jnp.int32, logits.shape, 1)
    logits = jnp.where(lane < N_EXPERTS, logits, -jnp.inf)
    e = jnp.exp(logits - logits.max(axis=-1, keepdims=True))
    aff = e / e.sum(axis=-1, keepdims=True)
    aff_ref[0] = aff.T[0:N_EXPERTS, :]


def _outproj(fo, po, ao, x, mod3, mod_row, layer, norm_g, w_out_bf, w_router_pad):
    b, n, d = x.shape
    fw = fo.shape[-1]
    pw = po.shape[-1]
    aw = ao.shape[-1]
    n6 = mod3.shape[-1]
    tm = _row_tile(n, 512)

    def row_spec(w):
        return pl.BlockSpec((1, tm, w), lambda bi, ti: (bi, ti, 0))

    return pl.pallas_call(
        functools.partial(_outproj_kernel, d=d, fw=fw, pw=pw),
        grid=(b, n // tm),
        in_specs=[
            row_spec(fw), row_spec(pw), row_spec(aw), row_spec(d),
            pl.BlockSpec((1, 1, n6), lambda bi, ti: (mod_row(bi), 0, 0)),
            pl.BlockSpec((1, 1, d), lambda bi, ti: (layer, 0, 0)),
            pl.BlockSpec((1, d, d), lambda bi, ti: (layer, 0, 0)),
            pl.BlockSpec((1, d, LANES), lambda bi, ti: (layer, 0, 0)),
        ],
        out_specs=(row_spec(d), row_spec(d),
                   pl.BlockSpec((1, N_EXPERTS, tm), lambda bi, ti: (bi, 0, ti))),
        out_shape=(jax.ShapeDtypeStruct((b, n, d), F32),
                   jax.ShapeDtypeStruct((b, n, d), F32),
                   jax.ShapeDtypeStruct((b, N_EXPERTS, n), F32)),
        compiler_params=_cparams(("parallel", "arbitrary")),
        name="outproj",
    )(fo, po, ao, x, mod3, norm_g, w_out_bf, w_router_pad)


def _gather_kernel(idx_ref, h_ref, o_ref, rows_ref, *, cap):
    def body(r, carry):
        t = idx_ref[0, 0, r]
        rows_ref[pl.ds(r, 1), :] = h_ref[0, pl.ds(t, 1), :]
        return carry

    lax.fori_loop(0, cap, body, 0, unroll=8)
    o_ref[0, 0] = rows_ref[...].astype(BF16)


def _gather(h, idx):
    b, n, d = h.shape
    _, e, cap = idx.shape
    return pl.pallas_call(
        functools.partial(_gather_kernel, cap=cap),
        grid=(b, e),
        in_specs=[
            pl.BlockSpec((1, 1, cap), lambda bi, ei: (bi * e + ei, 0, 0), memory_space=pltpu.SMEM),
            pl.BlockSpec((1, n, d), lambda bi, ei: (bi, 0, 0)),
        ],
        out_specs=pl.BlockSpec((1, 1, cap, d), lambda bi, ei: (bi, ei, 0, 0)),
        out_shape=jax.ShapeDtypeStruct((b, e, cap, d), BF16),
        scratch_shapes=[pltpu.VMEM((cap, d), F32)],
        compiler_params=_cparams(("parallel", "arbitrary")),
        name="moe_gather",
    )(idx.reshape(b * e, 1, cap), h)


def _ffn_kernel(xs_ref, wg_ref, wu_ref, wd_ref, y_ref, wg_bf, wu_bf, wd_bf, *, bt):
    @pl.when(pl.program_id(1) == 0)
    def _():
        wg_bf[...] = wg_ref[0, 0].astype(BF16)
        wu_bf[...] = wu_ref[0, 0].astype(BF16)
        wd_bf[...] = wd_ref[0, 0].astype(BF16)

    for bi in range(bt):
        xs = xs_ref[bi, 0]
        a = jnp.dot(xs, wg_bf[...], preferred_element_type=F32)
        u = jnp.dot(xs, wu_bf[...], preferred_element_type=F32)
        hid = (a * jax.nn.sigmoid(a) * u).astype(BF16)
        y_ref[bi, 0] = jnp.dot(hid, wd_bf[...], preferred_element_type=F32)


def _ffn(xs, layer, w_gate, w_up, w_down):
    b, e, cap, d = xs.shape
    ff = w_gate.shape[-1]
    bt = min(b, 4)
    assert b % bt == 0
    return pl.pallas_call(
        functools.partial(_ffn_kernel, bt=bt),
        grid=(e, b // bt),
        in_specs=[
            pl.BlockSpec((bt, 1, cap, d), lambda ei, ji: (ji, ei, 0, 0)),
            pl.BlockSpec((1, 1, d, ff), lambda ei, ji: (layer, ei, 0, 0)),
            pl.BlockSpec((1, 1, d, ff), lambda ei, ji: (layer, ei, 0, 0)),
            pl.BlockSpec((1, 1, ff, d), lambda ei, ji: (layer, ei, 0, 0)),
        ],
        out_specs=pl.BlockSpec((bt, 1, cap, d), lambda ei, ji: (ji, ei, 0, 0)),
        out_shape=jax.ShapeDtypeStruct((b, e, cap, d), F32),
        scratch_shapes=[pltpu.VMEM((d, ff), BF16), pltpu.VMEM((d, ff), BF16), pltpu.VMEM((ff, d), BF16)],
        compiler_params=_cparams(("parallel", "arbitrary")),
        name="moe_ffn",
    )(xs, w_gate, w_up, w_down)


def _combine_kernel(idx_ref, gate_ref, y_ref, xn_ref, mod_ref, o_ref, *, cap, d, n_exp):
    ei = pl.program_id(1)

    @pl.when(ei == 0)
    def _():
        o_ref[...] = jnp.zeros_like(o_ref)

    def body(r, carry):
        t = idx_ref[0, 0, r]
        g = gate_ref[0, 0, r]
        o_ref[0, pl.ds(t, 1), :] = o_ref[0, pl.ds(t, 1), :] + y_ref[0, 0, pl.ds(r, 1), :] * g
        return carry

    lax.fori_loop(0, cap, body, 0, unroll=8)

    @pl.when(ei == n_exp - 1)
    def _():
        o_ref[0] = xn_ref[0] + mod_ref[0][:, 5 * d:6 * d] * o_ref[0]


def _combine(y, idx, gate, xn, mod3, mod_row):
    b, e, cap, d = y.shape
    n = xn.shape[1]
    n6 = mod3.shape[-1]
    return pl.pallas_call(
        functools.partial(_combine_kernel, cap=cap, d=d, n_exp=e),
        grid=(b, e),
        in_specs=[
            pl.BlockSpec((1, 1, cap), lambda bi, ei: (bi * e + ei, 0, 0), memory_space=pltpu.SMEM),
            pl.BlockSpec((1, 1, cap), lambda bi, ei: (bi * e + ei, 0, 0), memory_space=pltpu.SMEM),
            pl.BlockSpec((1, 1, cap, d), lambda bi, ei: (bi, ei, 0, 0)),
            pl.BlockSpec((1, n, d), lambda bi, ei: (bi, 0, 0)),
            pl.BlockSpec((1, 1, n6), lambda bi, ei: (mod_row(bi), 0, 0)),
        ],
        out_specs=pl.BlockSpec((1, n, d), lambda bi, ei: (bi, 0, 0)),
        out_shape=jax.ShapeDtypeStruct((b, n, d), F32),
        compiler_params=_cparams(("parallel", "arbitrary")),
        name="moe_combine",
    )(idx.reshape(b * e, 1, cap), gate.reshape(b * e, 1, cap), y, xn, mod3)


def _moe(h, aff_t, xn, mod3, mod_row, layer, w_gate, w_up, w_down):
    n = h.shape[1]
    cap = EC_CAPACITY_FACTOR * n // N_EXPERTS
    gate, idx = lax.top_k(aff_t, cap)
    xs = _gather(h, idx)
    y = _ffn(xs, layer, w_gate, w_up, w_down)
    return _combine(y, idx, gate, xn, mod3, mod_row)


def _rope_tables(n):
    t = jnp.arange(n, dtype=jnp.int32)
    row = (t // GRID_W).astype(F32)
    col = (t % GRID_W).astype(F32)
    inv = ROPE_THETA ** (-jnp.arange(0, ROPE_HALF, 2, dtype=F32) / ROPE_HALF)
    ar = row[:, None] * inv[None, :]
    ac = col[:, None] * inv[None, :]
    ang = jnp.concatenate([ar, ar, ac, ac], axis=-1)
    lane = jnp.arange(HEAD_DIM)
    sign = jnp.where((lane % ROPE_HALF) < ROPE_HALF // 2, -1.0, 1.0).astype(F32)
    return jnp.cos(ang), jnp.sin(ang) * sign[None, :]


def _block_diag(blocks):
    g, a, c = blocks.shape
    eye = jnp.eye(g, dtype=blocks.dtype)
    return (eye[:, None, :, None] * blocks[:, :, None, :]).reshape(g * a, g * c)


def kernel(x, c, ctx, c_ctx, ada_w, ada_b, norm1_g, norm2_g, w_in, w_fourier, w_pool, pool_scale,
           q_norm_g, k_norm_g, w_out, w_router, w_gate, w_up, w_down):
    b, n, d = x.shape
    lc = ctx.shape[1]
    depth = ada_w.shape[0]
    fw = d // 4
    k_off = fw + d // 4 + d // 2

    rows = -(-(b + 1) // SUBLANES) * SUBLANES
    cond = jnp.zeros((rows, d), F32).at[:b].set(c).at[b].set(c_ctx)
    mod = _ada_mod(cond, ada_w, ada_b)

    def lat_row(bi):
        return bi

    def ctx_row(bi):
        return b

    norm1_g = norm1_g.reshape(depth, 1, d)
    norm2_g = norm2_g.reshape(depth, 1, d)
    q_norm_g = q_norm_g.reshape(depth, 1, HEAD_DIM)
    k_norm_g = k_norm_g.reshape(depth, 1, HEAD_DIM)
    pool_scale = pool_scale.reshape(depth, 1, d // 4)
    w_in_bf = w_in.astype(BF16)
    w_out_bf = w_out.astype(BF16)
    w_fourier_bf = w_fourier.astype(BF16)
    w_router_pad = jnp.zeros((depth, d, LANES), BF16).at[:, :, :N_EXPERTS].set(w_router.astype(BF16))
    w_pool_bd = jax.vmap(_block_diag)(w_pool).astype(BF16)

    cos_t, sin_t = _rope_tables(n)
    cos_c, sin_c = _rope_tables(lc)
    cn, sn = (t.astype(BF16) for t in _dft_tables(n))
    cnc, snc = (t.astype(BF16) for t in _dft_tables(lc))
    hd = fw // N_FOURIER_HEADS
    cch, sch = _dft_tables(hd)
    cc_bd = _block_diag(jnp.broadcast_to(cch, (N_FOURIER_HEADS, hd, hd))).astype(BF16)
    sc_bd = _block_diag(jnp.broadcast_to(sch, (N_FOURIER_HEADS, hd, hd))).astype(BF16)

    for i in range(depth):
        last = i == depth - 1
        mod3 = mod[i].reshape(rows, 1, 6 * d)

        fx, px, qx, kx, vx = _inproj(x, mod3, lat_row, i, norm1_g, w_in_bf, q_norm_g, k_norm_g,
                                     cos_t, sin_t, True)
        fc, pc, qc, kc, vc = _inproj(ctx, mod3, ctx_row, i, norm1_g, w_in_bf, q_norm_g, k_norm_g,
                                     cos_c, sin_c, False)

        ax = _attention(qx, [kx, kc], [vx, vc])
        fox = _fourier(fx, i, w_fourier_bf, cn, sn, cc_bd, sc_bd)
        pox = _pool(px, i, w_pool_bd, pool_scale)
        xn, hx, affx = _outproj(fox, pox, ax, x, mod3, lat_row, i, norm2_g, w_out_bf, w_router_pad)
        x_next = _moe(hx, affx, xn, mod3, lat_row, i, w_gate, w_up, w_down)

        if not last:
            ac = _attention(qc, [kc], [vc])
            foc = _fourier(fc, i, w_fourier_bf, cnc, snc, cc_bd, sc_bd)
            poc = _pool(pc, i, w_pool_bd, pool_scale)
            cn_, hc, affc = _outproj(foc, poc, ac, ctx, mod3, ctx_row, i, norm2_g, w_out_bf, w_router_pad)
            ctx = _moe(hc, affc, cn_, mod3, ctx_row, i, w_gate, w_up, w_down)
        x = x_next
    return x
```

```python
import functools

import jax
import jax.numpy as jnp
from jax import lax
from jax.experimental import pallas as pl
from jax.experimental.pallas import tpu as pltpu

GRID_W = 64
EPS = 1e-6
N_FOURIER_HEADS = 4
POOL_WINDOWS = (2, 4, 8, 16)
HEAD_DIM = 128
N_KV_HEADS = 2
ROPE_HALF = HEAD_DIM // 2
ROPE_THETA = 10000.0
N_EXPERTS = 16
EC_CAPACITY_FACTOR = 2

LANES = 128
SUBLANES = 8
VMEM_LIMIT_BYTES = 56 * 1024 * 1024

F32 = jnp.float32
BF16 = jnp.bfloat16
HIGHEST = lax.Precision.HIGHEST


def _cparams(semantics):
    return pltpu.CompilerParams(dimension_semantics=semantics, vmem_limit_bytes=VMEM_LIMIT_BYTES)


def _row_tile(n, target):
    t = min(n, target)
    assert n % t == 0
    return t


def _ada_kernel(c_ref, w_ref, b_ref, o_ref):
    c = c_ref[...]
    s = c * jax.nn.sigmoid(c)
    o_ref[0] = jnp.dot(s, w_ref[0], preferred_element_type=F32, precision=HIGHEST) + b_ref[0]


def _ada_mod(cond, ada_w, ada_b):
    depth, d, n6 = ada_w.shape
    r = cond.shape[0]
    tn = _row_tile(n6, 1536)
    return pl.pallas_call(
        _ada_kernel,
        grid=(depth, n6 // tn),
        in_specs=[
            pl.BlockSpec((r, d), lambda i, j: (0, 0)),
            pl.BlockSpec((1, d, tn), lambda i, j: (i, 0, j)),
            pl.BlockSpec((1, 1, tn), lambda i, j: (i, 0, j)),
        ],
        out_specs=pl.BlockSpec((1, r, tn), lambda i, j: (i, 0, j)),
        out_shape=jax.ShapeDtypeStruct((depth, r, n6), F32),
        compiler_params=_cparams(("arbitrary", "arbitrary")),
        name="ada_mod",
    )(cond, ada_w, ada_b.reshape(depth, 1, n6))


def _modulate(xf, g, shift, scale):
    ms = jnp.mean(xf * xf, axis=-1, keepdims=True)
    y = xf * lax.rsqrt(ms + EPS) * g
    return y * (1.0 + scale) + shift


def _head_rms(xh, g):
    ms = jnp.mean(xh * xh, axis=-1, keepdims=True)
    return xh * lax.rsqrt(ms + EPS) * g


def _rope(xh, cos, sin_signed, lo_half):
    partner = jnp.where(lo_half, pltpu.roll(xh, HEAD_DIM - ROPE_HALF // 2, 1), pltpu.roll(xh, ROPE_HALF // 2, 1))
    return xh * cos + partner * sin_signed


def _inproj_kernel(x_ref, mod_ref, g_ref, w_ref, qg_ref, kg_ref, cos_ref, sin_ref,
                   f_ref, p_ref, q_ref, k_ref, v_ref, *, d, fw, pw, aw, kvw, use_rope):
    xf = x_ref[0]
    mod = mod_ref[0]
    h = _modulate(xf, g_ref[0], mod[:, 0:d], mod[:, d:2 * d]).astype(BF16)
    pr = jnp.dot(h, w_ref[0], preferred_element_type=F32)
    f_ref[0] = pr[:, 0:fw].astype(BF16)
    p_ref[0] = pr[:, fw:fw + pw]
    q_off = fw + pw
    k_off = q_off + aw
    v_off = k_off + kvw
    if use_rope:
        cos = cos_ref[...]
        sin = sin_ref[...]
        lane = lax.broadcasted_iota(jnp.int32, cos.shape, 1)
        lo_half = (lane % ROPE_HALF) < (ROPE_HALF // 2)
    q_scale = HEAD_DIM ** -0.5
    for j in range(aw // HEAD_DIM):
        qh = _head_rms(pr[:, q_off + j * HEAD_DIM:q_off + (j + 1) * HEAD_DIM], qg_ref[0])
        if use_rope:
            qh = _rope(qh, cos, sin, lo_half)
        q_ref[0, :, j * HEAD_DIM:(j + 1) * HEAD_DIM] = (qh * q_scale).astype(BF16)
    for j in range(kvw // HEAD_DIM):
        kh = _head_rms(pr[:, k_off + j * HEAD_DIM:k_off + (j + 1) * HEAD_DIM], kg_ref[0])
        if use_rope:
            kh = _rope(kh, cos, sin, lo_half)
        k_ref[0, :, j * HEAD_DIM:(j + 1) * HEAD_DIM] = kh.astype(BF16)
    v_ref[0] = pr[:, v_off:v_off + kvw].astype(BF16)


def _inproj(x, mod3, mod_row, layer, norm_g, w_in_bf, qg, kg, cos_t, sin_t, use_rope):
    b, n, d = x.shape
    in_w = w_in_bf.shape[-1]
    fw = d // 4
    pw = d // 4
    aw = d // 2
    kvw = N_KV_HEADS * HEAD_DIM
    tm = _row_tile(n, 512)
    n6 = mod3.shape[-1]
    kern = functools.partial(_inproj_kernel, d=d, fw=fw, pw=pw, aw=aw, kvw=kvw, use_rope=use_rope)
    out_shapes = (
        jax.ShapeDtypeStruct((b, n, fw), BF16),
        jax.ShapeDtypeStruct((b, n, pw), F32),
        jax.ShapeDtypeStruct((b, n, aw), BF16),
        jax.ShapeDtypeStruct((b, n, kvw), BF16),
        jax.ShapeDtypeStruct((b, n, kvw), BF16),
    )

    def row_spec(w):
        return pl.BlockSpec((1, tm, w), lambda bi, ti: (bi, ti, 0))

    return pl.pallas_call(
        kern,
        grid=(b, n // tm),
        in_specs=[
            pl.BlockSpec((1, tm, d), lambda bi, ti: (bi, ti, 0)),
            pl.BlockSpec((1, 1, n6), lambda bi, ti: (mod_row(bi), 0, 0)),
            pl.BlockSpec((1, 1, d), lambda bi, ti: (layer, 0, 0)),
            pl.BlockSpec((1, d, in_w), lambda bi, ti: (layer, 0, 0)),
            pl.BlockSpec((1, 1, HEAD_DIM), lambda bi, ti: (layer, 0, 0)),
            pl.BlockSpec((1, 1, HEAD_DIM), lambda bi, ti: (layer, 0, 0)),
            pl.BlockSpec((tm, HEAD_DIM), lambda bi, ti: (ti, 0)),
            pl.BlockSpec((tm, HEAD_DIM), lambda bi, ti: (ti, 0)),
        ],
        out_specs=(row_spec(fw), row_spec(pw), row_spec(aw), row_spec(kvw), row_spec(kvw)),
        out_shape=out_shapes,
        compiler_params=_cparams(("parallel", "arbitrary")),
        name="inproj",
    )(x, mod3, norm_g, w_in_bf, qg, kg, cos_t, sin_t)


def _attn_kernel(*refs, n_src, group):
    q_ref = refs[0]
    k_refs = refs[1:1 + n_src]
    v_refs = refs[1 + n_src:1 + 2 * n_src]
    o_ref = refs[1 + 2 * n_src]
    dn = (((1,), (1,)), ((), ()))
    for gq in range(group):
        q = q_ref[0, :, gq * HEAD_DIM:(gq + 1) * HEAD_DIM]
        s = [lax.dot_general(q, k_ref[0], dn, preferred_element_type=F32) for k_ref in k_refs]
        m = s[0].max(axis=-1, keepdims=True)
        for si in s[1:]:
            m = jnp.maximum(m, si.max(axis=-1, keepdims=True))
        p = [jnp.exp(si - m) for si in s]
        l = p[0].sum(axis=-1, keepdims=True)
        for pi in p[1:]:
            l = l + pi.sum(axis=-1, keepdims=True)
        o = jnp.dot(p[0].astype(BF16), v_refs[0][0], preferred_element_type=F32)
        for pi, v_ref in zip(p[1:], v_refs[1:]):
            o = o + jnp.dot(pi.astype(BF16), v_ref[0], preferred_element_type=F32)
        o_ref[0, :, gq * HEAD_DIM:(gq + 1) * HEAD_DIM] = (o / l).astype(BF16)


def _attention(q, ks, vs):
    b, n, aw = q.shape
    group = aw // HEAD_DIM // N_KV_HEADS
    gw = group * HEAD_DIM
    tq = _row_tile(n, 512)
    n_src = len(ks)

    def kv_spec(a):
        return pl.BlockSpec((1, a.shape[1], HEAD_DIM), lambda bi, hi, qi: (bi, 0, hi))

    return pl.pallas_call(
        functools.partial(_attn_kernel, n_src=n_src, group=group),
        grid=(b, N_KV_HEADS, n // tq),
        in_specs=[pl.BlockSpec((1, tq, gw), lambda bi, hi, qi: (bi, qi, hi))]
        + [kv_spec(a) for a in ks] + [kv_spec(a) for a in vs],
        out_specs=pl.BlockSpec((1, tq, gw), lambda bi, hi, qi: (bi, qi, hi)),
        out_shape=jax.ShapeDtypeStruct((b, n, aw), BF16),
        compiler_params=_cparams(("parallel", "parallel", "arbitrary")),
        name="attention",
    )(q, *ks, *vs)


def _fourier_kernel(f_ref, cc_ref, sc_ref, cn_ref, sn_ref, w_ref, o_ref, xc_ref, xs_ref, *, scale):
    @pl.when(pl.program_id(1) == 0)
    def _():
        f = f_ref[0]
        xc_ref[...] = jnp.dot(f, cc_ref[...], preferred_element_type=F32).astype(BF16)
        xs_ref[...] = jnp.dot(f, sc_ref[...], preferred_element_type=F32).astype(BF16)

    fr = (jnp.dot(cn_ref[...], xc_ref[...], preferred_element_type=F32)
          - jnp.dot(sn_ref[...], xs_ref[...], preferred_element_type=F32)) * scale
    o_ref[0] = jnp.dot(fr.astype(BF16), w_ref[0], preferred_element_type=F32).astype(BF16)


def _dft_tables(n):
    k = jnp.arange(n, dtype=jnp.int32)
    ang = ((k[:, None] * k[None, :]) % n).astype(F32) * (2.0 * jnp.pi / n)
    return jnp.cos(ang), jnp.sin(ang)


def _fourier(f, layer, w_fourier_bf, cn, sn, cc_bd, sc_bd):
    b, n, fw = f.shape
    tm = _row_tile(n, 512)
    scale = float((n * (fw // N_FOURIER_HEADS)) ** -0.5)
    return pl.pallas_call(
        functools.partial(_fourier_kernel, scale=scale),
        grid=(b, n // tm),
        in_specs=[
            pl.BlockSpec((1, n, fw), lambda bi, ti: (bi, 0, 0)),
            pl.BlockSpec((fw, fw), lambda bi, ti: (0, 0)),
            pl.BlockSpec((fw, fw), lambda bi, ti: (0, 0)),
            pl.BlockSpec((tm, n), lambda bi, ti: (ti, 0)),
            pl.BlockSpec((tm, n), lambda bi, ti: (ti, 0)),
            pl.BlockSpec((1, fw, fw), lambda bi, ti: (layer, 0, 0)),
        ],
        out_specs=pl.BlockSpec((1, tm, fw), lambda bi, ti: (bi, ti, 0)),
        out_shape=jax.ShapeDtypeStruct((b, n, fw), BF16),
        scratch_shapes=[pltpu.VMEM((n, fw), BF16), pltpu.VMEM((n, fw), BF16)],
        compiler_params=_cparams(("parallel", "arbitrary")),
        name="fourier",
    )(f, cc_bd, sc_bd, cn, sn, w_fourier_bf)


POOL_HALO = 16
POOL_CHUNK = 128


def _pool_kernel(p_ref, w_ref, s_ref, o_ref, buf_ref, *, n, pw):
    gdim = pw // len(POOL_WINDOWS)
    zeros = jnp.zeros((POOL_HALO, pw), F32)
    buf_ref[0:POOL_HALO, :] = zeros
    buf_ref[POOL_HALO + n:POOL_HALO + n + POOL_HALO, :] = zeros
    buf_ref[POOL_HALO:POOL_HALO + n, :] = p_ref[0]

    lane = lax.broadcasted_iota(jnp.int32, (1, pw), 1)
    half = jnp.zeros((1, pw), jnp.int32)
    for gi, win in enumerate(POOL_WINDOWS):
        half = jnp.where((lane >= gi * gdim) & (lane < (gi + 1) * gdim), win // 2, half)
    max_half = max(POOL_WINDOWS) // 2
    ch = min(POOL_CHUNK, n)
    for c0 in range(0, n, ch):
        acc = jnp.zeros((ch, pw), F32)
        for s in range(-max_half, max_half):
            term = buf_ref[POOL_HALO + c0 + s:POOL_HALO + c0 + s + ch, :]
            inside = (s >= -half) & (s < half)
            acc = acc + jnp.where(inside, term, 0.0)
        t = c0 + lax.broadcasted_iota(jnp.int32, (ch, pw), 0)
        cnt = (jnp.minimum(t + half, n) - jnp.maximum(t - half, 0)).astype(F32)
        dlt = acc / cnt - buf_ref[POOL_HALO + c0:POOL_HALO + c0 + ch, :]
        y = jnp.dot(dlt.astype(BF16), w_ref[0], preferred_element_type=F32) * s_ref[0]
        o_ref[0, c0:c0 + ch, :] = y.astype(BF16)


def _pool(p, layer, w_pool_bd, pool_scale):
    b, n, pw = p.shape
    return pl.pallas_call(
        functools.partial(_pool_kernel, n=n, pw=pw),
        grid=(b,),
        in_specs=[
            pl.BlockSpec((1, n, pw), lambda bi: (bi, 0, 0)),
            pl.BlockSpec((1, pw, pw), lambda bi: (layer, 0, 0)),
            pl.BlockSpec((1, 1, pw), lambda bi: (layer, 0, 0)),
        ],
        out_specs=pl.BlockSpec((1, n, pw), lambda bi: (bi, 0, 0)),
        out_shape=jax.ShapeDtypeStruct((b, n, pw), BF16),
        scratch_shapes=[pltpu.VMEM((n + 2 * POOL_HALO, pw), F32)],
        compiler_params=_cparams(("parallel",)),
        name="pool",
    )(p, w_pool_bd, pool_scale)


def _outproj_kernel(fo_ref, po_ref, ao_ref, x_ref, mod_ref, g_ref, w_ref, wr_ref,
                    xn_ref, h_ref, aff_ref, *, d, fw, pw):
    w = w_ref[0]
    ox = jnp.dot(fo_ref[0], w[0:fw], preferred_element_type=F32)
    ox = ox + jnp.dot(po_ref[0], w[fw:fw + pw], preferred_element_type=F32)
    ox = ox + jnp.dot(ao_ref[0], w[fw + pw:], preferred_element_type=F32)
    mod = mod_ref[0]
    xn = x_ref[0] + mod[:, 2 * d:3 * d] * ox
    xn_ref[0] = xn
    h = _modulate(xn, g_ref[0], mod[:, 3 * d:4 * d], mod[:, 4 * d:5 * d])
    for c in range(d // LANES):
        h_ref[0, :, c, :] = h[:, c * LANES:(c + 1) * LANES]
    logits = jnp.dot(h.astype(BF16), wr_ref[0], preferred_element_type=F32)
    lane = lax.broadcasted_iota(jnp.int32, logits.shape, 1)
    logits = jnp.where(lane < N_EXPERTS, logits, -jnp.inf)
    e = jnp.exp(logits - logits.max(axis=-1, keepdims=True))
    aff = e / e.sum(axis=-1, keepdims=True)
    aff_ref[0] = aff.T[0:N_EXPERTS, :]


def _outproj(fo, po, ao, x, mod3, mod_row, layer, norm_g, w_out_bf, w_router_pad):
    b, n, d = x.shape
    fw = fo.shape[-1]
    pw = po.shape[-1]
    aw = ao.shape[-1]
    n6 = mod3.shape[-1]
    tm = _row_tile(n, 512)

    def row_spec(w):
        return pl.BlockSpec((1, tm, w), lambda bi, ti: (bi, ti, 0))

    return pl.pallas_call(
        functools.partial(_outproj_kernel, d=d, fw=fw, pw=pw),
        grid=(b, n // tm),
        in_specs=[
            row_spec(fw), row_spec(pw), row_spec(aw), row_spec(d),
            pl.BlockSpec((1, 1, n6), lambda bi, ti: (mod_row(bi), 0, 0)),
            pl.BlockSpec((1, 1, d), lambda bi, ti: (layer, 0, 0)),
            pl.BlockSpec((1, d, d), lambda bi, ti: (layer, 0, 0)),
            pl.BlockSpec((1, d, LANES), lambda bi, ti: (layer, 0, 0)),
        ],
        out_specs=(row_spec(d),
                   pl.BlockSpec((1, tm, d // LANES, LANES), lambda bi, ti: (bi, ti, 0, 0)),
                   pl.BlockSpec((1, N_EXPERTS, tm), lambda bi, ti: (bi, 0, ti))),
        out_shape=(jax.ShapeDtypeStruct((b, n, d), F32),
                   jax.ShapeDtypeStruct((b, n, d // LANES, LANES), F32),
                   jax.ShapeDtypeStruct((b, N_EXPERTS, n), F32)),
        compiler_params=_cparams(("parallel", "arbitrary")),
        name="outproj",
    )(fo, po, ao, x, mod3, norm_g, w_out_bf, w_router_pad)


ROUTE_BLOCK = 256
COMBINE_WINDOW = 64
NOT_IN_WINDOW = -(1 << 20)
LHS_ROWS = 16


def _combine_geometry(cap):
    win = min(COMBINE_WINDOW, cap)
    grp = min(N_EXPERTS, ROUTE_BLOCK // win)
    return win, grp


def _route_kernel(aff_ref, idx_ref, gate_ref, key_ref, relw_ref, soff_ref,
                  tri_ref, split_ref, lhs_ref, *, n, cap, blk, win, grp):
    nb = n // blk
    e_n = N_EXPERTS

    @pl.when(pl.program_id(0) == 0)
    def _():
        r = lax.broadcasted_iota(jnp.int32, (blk, blk), 0)
        c = lax.broadcasted_iota(jnp.int32, (blk, blk), 1)
        tri_ref[...] = jnp.where(r < c, 1.0, 0.0).astype(BF16)
        t = lax.broadcasted_iota(jnp.int32, (1, n), 1)
        lhs_ref[...] = jnp.zeros_like(lhs_ref)
        lhs_ref[0:1, :] = (t >> 8).astype(F32)
        lhs_ref[1:2, :] = (t & 255).astype(F32)

    aff = aff_ref[0]

    def search(i, thr_bits):
        cand = thr_bits | lax.shift_left(jnp.int32(1), 30 - i)
        cnt = jnp.sum(jnp.where(aff >= pltpu.bitcast(cand, F32), 1.0, 0.0), axis=-1, keepdims=True)
        return jnp.where(cnt >= cap, cand, thr_bits)

    thr = pltpu.bitcast(lax.fori_loop(0, 31, search, jnp.zeros((e_n, 1), jnp.int32)), F32)
    gt = jnp.where(aff > thr, 1.0, 0.0)
    eq = jnp.where(aff == thr, 1.0, 0.0)
    need = cap - jnp.sum(gt, axis=-1, keepdims=True)
    tri = tri_ref[...]

    sel = []
    off = jnp.zeros((e_n, 1), F32)
    for j in range(nb):
        eqj = eq[:, j * blk:(j + 1) * blk]
        rank = jnp.dot(eqj.astype(BF16), tri, preferred_element_type=F32) + off
        off = off + jnp.sum(eqj, axis=-1, keepdims=True)
        sel.append(jnp.maximum(gt[:, j * blk:(j + 1) * blk], jnp.where(rank < need, eqj, 0.0)))

    lane = lax.broadcasted_iota(jnp.int32, (e_n, LANES), 1)
    q_off = ((lax.broadcasted_iota(jnp.int32, (e_n, 1), 0) % grp) * win).astype(F32)
    soff = jnp.zeros((e_n, LANES), F32)
    off = jnp.zeros((e_n, 1), F32)
    for j in range(nb):
        selj = sel[j]
        pos = jnp.dot(selj.astype(BF16), tri, preferred_element_type=F32) + off
        soff = jnp.where(lane == j, off, soff)
        start = jnp.minimum(jnp.floor(off * (1.0 / 16.0)) * 16.0, float(cap - win))
        rel = pos - start
        chosen = selj > 0.0
        key_ref[0, :, j * blk:(j + 1) * blk] = jnp.where(chosen, pos, -1.0).astype(jnp.int32)
        relw_ref[0, :, j * blk:(j + 1) * blk] = jnp.where(
            chosen & (rel < win), rel + q_off, float(NOT_IN_WINDOW)).astype(jnp.int32)
        off = off + jnp.sum(selj, axis=-1, keepdims=True)
    soff_ref[0] = jnp.where(lane == nb, off, soff).astype(jnp.int32)

    a_hi = aff.astype(BF16).astype(F32)
    r1 = aff - a_hi
    a_mid = r1.astype(BF16).astype(F32)
    split_ref[0] = a_hi
    split_ref[1] = a_mid
    split_ref[2] = (r1 - a_mid).astype(BF16).astype(F32)
    slot = lax.broadcasted_iota(jnp.int32, (cap, n), 0)
    dn = (((1,), (1,)), ((), ()))

    def per_expert(e, carry):
        for k in range(3):
            lhs_ref[2 + k:3 + k, :] = split_ref[k, pl.ds(e, 1), :]
        onehot = jnp.where(key_ref[0, pl.ds(e, 1), :] == slot, 1.0, 0.0).astype(BF16)
        res = lax.dot_general(lhs_ref[...].astype(BF16), onehot, dn, preferred_element_type=F32)
        idx_ref[0, pl.ds(e, 1), :] = (res[0:1] * 256.0 + res[1:2]).astype(jnp.int32)
        gate_ref[0, pl.ds(e, 1), :] = res[2:3] + res[3:4] + res[4:5]
        return carry

    lax.fori_loop(0, e_n, per_expert, 0)


def _route(aff_t, cap):
    b, e, n = aff_t.shape
    blk = min(ROUTE_BLOCK, n)
    assert n % blk == 0 and n // blk < LANES
    win, grp = _combine_geometry(cap)

    def spec(w):
        return pl.BlockSpec((1, e, w), lambda bi: (bi, 0, 0))

    return pl.pallas_call(
        functools.partial(_route_kernel, n=n, cap=cap, blk=blk, win=win, grp=grp),
        grid=(b,),
        in_specs=[spec(n)],
        out_specs=(spec(cap), spec(cap), spec(n), spec(n), spec(LANES)),
        out_shape=(jax.ShapeDtypeStruct((b, e, cap), jnp.int32),
                   jax.ShapeDtypeStruct((b, e, cap), F32),
                   jax.ShapeDtypeStruct((b, e, n), jnp.int32),
                   jax.ShapeDtypeStruct((b, e, n), jnp.int32),
                   jax.ShapeDtypeStruct((b, e, LANES), jnp.int32)),
        scratch_shapes=[pltpu.VMEM((blk, blk), BF16), pltpu.VMEM((3, e, n), F32),
                        pltpu.VMEM((LHS_ROWS, n), F32)],
        compiler_params=_cparams(("arbitrary",)),
        name="moe_route",
    )(aff_t)


def _gather_kernel(idx_ref, h_ref, o_ref, rows_ref, *, cap, n_exp, d):
    def per_expert(e, carry):
        def body(r, c):
            rows_ref[r] = h_ref[0, idx_ref[0, 0, e * cap + r]]
            return c

        lax.fori_loop(0, cap, body, 0, unroll=16)
        tiles = jnp.swapaxes(rows_ref[...], 0, 1)
        for c in range(d // LANES):
            o_ref[0, e, :, c * LANES:(c + 1) * LANES] = tiles[c].astype(BF16)
        return carry

    lax.fori_loop(0, n_exp, per_expert, 0)


def _gather(h3, idx):
    b, n, s, l = h3.shape
    d = s * l
    _, e, cap = idx.shape
    return pl.pallas_call(
        functools.partial(_gather_kernel, cap=cap, n_exp=e, d=d),
        grid=(b,),
        in_specs=[
            pl.BlockSpec((1, 1, e * cap), lambda bi: (bi, 0, 0), memory_space=pltpu.SMEM),
            pl.BlockSpec((1, n, s, l), lambda bi: (bi, 0, 0, 0)),
        ],
        out_specs=pl.BlockSpec((1, e, cap, d), lambda bi: (bi, 0, 0, 0)),
        out_shape=jax.ShapeDtypeStruct((b, e, cap, d), BF16),
        scratch_shapes=[pltpu.VMEM((cap, s, l), F32)],
        compiler_params=_cparams(("arbitrary",)),
        name="moe_gather",
    )(idx.reshape(b, 1, e * cap), h3)


def _ffn_kernel(*refs, bt, n_sets):
    ins = refs[:2 * n_sets]
    wg_ref, wu_ref, wd_ref = refs[2 * n_sets:2 * n_sets + 3]
    outs = refs[2 * n_sets + 3:3 * n_sets + 3]
    wg_bf, wu_bf, wd_bf = refs[3 * n_sets + 3:]

    @pl.when(pl.program_id(1) == 0)
    def _():
        wg_bf[...] = wg_ref[0, 0].astype(BF16)
        wu_bf[...] = wu_ref[0, 0].astype(BF16)
        wd_bf[...] = wd_ref[0, 0].astype(BF16)

    def expert(xs, gate):
        a = jnp.dot(xs, wg_bf[...], preferred_element_type=F32)
        u = jnp.dot(xs, wu_bf[...], preferred_element_type=F32)
        hid = (a * jax.nn.sigmoid(a) * u).astype(BF16)
        return (jnp.dot(hid, wd_bf[...], preferred_element_type=F32) * gate).astype(BF16)

    for si in range(n_sets):
        xs_ref, g_ref, y_ref = ins[2 * si], ins[2 * si + 1], outs[si]
        cap, d = xs_ref.shape[2], xs_ref.shape[3]
        if cap >= 256:
            for bi in range(bt):
                y_ref[bi, 0] = expert(xs_ref[bi, 0], g_ref[bi, 0])
        else:
            xs = xs_ref[:, 0].reshape(bt * cap, d)
            y_ref[:, 0] = expert(xs, g_ref[:, 0].reshape(bt * cap, 1)).reshape(bt, cap, d)


def _ffn(sets, layer, w_gate, w_up, w_down):
    b, e, _, d = sets[0][0].shape
    ff = w_gate.shape[-1]
    bt = min(b, 4)
    assert b % bt == 0
    in_specs = []
    out_specs = []
    out_shape = []
    args = []
    for xs, gate in sets:
        cap = xs.shape[2]
        in_specs.append(pl.BlockSpec((bt, 1, cap, d), lambda ei, ji: (ji, ei, 0, 0)))
        in_specs.append(pl.BlockSpec((bt, 1, cap, 1), lambda ei, ji: (ji, ei, 0, 0)))
        out_specs.append(pl.BlockSpec((bt, 1, cap, d), lambda ei, ji: (ji, ei, 0, 0)))
        out_shape.append(jax.ShapeDtypeStruct(xs.shape, BF16))
        args += [xs, gate]
    in_specs += [
        pl.BlockSpec((1, 1, d, ff), lambda ei, ji: (layer, ei, 0, 0)),
        pl.BlockSpec((1, 1, d, ff), lambda ei, ji: (layer, ei, 0, 0)),
        pl.BlockSpec((1, 1, ff, d), lambda ei, ji: (layer, ei, 0, 0)),
    ]
    return pl.pallas_call(
        functools.partial(_ffn_kernel, bt=bt, n_sets=len(sets)),
        grid=(e, b // bt),
        in_specs=in_specs,
        out_specs=tuple(out_specs),
        out_shape=tuple(out_shape),
        scratch_shapes=[pltpu.VMEM((d, ff), BF16), pltpu.VMEM((d, ff), BF16), pltpu.VMEM((ff, d), BF16)],
        compiler_params=_cparams(("parallel", "arbitrary")),
        name="moe_ffn",
    )(*args, w_gate, w_up, w_down)


def _combine_kernel(soff_ref, relw_ref, key_ref, y_ref, xn_ref, mod_ref, o_ref, rhs_ref,
                    *, cap, d, n_exp, nb, win, grp):
    j = pl.program_id(1)
    stride = nb + 1
    blk = relw_ref.shape[-1]
    dn_t = (((0,), (0,)), ((), ()))

    def window_start(e):
        s = soff_ref[0, 0, e * stride + j]
        return s, jnp.minimum((s >> 4) << 4, cap - win)

    row = lax.broadcasted_iota(jnp.int32, (grp * win, blk), 0)
    acc = jnp.zeros((blk, d), F32)
    for g0 in range(0, n_exp, grp):
        slabs = []
        for q in range(grp):
            e = g0 + q
            _, start = window_start(e)
            rhs_ref[q * win:(q + 1) * win, :] = y_ref[0, e, pl.ds(pl.multiple_of(start, 16), win), :]
            slabs.append(jnp.broadcast_to(relw_ref[0, e:e + 1, :], (win, blk)))
        onehot = jnp.where(jnp.concatenate(slabs, axis=0) == row, 1.0, 0.0).astype(BF16)
        acc = acc + lax.dot_general(onehot, rhs_ref[...], dn_t, preferred_element_type=F32)
    o_ref[0] = acc

    row_w = lax.broadcasted_iota(jnp.int32, (win, blk), 0)

    def further_windows(e, carry):
        s, start0 = window_start(e)
        count = soff_ref[0, 0, e * stride + j + 1] - s
        n_win = (s - start0 + count + win - 1) // win

        def body(k, c):
            lo = start0 + k * win
            st = pl.multiple_of(jnp.minimum(lo, cap - win), 16)
            krow = key_ref[0, pl.ds(e, 1), :]
            onehot = jnp.where((krow - st == row_w) & (krow >= lo), 1.0, 0.0).astype(BF16)
            o_ref[0] = o_ref[0] + lax.dot_general(onehot, y_ref[0, e, pl.ds(st, win), :], dn_t,
                                                  preferred_element_type=F32)
            return c

        lax.fori_loop(1, n_win, body, 0)
        return carry

    lax.fori_loop(0, n_exp, further_windows, 0)
    o_ref[0] = xn_ref[0] + mod_ref[0][:, 5 * d:6 * d] * o_ref[0]


def _combine(y, soff, relw, key, xn, mod3, mod_row):
    b, e, cap, d = y.shape
    n = xn.shape[1]
    n6 = mod3.shape[-1]
    blk = min(ROUTE_BLOCK, n)
    nb = n // blk
    win, grp = _combine_geometry(cap)
    soff_s = soff[:, :, :nb + 1].reshape(b, 1, e * (nb + 1))
    return pl.pallas_call(
        functools.partial(_combine_kernel, cap=cap, d=d, n_exp=e, nb=nb, win=win, grp=grp),
        grid=(b, nb),
        in_specs=[
            pl.BlockSpec((1, 1, e * (nb + 1)), lambda bi, ji: (bi, 0, 0), memory_space=pltpu.SMEM),
            pl.BlockSpec((1, e, blk), lambda bi, ji: (bi, 0, ji)),
            pl.BlockSpec((1, e, blk), lambda bi, ji: (bi, 0, ji)),
            pl.BlockSpec((1, e, cap, d), lambda bi, ji: (bi, 0, 0, 0)),
            pl.BlockSpec((1, blk, d), lambda bi, ji: (bi, ji, 0)),
            pl.BlockSpec((1, 1, n6), lambda bi, ji: (mod_row(bi), 0, 0)),
        ],
        out_specs=pl.BlockSpec((1, blk, d), lambda bi, ji: (bi, ji, 0)),
        out_shape=jax.ShapeDtypeStruct((b, n, d), F32),
        scratch_shapes=[pltpu.VMEM((grp * win, d), BF16)],
        compiler_params=_cparams(("parallel", "arbitrary")),
        name="moe_combine",
    )(soff_s, relw, key, y, xn, mod3)


def _moe_dispatch(h3, aff_t):
    n = h3.shape[1]
    cap = EC_CAPACITY_FACTOR * n // N_EXPERTS
    idx, gate, key, relw, soff = _route(aff_t, cap)
    xs = _gather(h3, idx)
    return xs, gate.reshape(*gate.shape, 1), (soff, relw, key)


def _rope_tables(n):
    t = jnp.arange(n, dtype=jnp.int32)
    row = (t // GRID_W).astype(F32)
    col = (t % GRID_W).astype(F32)
    inv = ROPE_THETA ** (-jnp.arange(0, ROPE_HALF, 2, dtype=F32) / ROPE_HALF)
    ar = row[:, None] * inv[None, :]
    ac = col[:, None] * inv[None, :]
    ang = jnp.concatenate([ar, ar, ac, ac], axis=-1)
    lane = jnp.arange(HEAD_DIM)
    sign = jnp.where((lane % ROPE_HALF) < ROPE_HALF // 2, -1.0, 1.0).astype(F32)
    return jnp.cos(ang), jnp.sin(ang) * sign[None, :]


def _block_diag(blocks):
    g, a, c = blocks.shape
    eye = jnp.eye(g, dtype=blocks.dtype)
    return (eye[:, None, :, None] * blocks[:, :, None, :]).reshape(g * a, g * c)


def kernel(x, c, ctx, c_ctx, ada_w, ada_b, norm1_g, norm2_g, w_in, w_fourier, w_pool, pool_scale,
           q_norm_g, k_norm_g, w_out, w_router, w_gate, w_up, w_down):
    b, n, d = x.shape
    lc = ctx.shape[1]
    depth = ada_w.shape[0]
    fw = d // 4

    rows = -(-(b + 1) // SUBLANES) * SUBLANES
    cond = jnp.zeros((rows, d), F32).at[:b].set(c).at[b].set(c_ctx)
    mod = _ada_mod(cond, ada_w, ada_b)

    def lat_row(bi):
        return bi

    def ctx_row(bi):
        return b

    norm1_g = norm1_g.reshape(depth, 1, d)
    norm2_g = norm2_g.reshape(depth, 1, d)
    q_norm_g = q_norm_g.reshape(depth, 1, HEAD_DIM)
    k_norm_g = k_norm_g.reshape(depth, 1, HEAD_DIM)
    pool_scale = pool_scale.reshape(depth, 1, d // 4)
    w_in_bf = w_in.astype(BF16)
    w_out_bf = w_out.astype(BF16)
    w_fourier_bf = w_fourier.astype(BF16)
    w_router_pad = jnp.zeros((depth, d, LANES), BF16).at[:, :, :N_EXPERTS].set(w_router.astype(BF16))
    w_pool_bd = jax.vmap(_block_diag)(w_pool).astype(BF16)

    cos_t, sin_t = _rope_tables(n)
    cos_c, sin_c = _rope_tables(lc)
    cn, sn = (t.astype(BF16) for t in _dft_tables(n))
    cnc, snc = (t.astype(BF16) for t in _dft_tables(lc))
    hd = fw // N_FOURIER_HEADS
    cch, sch = _dft_tables(hd)
    cc_bd = _block_diag(jnp.broadcast_to(cch, (N_FOURIER_HEADS, hd, hd))).astype(BF16)
    sc_bd = _block_diag(jnp.broadcast_to(sch, (N_FOURIER_HEADS, hd, hd))).astype(BF16)

    for i in range(depth):
        last = i == depth - 1
        mod3 = mod[i].reshape(rows, 1, 6 * d)

        fx, px, qx, kx, vx = _inproj(x, mod3, lat_row, i, norm1_g, w_in_bf, q_norm_g, k_norm_g,
                                     cos_t, sin_t, True)
        fc, pc, qc, kc, vc = _inproj(ctx, mod3, ctx_row, i, norm1_g, w_in_bf, q_norm_g, k_norm_g,
                                     cos_c, sin_c, False)

        ax = _attention(qx, [kx, kc], [vx, vc])
        fox = _fourier(fx, i, w_fourier_bf, cn, sn, cc_bd, sc_bd)
        pox = _pool(px, i, w_pool_bd, pool_scale)
        xn, hx, affx = _outproj(fox, pox, ax, x, mod3, lat_row, i, norm2_g, w_out_bf, w_router_pad)
        xs, gx, route_x = _moe_dispatch(hx, affx)

        if last:
            (yx,) = _ffn([(xs, gx)], i, w_gate, w_up, w_down)
        else:
            ac = _attention(qc, [kc], [vc])
            foc = _fourier(fc, i, w_fourier_bf, cnc, snc, cc_bd, sc_bd)
            poc = _pool(pc, i, w_pool_bd, pool_scale)
            cn_, hc, affc = _outproj(foc, poc, ac, ctx, mod3, ctx_row, i, norm2_g, w_out_bf, w_router_pad)
            xsc, gc, route_c = _moe_dispatch(hc, affc)
            yx, yc = _ffn([(xs, gx), (xsc, gc)], i, w_gate, w_up, w_down)
            ctx = _combine(yc, *route_c, cn_, mod3, ctx_row)
        x = _combine(yx, *route_x, xn, mod3, lat_row)
    return x
```

```python
import functools

import jax
import jax.numpy as jnp
from jax import lax
from jax.experimental import pallas as pl
from jax.experimental.pallas import tpu as pltpu

GRID_W = 64
EPS = 1e-6
N_FOURIER_HEADS = 4
POOL_WINDOWS = (2, 4, 8, 16)
HEAD_DIM = 128
N_KV_HEADS = 2
ROPE_HALF = HEAD_DIM // 2
ROPE_THETA = 10000.0
N_EXPERTS = 16
EC_CAPACITY_FACTOR = 2

LANES = 128
SUBLANES = 8
VMEM_LIMIT_BYTES = 56 * 1024 * 1024

LOG2_E = 1.4426950408889634

F32 = jnp.float32
BF16 = jnp.bfloat16
HIGHEST = lax.Precision.HIGHEST


def _cparams(semantics):
    return pltpu.CompilerParams(dimension_semantics=semantics, vmem_limit_bytes=VMEM_LIMIT_BYTES)


def _row_tile(n, target):
    t = min(n, target)
    assert n % t == 0
    return t


def _ada_kernel(c_ref, w_ref, b_ref, o_ref):
    c = c_ref[...]
    s = c * jax.nn.sigmoid(c)
    o_ref[0] = jnp.dot(s, w_ref[0], preferred_element_type=F32, precision=HIGHEST) + b_ref[0]


def _ada_mod(cond, ada_w, ada_b):
    depth, d, n6 = ada_w.shape
    r = cond.shape[0]
    tn = _row_tile(n6, 1536)
    return pl.pallas_call(
        _ada_kernel,
        grid=(depth, n6 // tn),
        in_specs=[
            pl.BlockSpec((r, d), lambda i, j: (0, 0)),
            pl.BlockSpec((1, d, tn), lambda i, j: (i, 0, j)),
            pl.BlockSpec((1, 1, tn), lambda i, j: (i, 0, j)),
        ],
        out_specs=pl.BlockSpec((1, r, tn), lambda i, j: (i, 0, j)),
        out_shape=jax.ShapeDtypeStruct((depth, r, n6), F32),
        compiler_params=_cparams(("arbitrary", "arbitrary")),
        name="ada_mod",
    )(cond, ada_w, ada_b.reshape(depth, 1, n6))


def _modulate(xf, g, shift, scale):
    ms = jnp.mean(xf * xf, axis=-1, keepdims=True)
    y = xf * lax.rsqrt(ms + EPS) * g
    return y * (1.0 + scale) + shift


def _head_rms(xh, g):
    ms = jnp.mean(xh * xh, axis=-1, keepdims=True)
    return xh * lax.rsqrt(ms + EPS) * g


def _rope(xh, cos, sin_signed, lo_half):
    partner = jnp.where(lo_half, pltpu.roll(xh, HEAD_DIM - ROPE_HALF // 2, 1), pltpu.roll(xh, ROPE_HALF // 2, 1))
    return xh * cos + partner * sin_signed


def _inproj_kernel(x_ref, mod_ref, g_ref, w_ref, qg_ref, kg_ref, cos_ref, sin_ref,
                   f_ref, p_ref, q_ref, k_ref, vt_ref, *, d, fw, pw, aw, kvw, use_rope):
    xf = x_ref[0]
    mod = mod_ref[0]
    h = _modulate(xf, g_ref[0], mod[:, 0:d], mod[:, d:2 * d]).astype(BF16)
    pr = jnp.dot(h, w_ref[0], preferred_element_type=F32)
    f_ref[0] = pr[:, 0:fw].astype(BF16)
    p_ref[0] = pr[:, fw:fw + pw]
    q_off = fw + pw
    k_off = q_off + aw
    v_off = k_off + kvw
    if use_rope:
        cos = cos_ref[...]
        sin = sin_ref[...]
        lane = lax.broadcasted_iota(jnp.int32, cos.shape, 1)
        lo_half = (lane % ROPE_HALF) < (ROPE_HALF // 2)
    q_scale = HEAD_DIM ** -0.5 * LOG2_E
    for j in range(aw // HEAD_DIM):
        qh = _head_rms(pr[:, q_off + j * HEAD_DIM:q_off + (j + 1) * HEAD_DIM], qg_ref[0])
        if use_rope:
            qh = _rope(qh, cos, sin, lo_half)
        q_ref[0, :, j * HEAD_DIM:(j + 1) * HEAD_DIM] = (qh * q_scale).astype(BF16)
    for j in range(kvw // HEAD_DIM):
        kh = _head_rms(pr[:, k_off + j * HEAD_DIM:k_off + (j + 1) * HEAD_DIM], kg_ref[0])
        if use_rope:
            kh = _rope(kh, cos, sin, lo_half)
        k_ref[0, :, j * HEAD_DIM:(j + 1) * HEAD_DIM] = kh.astype(BF16)
    vt_ref[0] = pr[:, v_off:v_off + kvw].T.astype(BF16)


def _inproj(x, mod3, mod_row, layer, norm_g, w_in_bf, qg, kg, cos_t, sin_t, use_rope):
    b, n, d = x.shape
    in_w = w_in_bf.shape[-1]
    fw = d // 4
    pw = d // 4
    aw = d // 2
    kvw = N_KV_HEADS * HEAD_DIM
    tm = _row_tile(n, 512)
    n6 = mod3.shape[-1]
    kern = functools.partial(_inproj_kernel, d=d, fw=fw, pw=pw, aw=aw, kvw=kvw, use_rope=use_rope)
    out_shapes = (
        jax.ShapeDtypeStruct((b, n, fw), BF16),
        jax.ShapeDtypeStruct((b, n, pw), F32),
        jax.ShapeDtypeStruct((b, n, aw), BF16),
        jax.ShapeDtypeStruct((b, n, kvw), BF16),
        jax.ShapeDtypeStruct((b, kvw, n), BF16),
    )

    def row_spec(w):
        return pl.BlockSpec((1, tm, w), lambda bi, ti: (bi, ti, 0))

    return pl.pallas_call(
        kern,
        grid=(b, n // tm),
        in_specs=[
            pl.BlockSpec((1, tm, d), lambda bi, ti: (bi, ti, 0)),
            pl.BlockSpec((1, 1, n6), lambda bi, ti: (mod_row(bi), 0, 0)),
            pl.BlockSpec((1, 1, d), lambda bi, ti: (layer, 0, 0)),
            pl.BlockSpec((1, d, in_w), lambda bi, ti: (layer, 0, 0)),
            pl.BlockSpec((1, 1, HEAD_DIM), lambda bi, ti: (layer, 0, 0)),
            pl.BlockSpec((1, 1, HEAD_DIM), lambda bi, ti: (layer, 0, 0)),
            pl.BlockSpec((tm, HEAD_DIM), lambda bi, ti: (ti, 0)),
            pl.BlockSpec((tm, HEAD_DIM), lambda bi, ti: (ti, 0)),
        ],
        out_specs=(row_spec(fw), row_spec(pw), row_spec(aw), row_spec(kvw),
                   pl.BlockSpec((1, kvw, tm), lambda bi, ti: (bi, 0, ti))),
        out_shape=out_shapes,
        compiler_params=_cparams(("parallel", "arbitrary")),
        name="inproj",
    )(x, mod3, norm_g, w_in_bf, qg, kg, cos_t, sin_t)


ATTN_CHUNK = 256
ATTN_SLAB = 64


def _attn_kernel(q_ref, k_ref, vt_ref, o_ref, s0_ref, s1_ref, *, group, n_chunks, chunk):
    dn = (((1,), (1,)), ((), ()))

    def scores(c, s_ref):
        r0 = pl.multiple_of(c * chunk, chunk)
        for g in range(group):
            q = q_ref[0, pl.ds(r0, chunk), g * HEAD_DIM:(g + 1) * HEAD_DIM]
            s_ref[g] = lax.dot_general(k_ref[0], q, dn, preferred_element_type=F32)

    def finish(c, s_ref):
        r0 = pl.multiple_of(c * chunk, chunk)
        for g in range(group):
            s = s_ref[g]
            nk = s.shape[0]
            slab = ATTN_SLAB if nk % ATTN_SLAB == 0 else nk
            m = s.reshape(nk // slab, slab, chunk).max(axis=0).max(axis=0, keepdims=True)
            p = jnp.exp2(s - m)
            l = p.reshape(nk // slab, slab, chunk).sum(axis=0).sum(axis=0, keepdims=True)
            ot = jnp.dot(vt_ref[0], p.astype(BF16), preferred_element_type=F32)
            o_ref[0, pl.ds(r0, chunk), g * HEAD_DIM:(g + 1) * HEAD_DIM] = (ot / l).T.astype(BF16)

    scores(0, s0_ref)
    if n_chunks == 1:
        finish(0, s0_ref)
        return

    def body(i, carry):
        c = 2 * i
        scores(c + 1, s1_ref)
        finish(c, s0_ref)
        scores(c + 2, s0_ref)
        finish(c + 1, s1_ref)
        return carry

    lax.fori_loop(0, n_chunks // 2 - 1, body, 0)
    scores(n_chunks - 1, s1_ref)
    finish(n_chunks - 2, s0_ref)
    finish(n_chunks - 1, s1_ref)


def _attention(q, k_all, vt_all):
    b, n, aw = q.shape
    nk = k_all.shape[1]
    group = aw // HEAD_DIM // N_KV_HEADS
    gw = group * HEAD_DIM
    chunk = min(ATTN_CHUNK, n)
    n_chunks = n // chunk
    assert n % chunk == 0 and (n_chunks == 1 or n_chunks % 2 == 0)
    return pl.pallas_call(
        functools.partial(_attn_kernel, group=group, n_chunks=n_chunks, chunk=chunk),
        grid=(b, N_KV_HEADS),
        in_specs=[pl.BlockSpec((1, n, gw), lambda bi, hi: (bi, 0, hi)),
                  pl.BlockSpec((1, nk, HEAD_DIM), lambda bi, hi: (bi, 0, hi)),
                  pl.BlockSpec((1, HEAD_DIM, nk), lambda bi, hi: (bi, hi, 0))],
        out_specs=pl.BlockSpec((1, n, gw), lambda bi, hi: (bi, 0, hi)),
        out_shape=jax.ShapeDtypeStruct((b, n, aw), BF16),
        scratch_shapes=[pltpu.VMEM((group, nk, chunk), F32), pltpu.VMEM((group, nk, chunk), F32)],
        compiler_params=_cparams(("parallel", "arbitrary")),
        name="attention",
    )(q, k_all, vt_all)


def _fourier_kernel(f_ref, cc_ref, sc_ref, cn_ref, sn_ref, w_ref, o_ref, xc_ref, xs_ref, *, scale):
    @pl.when(pl.program_id(1) == 0)
    def _():
        f = f_ref[0]
        xc_ref[...] = jnp.dot(f, cc_ref[...], preferred_element_type=F32).astype(BF16)
        xs_ref[...] = jnp.dot(f, sc_ref[...], preferred_element_type=F32).astype(BF16)

    fr = (jnp.dot(cn_ref[...], xc_ref[...], preferred_element_type=F32)
          - jnp.dot(sn_ref[...], xs_ref[...], preferred_element_type=F32)) * scale
    o_ref[0] = jnp.dot(fr.astype(BF16), w_ref[0], preferred_element_type=F32).astype(BF16)


DFT_TABLE_MINOR = 16


def _dft_tables(n):
    k = jnp.arange(n, dtype=jnp.int32)

    def cos_sin(t):
        ang = ((k[:, None] * t[None, :]) % n).astype(F32) * (2.0 * jnp.pi / n)
        return jnp.cos(ang), jnp.sin(ang)

    if n <= 16 * DFT_TABLE_MINOR:
        return cos_sin(k)
    c1, s1 = cos_sin(jnp.arange(n // DFT_TABLE_MINOR, dtype=jnp.int32) * DFT_TABLE_MINOR)
    c0, s0 = cos_sin(jnp.arange(DFT_TABLE_MINOR, dtype=jnp.int32))
    cos = c1[:, :, None] * c0[:, None, :] - s1[:, :, None] * s0[:, None, :]
    sin = s1[:, :, None] * c0[:, None, :] + c1[:, :, None] * s0[:, None, :]
    return cos.reshape(n, n), sin.reshape(n, n)


def _fourier(f, layer, w_fourier_bf, cn, sn, cc_bd, sc_bd):
    b, n, fw = f.shape
    tm = _row_tile(n, 512)
    scale = float((n * (fw // N_FOURIER_HEADS)) ** -0.5)
    return pl.pallas_call(
        functools.partial(_fourier_kernel, scale=scale),
        grid=(b, n // tm),
        in_specs=[
            pl.BlockSpec((1, n, fw), lambda bi, ti: (bi, 0, 0)),
            pl.BlockSpec((fw, fw), lambda bi, ti: (0, 0)),
            pl.BlockSpec((fw, fw), lambda bi, ti: (0, 0)),
            pl.BlockSpec((tm, n), lambda bi, ti: (ti, 0)),
            pl.BlockSpec((tm, n), lambda bi, ti: (ti, 0)),
            pl.BlockSpec((1, fw, fw), lambda bi, ti: (layer, 0, 0)),
        ],
        out_specs=pl.BlockSpec((1, tm, fw), lambda bi, ti: (bi, ti, 0)),
        out_shape=jax.ShapeDtypeStruct((b, n, fw), BF16),
        scratch_shapes=[pltpu.VMEM((n, fw), BF16), pltpu.VMEM((n, fw), BF16)],
        compiler_params=_cparams(("parallel", "arbitrary")),
        name="fourier",
    )(f, cc_bd, sc_bd, cn, sn, w_fourier_bf)


POOL_HALO = 16
POOL_CHUNK = 128


def _pool_kernel(p_ref, w_ref, s_ref, o_ref, buf_ref, *, n, pw):
    gdim = pw // len(POOL_WINDOWS)
    zeros = jnp.zeros((POOL_HALO, pw), F32)
    buf_ref[0:POOL_HALO, :] = zeros
    buf_ref[POOL_HALO + n:POOL_HALO + n + POOL_HALO, :] = zeros
    buf_ref[POOL_HALO:POOL_HALO + n, :] = p_ref[0]

    lane = lax.broadcasted_iota(jnp.int32, (1, pw), 1)
    half = jnp.zeros((1, pw), jnp.int32)
    for gi, win in enumerate(POOL_WINDOWS):
        half = jnp.where((lane >= gi * gdim) & (lane < (gi + 1) * gdim), win // 2, half)
    max_half = max(POOL_WINDOWS) // 2
    ch = min(POOL_CHUNK, n)
    for c0 in range(0, n, ch):
        acc = jnp.zeros((ch, pw), F32)
        for s in range(-max_half, max_half):
            term = buf_ref[POOL_HALO + c0 + s:POOL_HALO + c0 + s + ch, :]
            inside = (s >= -half) & (s < half)
            acc = acc + jnp.where(inside, term, 0.0)
        t = c0 + lax.broadcasted_iota(jnp.int32, (ch, pw), 0)
        cnt = (jnp.minimum(t + half, n) - jnp.maximum(t - half, 0)).astype(F32)
        dlt = acc / cnt - buf_ref[POOL_HALO + c0:POOL_HALO + c0 + ch, :]
        y = jnp.dot(dlt.astype(BF16), w_ref[0], preferred_element_type=F32) * s_ref[0]
        o_ref[0, c0:c0 + ch, :] = y.astype(BF16)


def _pool(p, layer, w_pool_bd, pool_scale):
    b, n, pw = p.shape
    return pl.pallas_call(
        functools.partial(_pool_kernel, n=n, pw=pw),
        grid=(b,),
        in_specs=[
            pl.BlockSpec((1, n, pw), lambda bi: (bi, 0, 0)),
            pl.BlockSpec((1, pw, pw), lambda bi: (layer, 0, 0)),
            pl.BlockSpec((1, 1, pw), lambda bi: (layer, 0, 0)),
        ],
        out_specs=pl.BlockSpec((1, n, pw), lambda bi: (bi, 0, 0)),
        out_shape=jax.ShapeDtypeStruct((b, n, pw), BF16),
        scratch_shapes=[pltpu.VMEM((n + 2 * POOL_HALO, pw), F32)],
        compiler_params=_cparams(("parallel",)),
        name="pool",
    )(p, w_pool_bd, pool_scale)


def _outproj_kernel(fo_ref, po_ref, ao_ref, x_ref, mod_ref, g_ref, w_ref, wr_ref,
                    xn_ref, h_ref, aff_ref, *, d, fw, pw):
    w = w_ref[0]
    ox = jnp.dot(fo_ref[0], w[0:fw], preferred_element_type=F32)
    ox = ox + jnp.dot(po_ref[0], w[fw:fw + pw], preferred_element_type=F32)
    ox = ox + jnp.dot(ao_ref[0], w[fw + pw:], preferred_element_type=F32)
    mod = mod_ref[0]
    xn = x_ref[0] + mod[:, 2 * d:3 * d] * ox
    xn_ref[0] = xn
    h = _modulate(xn, g_ref[0], mod[:, 3 * d:4 * d], mod[:, 4 * d:5 * d])
    for c in range(d // LANES):
        h_ref[0, :, c, :] = h[:, c * LANES:(c + 1) * LANES]
    logits = jnp.dot(h.astype(BF16), wr_ref[0], preferred_element_type=F32)
    lane = lax.broadcasted_iota(jnp.int32, logits.shape, 1)
    logits = jnp.where(lane < N_EXPERTS, logits, -jnp.inf)
    e = jnp.exp(logits - logits.max(axis=-1, keepdims=True))
    aff = e / e.sum(axis=-1, keepdims=True)
    aff_ref[0] = aff.T[0:N_EXPERTS, :]


def _outproj(fo, po, ao, x, mod3, mod_row, layer, norm_g, w_out_bf, w_router_pad):
    b, n, d = x.shape
    fw = fo.shape[-1]
    pw = po.shape[-1]
    aw = ao.shape[-1]
    n6 = mod3.shape[-1]
    tm = _row_tile(n, 512)

    def row_spec(w):
        return pl.BlockSpec((1, tm, w), lambda bi, ti: (bi, ti, 0))

    return pl.pallas_call(
        functools.partial(_outproj_kernel, d=d, fw=fw, pw=pw),
        grid=(b, n // tm),
        in_specs=[
            row_spec(fw), row_spec(pw), row_spec(aw), row_spec(d),
            pl.BlockSpec((1, 1, n6), lambda bi, ti: (mod_row(bi), 0, 0)),
            pl.BlockSpec((1, 1, d), lambda bi, ti: (layer, 0, 0)),
            pl.BlockSpec((1, d, d), lambda bi, ti: (layer, 0, 0)),
            pl.BlockSpec((1, d, LANES), lambda bi, ti: (layer, 0, 0)),
        ],
        out_specs=(row_spec(d),
                   pl.BlockSpec((1, tm, d // LANES, LANES), lambda bi, ti: (bi, ti, 0, 0)),
                   pl.BlockSpec((1, N_EXPERTS, tm), lambda bi, ti: (bi, 0, ti))),
        out_shape=(jax.ShapeDtypeStruct((b, n, d), F32),
                   jax.ShapeDtypeStruct((b, n, d // LANES, LANES), F32),
                   jax.ShapeDtypeStruct((b, N_EXPERTS, n), F32)),
        compiler_params=_cparams(("parallel", "arbitrary")),
        name="outproj",
    )(fo, po, ao, x, mod3, norm_g, w_out_bf, w_router_pad)


ROUTE_BLOCK = 256
COMBINE_WINDOW = 64
NOT_IN_WINDOW = -(1 << 20)
LHS_ROWS = 16


def _combine_geometry(cap):
    win = min(COMBINE_WINDOW, cap)
    grp = min(N_EXPERTS, ROUTE_BLOCK // win)
    return win, grp


def _route_kernel(aff_ref, idx_ref, gate_ref, key_ref, relw_ref, soff_ref,
                  tri_ref, split_ref, lhs_ref, *, n, cap, blk, win, grp):
    nb = n // blk
    e_n = N_EXPERTS

    @pl.when(pl.program_id(0) == 0)
    def _():
        r = lax.broadcasted_iota(jnp.int32, (blk, blk), 0)
        c = lax.broadcasted_iota(jnp.int32, (blk, blk), 1)
        tri_ref[...] = jnp.where(r < c, 1.0, 0.0).astype(BF16)
        t = lax.broadcasted_iota(jnp.int32, (1, n), 1)
        lhs_ref[...] = jnp.zeros_like(lhs_ref)
        lhs_ref[0:1, :] = (t >> 8).astype(F32)
        lhs_ref[1:2, :] = (t & 255).astype(F32)

    aff = aff_ref[0]

    def search(i, thr_bits):
        cand = thr_bits | lax.shift_left(jnp.int32(1), 30 - i)
        cnt = jnp.sum(jnp.where(aff >= pltpu.bitcast(cand, F32), 1.0, 0.0), axis=-1, keepdims=True)
        return jnp.where(cnt >= cap, cand, thr_bits)

    thr = pltpu.bitcast(lax.fori_loop(0, 31, search, jnp.zeros((e_n, 1), jnp.int32)), F32)
    gt = jnp.where(aff > thr, 1.0, 0.0)
    eq = jnp.where(aff == thr, 1.0, 0.0)
    need = cap - jnp.sum(gt, axis=-1, keepdims=True)
    tri = tri_ref[...]

    sel = []
    off = jnp.zeros((e_n, 1), F32)
    for j in range(nb):
        eqj = eq[:, j * blk:(j + 1) * blk]
        rank = jnp.dot(eqj.astype(BF16), tri, preferred_element_type=F32) + off
        off = off + jnp.sum(eqj, axis=-1, keepdims=True)
        sel.append(jnp.maximum(gt[:, j * blk:(j + 1) * blk], jnp.where(rank < need, eqj, 0.0)))

    lane = lax.broadcasted_iota(jnp.int32, (e_n, LANES), 1)
    q_off = ((lax.broadcasted_iota(jnp.int32, (e_n, 1), 0) % grp) * win).astype(F32)
    soff = jnp.zeros((e_n, LANES), F32)
    off = jnp.zeros((e_n, 1), F32)
    for j in range(nb):
        selj = sel[j]
        pos = jnp.dot(selj.astype(BF16), tri, preferred_element_type=F32) + off
        soff = jnp.where(lane == j, off, soff)
        start = jnp.minimum(jnp.floor(off * (1.0 / 16.0)) * 16.0, float(cap - win))
        rel = pos - start
        chosen = selj > 0.0
        key_ref[0, :, j * blk:(j + 1) * blk] = jnp.where(chosen, pos, -1.0).astype(jnp.int32)
        relw_ref[0, :, j * blk:(j + 1) * blk] = jnp.where(
            chosen & (rel < win), rel + q_off, float(NOT_IN_WINDOW)).astype(jnp.int32)
        off = off + jnp.sum(selj, axis=-1, keepdims=True)
    soff_ref[0] = jnp.where(lane == nb, off, soff).astype(jnp.int32)

    a_hi = aff.astype(BF16).astype(F32)
    r1 = aff - a_hi
    a_mid = r1.astype(BF16).astype(F32)
    split_ref[0] = a_hi
    split_ref[1] = a_mid
    split_ref[2] = (r1 - a_mid).astype(BF16).astype(F32)
    slot = lax.broadcasted_iota(jnp.int32, (cap, n), 0)
    dn = (((1,), (1,)), ((), ()))

    def per_expert(e, carry):
        for k in range(3):
            lhs_ref[2 + k:3 + k, :] = split_ref[k, pl.ds(e, 1), :]
        onehot = jnp.where(key_ref[0, pl.ds(e, 1), :] == slot, 1.0, 0.0).astype(BF16)
        res = lax.dot_general(lhs_ref[...].astype(BF16), onehot, dn, preferred_element_type=F32)
        idx_ref[0, pl.ds(e, 1), :] = (res[0:1] * 256.0 + res[1:2]).astype(jnp.int32)
        gate_ref[0, pl.ds(e, 1), :] = res[2:3] + res[3:4] + res[4:5]
        return carry

    lax.fori_loop(0, e_n, per_expert, 0)


def _route(aff_t, cap):
    b, e, n = aff_t.shape
    blk = min(ROUTE_BLOCK, n)
    assert n % blk == 0 and n // blk < LANES
    win, grp = _combine_geometry(cap)

    def spec(w):
        return pl.BlockSpec((1, e, w), lambda bi: (bi, 0, 0))

    return pl.pallas_call(
        functools.partial(_route_kernel, n=n, cap=cap, blk=blk, win=win, grp=grp),
        grid=(b,),
        in_specs=[spec(n)],
        out_specs=(spec(cap), spec(cap), spec(n), spec(n), spec(LANES)),
        out_shape=(jax.ShapeDtypeStruct((b, e, cap), jnp.int32),
                   jax.ShapeDtypeStruct((b, e, cap), F32),
                   jax.ShapeDtypeStruct((b, e, n), jnp.int32),
                   jax.ShapeDtypeStruct((b, e, n), jnp.int32),
                   jax.ShapeDtypeStruct((b, e, LANES), jnp.int32)),
        scratch_shapes=[pltpu.VMEM((blk, blk), BF16), pltpu.VMEM((3, e, n), F32),
                        pltpu.VMEM((LHS_ROWS, n), F32)],
        compiler_params=_cparams(("arbitrary",)),
        name="moe_route",
    )(aff_t)


def _gather_kernel(idx_ref, h_ref, o_ref, rows_ref, *, cap, n_exp, d):
    def per_expert(e, carry):
        def body(r, c):
            rows_ref[r] = h_ref[0, idx_ref[0, 0, e * cap + r]]
            return c

        lax.fori_loop(0, cap, body, 0, unroll=16)
        tiles = jnp.swapaxes(rows_ref[...], 0, 1)
        for c in range(d // LANES):
            o_ref[0, e, :, c * LANES:(c + 1) * LANES] = tiles[c].astype(BF16)
        return carry

    lax.fori_loop(0, n_exp, per_expert, 0)


def _gather(h3, idx):
    b, n, s, l = h3.shape
    d = s * l
    _, e, cap = idx.shape
    return pl.pallas_call(
        functools.partial(_gather_kernel, cap=cap, n_exp=e, d=d),
        grid=(b,),
        in_specs=[
            pl.BlockSpec((1, 1, e * cap), lambda bi: (bi, 0, 0), memory_space=pltpu.SMEM),
            pl.BlockSpec((1, n, s, l), lambda bi: (bi, 0, 0, 0)),
        ],
        out_specs=pl.BlockSpec((1, e, cap, d), lambda bi: (bi, 0, 0, 0)),
        out_shape=jax.ShapeDtypeStruct((b, e, cap, d), BF16),
        scratch_shapes=[pltpu.VMEM((cap, s, l), F32)],
        compiler_params=_cparams(("arbitrary",)),
        name="moe_gather",
    )(idx.reshape(b, 1, e * cap), h3)


def _ffn_kernel(*refs, bt, n_sets):
    ins = refs[:2 * n_sets]
    wg_ref, wu_ref, wd_ref = refs[2 * n_sets:2 * n_sets + 3]
    outs = refs[2 * n_sets + 3:3 * n_sets + 3]
    wg_bf, wu_bf, wd_bf = refs[3 * n_sets + 3:]

    @pl.when(pl.program_id(1) == 0)
    def _():
        wg_bf[...] = wg_ref[0, 0].astype(BF16)
        wu_bf[...] = wu_ref[0, 0].astype(BF16)
        wd_bf[...] = wd_ref[0, 0].astype(BF16)

    def expert(xs, gate):
        a = jnp.dot(xs, wg_bf[...], preferred_element_type=F32)
        u = jnp.dot(xs, wu_bf[...], preferred_element_type=F32)
        hid = (a * jax.nn.sigmoid(a) * u).astype(BF16)
        return (jnp.dot(hid, wd_bf[...], preferred_element_type=F32) * gate).astype(BF16)

    for si in range(n_sets):
        xs_ref, g_ref, y_ref = ins[2 * si], ins[2 * si + 1], outs[si]
        cap, d = xs_ref.shape[2], xs_ref.shape[3]
        if cap >= 256:
            for bi in range(bt):
                y_ref[bi, 0] = expert(xs_ref[bi, 0], g_ref[bi, 0])
        else:
            xs = xs_ref[:, 0].reshape(bt * cap, d)
            y_ref[:, 0] = expert(xs, g_ref[:, 0].reshape(bt * cap, 1)).reshape(bt, cap, d)


def _ffn(sets, layer, w_gate, w_up, w_down):
    b, e, _, d = sets[0][0].shape
    ff = w_gate.shape[-1]
    bt = min(b, 4)
    assert b % bt == 0
    in_specs = []
    out_specs = []
    out_shape = []
    args = []
    for xs, gate in sets:
        cap = xs.shape[2]
        in_specs.append(pl.BlockSpec((bt, 1, cap, d), lambda ei, ji: (ji, ei, 0, 0)))
        in_specs.append(pl.BlockSpec((bt, 1, cap, 1), lambda ei, ji: (ji, ei, 0, 0)))
        out_specs.append(pl.BlockSpec((bt, 1, cap, d), lambda ei, ji: (ji, ei, 0, 0)))
        out_shape.append(jax.ShapeDtypeStruct(xs.shape, BF16))
        args += [xs, gate]
    in_specs += [
        pl.BlockSpec((1, 1, d, ff), lambda ei, ji: (layer, ei, 0, 0)),
        pl.BlockSpec((1, 1, d, ff), lambda ei, ji: (layer, ei, 0, 0)),
        pl.BlockSpec((1, 1, ff, d), lambda ei, ji: (layer, ei, 0, 0)),
    ]
    return pl.pallas_call(
        functools.partial(_ffn_kernel, bt=bt, n_sets=len(sets)),
        grid=(e, b // bt),
        in_specs=in_specs,
        out_specs=tuple(out_specs),
        out_shape=tuple(out_shape),
        scratch_shapes=[pltpu.VMEM((d, ff), BF16), pltpu.VMEM((d, ff), BF16), pltpu.VMEM((ff, d), BF16)],
        compiler_params=_cparams(("parallel", "arbitrary")),
        name="moe_ffn",
    )(*args, w_gate, w_up, w_down)


def _combine_kernel(soff_ref, relw_ref, key_ref, y_ref, xn_ref, mod_ref, o_ref, rhs_ref,
                    *, cap, d, n_exp, nb, win, grp):
    j = pl.program_id(1)
    stride = nb + 1
    blk = relw_ref.shape[-1]
    dn_t = (((0,), (0,)), ((), ()))

    def window_start(e):
        s = soff_ref[0, 0, e * stride + j]
        return s, jnp.minimum((s >> 4) << 4, cap - win)

    row = lax.broadcasted_iota(jnp.int32, (grp * win, blk), 0)
    acc = jnp.zeros((blk, d), F32)
    for g0 in range(0, n_exp, grp):
        slabs = []
        for q in range(grp):
            e = g0 + q
            _, start = window_start(e)
            rhs_ref[q * win:(q + 1) * win, :] = y_ref[0, e, pl.ds(pl.multiple_of(start, 16), win), :]
            slabs.append(jnp.broadcast_to(relw_ref[0, e:e + 1, :], (win, blk)))
        onehot = jnp.where(jnp.concatenate(slabs, axis=0) == row, 1.0, 0.0).astype(BF16)
        acc = acc + lax.dot_general(onehot, rhs_ref[...], dn_t, preferred_element_type=F32)
    o_ref[0] = acc

    row_w = lax.broadcasted_iota(jnp.int32, (win, blk), 0)

    def further_windows(e, carry):
        s, start0 = window_start(e)
        count = soff_ref[0, 0, e * stride + j + 1] - s
        n_win = (s - start0 + count + win - 1) // win

        def body(k, c):
            lo = start0 + k * win
            st = pl.multiple_of(jnp.minimum(lo, cap - win), 16)
            krow = key_ref[0, pl.ds(e, 1), :]
            onehot = jnp.where((krow - st == row_w) & (krow >= lo), 1.0, 0.0).astype(BF16)
            o_ref[0] = o_ref[0] + lax.dot_general(onehot, y_ref[0, e, pl.ds(st, win), :], dn_t,
                                                  preferred_element_type=F32)
            return c

        lax.fori_loop(1, n_win, body, 0)
        return carry

    lax.fori_loop(0, n_exp, further_windows, 0)
    o_ref[0] = xn_ref[0] + mod_ref[0][:, 5 * d:6 * d] * o_ref[0]


def _combine(y, soff, relw, key, xn, mod3, mod_row):
    b, e, cap, d = y.shape
    n = xn.shape[1]
    n6 = mod3.shape[-1]
    blk = min(ROUTE_BLOCK, n)
    nb = n // blk
    win, grp = _combine_geometry(cap)
    soff_s = soff[:, :, :nb + 1].reshape(b, 1, e * (nb + 1))
    return pl.pallas_call(
        functools.partial(_combine_kernel, cap=cap, d=d, n_exp=e, nb=nb, win=win, grp=grp),
        grid=(b, nb),
        in_specs=[
            pl.BlockSpec((1, 1, e * (nb + 1)), lambda bi, ji: (bi, 0, 0), memory_space=pltpu.SMEM),
            pl.BlockSpec((1, e, blk), lambda bi, ji: (bi, 0, ji)),
            pl.BlockSpec((1, e, blk), lambda bi, ji: (bi, 0, ji)),
            pl.BlockSpec((1, e, cap, d), lambda bi, ji: (bi, 0, 0, 0)),
            pl.BlockSpec((1, blk, d), lambda bi, ji: (bi, ji, 0)),
            pl.BlockSpec((1, 1, n6), lambda bi, ji: (mod_row(bi), 0, 0)),
        ],
        out_specs=pl.BlockSpec((1, blk, d), lambda bi, ji: (bi, ji, 0)),
        out_shape=jax.ShapeDtypeStruct((b, n, d), F32),
        scratch_shapes=[pltpu.VMEM((grp * win, d), BF16)],
        compiler_params=_cparams(("parallel", "arbitrary")),
        name="moe_combine",
    )(soff_s, relw, key, y, xn, mod3)


def _moe_dispatch(h3, aff_t):
    n = h3.shape[1]
    cap = EC_CAPACITY_FACTOR * n // N_EXPERTS
    idx, gate, key, relw, soff = _route(aff_t, cap)
    xs = _gather(h3, idx)
    return xs, gate.reshape(*gate.shape, 1), (soff, relw, key)


def _rope_tables(n):
    t = jnp.arange(n, dtype=jnp.int32)
    row = (t // GRID_W).astype(F32)
    col = (t % GRID_W).astype(F32)
    inv = ROPE_THETA ** (-jnp.arange(0, ROPE_HALF, 2, dtype=F32) / ROPE_HALF)
    ar = row[:, None] * inv[None, :]
    ac = col[:, None] * inv[None, :]
    ang = jnp.concatenate([ar, ar, ac, ac], axis=-1)
    lane = jnp.arange(HEAD_DIM)
    sign = jnp.where((lane % ROPE_HALF) < ROPE_HALF // 2, -1.0, 1.0).astype(F32)
    return jnp.cos(ang), jnp.sin(ang) * sign[None, :]


def _block_diag(blocks):
    g, a, c = blocks.shape
    eye = jnp.eye(g, dtype=blocks.dtype)
    return (eye[:, None, :, None] * blocks[:, :, None, :]).reshape(g * a, g * c)


def kernel(x, c, ctx, c_ctx, ada_w, ada_b, norm1_g, norm2_g, w_in, w_fourier, w_pool, pool_scale,
           q_norm_g, k_norm_g, w_out, w_router, w_gate, w_up, w_down):
    b, n, d = x.shape
    lc = ctx.shape[1]
    depth = ada_w.shape[0]
    fw = d // 4

    rows = -(-(b + 1) // SUBLANES) * SUBLANES
    cond = jnp.zeros((rows, d), F32).at[:b].set(c).at[b].set(c_ctx)
    mod = _ada_mod(cond, ada_w, ada_b)

    def lat_row(bi):
        return bi

    def ctx_row(bi):
        return b

    norm1_g = norm1_g.reshape(depth, 1, d)
    norm2_g = norm2_g.reshape(depth, 1, d)
    q_norm_g = q_norm_g.reshape(depth, 1, HEAD_DIM)
    k_norm_g = k_norm_g.reshape(depth, 1, HEAD_DIM)
    pool_scale = pool_scale.reshape(depth, 1, d // 4)
    w_in_bf = w_in.astype(BF16)
    w_out_bf = w_out.astype(BF16)
    w_fourier_bf = w_fourier.astype(BF16)
    w_router_pad = jnp.zeros((depth, d, LANES), BF16).at[:, :, :N_EXPERTS].set(w_router.astype(BF16))
    w_pool_bd = jax.vmap(_block_diag)(w_pool).astype(BF16)

    cos_t, sin_t = _rope_tables(n)
    cos_c, sin_c = _rope_tables(lc)
    cn, sn = (t.astype(BF16) for t in _dft_tables(n))
    cnc, snc = (t.astype(BF16) for t in _dft_tables(lc))
    hd = fw // N_FOURIER_HEADS
    cch, sch = _dft_tables(hd)
    cc_bd = _block_diag(jnp.broadcast_to(cch, (N_FOURIER_HEADS, hd, hd))).astype(BF16)
    sc_bd = _block_diag(jnp.broadcast_to(sch, (N_FOURIER_HEADS, hd, hd))).astype(BF16)

    for i in range(depth):
        last = i == depth - 1
        mod3 = mod[i].reshape(rows, 1, 6 * d)

        fx, px, qx, kx, vx = _inproj(x, mod3, lat_row, i, norm1_g, w_in_bf, q_norm_g, k_norm_g,
                                     cos_t, sin_t, True)
        fc, pc, qc, kc, vc = _inproj(ctx, mod3, ctx_row, i, norm1_g, w_in_bf, q_norm_g, k_norm_g,
                                     cos_c, sin_c, False)

        ax = _attention(qx, jnp.concatenate([kx, kc], axis=1), jnp.concatenate([vx, vc], axis=2))
        fox = _fourier(fx, i, w_fourier_bf, cn, sn, cc_bd, sc_bd)
        pox = _pool(px, i, w_pool_bd, pool_scale)
        xn, hx, affx = _outproj(fox, pox, ax, x, mod3, lat_row, i, norm2_g, w_out_bf, w_router_pad)
        xs, gx, route_x = _moe_dispatch(hx, affx)

        if last:
            (yx,) = _ffn([(xs, gx)], i, w_gate, w_up, w_down)
        else:
            ac = _attention(qc, kc, vc)
            foc = _fourier(fc, i, w_fourier_bf, cnc, snc, cc_bd, sc_bd)
            poc = _pool(pc, i, w_pool_bd, pool_scale)
            cn_, hc, affc = _outproj(foc, poc, ac, ctx, mod3, ctx_row, i, norm2_g, w_out_bf, w_router_pad)
            xsc, gc, route_c = _moe_dispatch(hc, affc)
            yx, yc = _ffn([(xs, gx), (xsc, gc)], i, w_gate, w_up, w_down)
            ctx = _combine(yc, *route_c, cn_, mod3, ctx_row)
        x = _combine(yx, *route_x, xn, mod3, lat_row)
    return x
```

```python
import functools

import jax
import jax.numpy as jnp
from jax import lax
from jax.experimental import pallas as pl
from jax.experimental.pallas import tpu as pltpu

GRID_W = 64
EPS = 1e-6
N_FOURIER_HEADS = 4
POOL_WINDOWS = (2, 4, 8, 16)
HEAD_DIM = 128
N_KV_HEADS = 2
ROPE_HALF = HEAD_DIM // 2
ROPE_THETA = 10000.0
N_EXPERTS = 16
EC_CAPACITY_FACTOR = 2

LANES = 128
SUBLANES = 8
VMEM_LIMIT_BYTES = 56 * 1024 * 1024

LOG2_E = 1.4426950408889634

F32 = jnp.float32
BF16 = jnp.bfloat16
HIGHEST = lax.Precision.HIGHEST


def _cparams(semantics):
    return pltpu.CompilerParams(dimension_semantics=semantics, vmem_limit_bytes=VMEM_LIMIT_BYTES)


def _row_tile(n, target):
    t = min(n, target)
    assert n % t == 0
    return t


def _ada_kernel(c_ref, w_ref, b_ref, o_ref):
    c = c_ref[...]
    s = c * jax.nn.sigmoid(c)
    o_ref[0] = jnp.dot(s, w_ref[0], preferred_element_type=F32, precision=HIGHEST) + b_ref[0]


def _ada_mod(cond, ada_w, ada_b):
    depth, d, n6 = ada_w.shape
    r = cond.shape[0]
    tn = _row_tile(n6, 1536)
    return pl.pallas_call(
        _ada_kernel,
        grid=(depth, n6 // tn),
        in_specs=[
            pl.BlockSpec((r, d), lambda i, j: (0, 0)),
            pl.BlockSpec((1, d, tn), lambda i, j: (i, 0, j)),
            pl.BlockSpec((1, 1, tn), lambda i, j: (i, 0, j)),
        ],
        out_specs=pl.BlockSpec((1, r, tn), lambda i, j: (i, 0, j)),
        out_shape=jax.ShapeDtypeStruct((depth, r, n6), F32),
        compiler_params=_cparams(("arbitrary", "arbitrary")),
        name="ada_mod",
    )(cond, ada_w, ada_b.reshape(depth, 1, n6))


def _modulate(xf, g, shift, scale):
    ms = jnp.mean(xf * xf, axis=-1, keepdims=True)
    y = xf * lax.rsqrt(ms + EPS) * g
    return y * (1.0 + scale) + shift


def _head_rms(xh, g):
    ms = jnp.mean(xh * xh, axis=-1, keepdims=True)
    return xh * lax.rsqrt(ms + EPS) * g


def _rope(xh, cos, sin_signed, lo_half):
    partner = jnp.where(lo_half, pltpu.roll(xh, HEAD_DIM - ROPE_HALF // 2, 1), pltpu.roll(xh, ROPE_HALF // 2, 1))
    return xh * cos + partner * sin_signed


def _inproj_kernel(*refs, d, fw, pw, aw, kvw, use_rope, n_alias):
    x_ref, mod_ref, g_ref, w_ref, qg_ref, kg_ref, cos_ref, sin_ref = refs[:8]
    f_ref, p_ref, q_ref, k_ref, vt_ref, pr0_ref, pr1_ref = refs[8 + n_alias:]
    step = pl.program_id(0)

    @pl.when(step == 0)
    def _():
        pr1_ref[...] = jnp.zeros_like(pr1_ref)

    def project(dst_ref):
        mod = mod_ref[0]
        h = _modulate(x_ref[0], g_ref[0], mod[:, 0:d], mod[:, d:2 * d]).astype(BF16)
        dst_ref[...] = jnp.dot(h, w_ref[0], preferred_element_type=F32)

    def emit(src_ref):
        f_ref[0] = src_ref[:, 0:fw].astype(BF16)
        p_ref[0] = src_ref[:, fw:fw + pw]
        q_off = fw + pw
        k_off = q_off + aw
        v_off = k_off + kvw
        if use_rope:
            cos = cos_ref[...]
            sin = sin_ref[...]
            lane = lax.broadcasted_iota(jnp.int32, cos.shape, 1)
            lo_half = (lane % ROPE_HALF) < (ROPE_HALF // 2)
        q_scale = HEAD_DIM ** -0.5 * LOG2_E
        for j in range(aw // HEAD_DIM):
            qh = _head_rms(src_ref[:, q_off + j * HEAD_DIM:q_off + (j + 1) * HEAD_DIM], qg_ref[0])
            if use_rope:
                qh = _rope(qh, cos, sin, lo_half)
            q_ref[0, :, j * HEAD_DIM:(j + 1) * HEAD_DIM] = (qh * q_scale).astype(BF16)
        for j in range(kvw // HEAD_DIM):
            kh = _head_rms(src_ref[:, k_off + j * HEAD_DIM:k_off + (j + 1) * HEAD_DIM], kg_ref[0])
            if use_rope:
                kh = _rope(kh, cos, sin, lo_half)
            k_ref[0, :, j * HEAD_DIM:(j + 1) * HEAD_DIM] = kh.astype(BF16)
        vt_ref[0] = src_ref[:, v_off:v_off + kvw].T.astype(BF16)

    @pl.when(step % 2 == 0)
    def _():
        project(pr0_ref)
        emit(pr1_ref)

    @pl.when(step % 2 == 1)
    def _():
        project(pr1_ref)
        emit(pr0_ref)


def _inproj(x, mod3, mod_row, layer, norm_g, w_in_bf, qg, kg, cos_t, sin_t, use_rope, nk, key_row0, kv_dst=None):
    b, n, d = x.shape
    in_w = w_in_bf.shape[-1]
    fw = d // 4
    pw = d // 4
    aw = d // 2
    kvw = N_KV_HEADS * HEAD_DIM
    tm = _row_tile(n, 512)
    nt = n // tm
    total = b * nt
    assert key_row0 % tm == 0
    kt0 = key_row0 // tm
    n6 = mod3.shape[-1]
    n_alias = 0 if kv_dst is None else 2
    kern = functools.partial(_inproj_kernel, d=d, fw=fw, pw=pw, aw=aw, kvw=kvw, use_rope=use_rope, n_alias=n_alias)
    out_shapes = (
        jax.ShapeDtypeStruct((b, n, fw), BF16),
        jax.ShapeDtypeStruct((b, n, pw), F32),
        jax.ShapeDtypeStruct((b, n, aw), BF16),
        jax.ShapeDtypeStruct((b, nk, kvw), BF16),
        jax.ShapeDtypeStruct((b, kvw, nk), BF16),
    )

    def cur(s):
        return jnp.minimum(s, total - 1)

    def prev(s):
        return jnp.maximum(s - 1, 0)

    def row_spec(w):
        return pl.BlockSpec((1, tm, w), lambda s: (prev(s) // nt, prev(s) % nt, 0))

    in_specs = [
        pl.BlockSpec((1, tm, d), lambda s: (cur(s) // nt, cur(s) % nt, 0)),
        pl.BlockSpec((1, 1, n6), lambda s: (mod_row(cur(s) // nt), 0, 0)),
        pl.BlockSpec((1, 1, d), lambda s: (layer, 0, 0)),
        pl.BlockSpec((1, d, in_w), lambda s: (layer, 0, 0)),
        pl.BlockSpec((1, 1, HEAD_DIM), lambda s: (layer, 0, 0)),
        pl.BlockSpec((1, 1, HEAD_DIM), lambda s: (layer, 0, 0)),
        pl.BlockSpec((tm, HEAD_DIM), lambda s: (prev(s) % nt, 0)),
        pl.BlockSpec((tm, HEAD_DIM), lambda s: (prev(s) % nt, 0)),
    ] + [pl.BlockSpec(memory_space=pl.ANY)] * n_alias
    return pl.pallas_call(
        kern,
        grid=(total + 1,),
        in_specs=in_specs,
        out_specs=(row_spec(fw), row_spec(pw), row_spec(aw),
                   pl.BlockSpec((1, tm, kvw), lambda s: (prev(s) // nt, kt0 + prev(s) % nt, 0)),
                   pl.BlockSpec((1, kvw, tm), lambda s: (prev(s) // nt, 0, kt0 + prev(s) % nt))),
        out_shape=out_shapes,
        scratch_shapes=[pltpu.VMEM((tm, in_w), F32), pltpu.VMEM((tm, in_w), F32)],
        input_output_aliases={} if kv_dst is None else {8: 3, 9: 4},
        compiler_params=_cparams(("arbitrary",)),
        name="inproj",
    )(x, mod3, norm_g, w_in_bf, qg, kg, cos_t, sin_t, *(kv_dst or ()))


ATTN_CHUNK = 256
ATTN_SLAB = 64


def _attn_kernel(q_ref, k_ref, vt_ref, o_ref, s0_ref, s1_ref, *, group, n_chunks, chunk):
    dn = (((1,), (1,)), ((), ()))

    def scores(c, s_ref):
        r0 = pl.multiple_of(c * chunk, chunk)
        for g in range(group):
            q = q_ref[0, pl.ds(r0, chunk), g * HEAD_DIM:(g + 1) * HEAD_DIM]
            s_ref[g] = lax.dot_general(k_ref[0], q, dn, preferred_element_type=F32)

    def finish(c, s_ref):
        r0 = pl.multiple_of(c * chunk, chunk)
        for g in range(group):
            s = s_ref[g]
            nk = s.shape[0]
            slab = ATTN_SLAB if nk % ATTN_SLAB == 0 else nk
            m = s.reshape(nk // slab, slab, chunk).max(axis=0).max(axis=0, keepdims=True)
            p = jnp.exp2(s - m)
            l = p.reshape(nk // slab, slab, chunk).sum(axis=0).sum(axis=0, keepdims=True)
            ot = jnp.dot(vt_ref[0], p.astype(BF16), preferred_element_type=F32)
            o_ref[0, pl.ds(r0, chunk), g * HEAD_DIM:(g + 1) * HEAD_DIM] = (ot / l).T.astype(BF16)

    scores(0, s0_ref)
    if n_chunks == 1:
        finish(0, s0_ref)
        return

    def body(i, carry):
        c = 2 * i
        scores(c + 1, s1_ref)
        finish(c, s0_ref)
        scores(c + 2, s0_ref)
        finish(c + 1, s1_ref)
        return carry

    lax.fori_loop(0, n_chunks // 2 - 1, body, 0)
    scores(n_chunks - 1, s1_ref)
    finish(n_chunks - 2, s0_ref)
    finish(n_chunks - 1, s1_ref)


def _attention(q, k_all, vt_all, key_row0, nk):
    b, n, aw = q.shape
    assert key_row0 % nk == 0
    kb = key_row0 // nk
    group = aw // HEAD_DIM // N_KV_HEADS
    gw = group * HEAD_DIM
    chunk = min(ATTN_CHUNK, n)
    n_chunks = n // chunk
    assert n % chunk == 0 and (n_chunks == 1 or n_chunks % 2 == 0)
    return pl.pallas_call(
        functools.partial(_attn_kernel, group=group, n_chunks=n_chunks, chunk=chunk),
        grid=(b, N_KV_HEADS),
        in_specs=[pl.BlockSpec((1, n, gw), lambda bi, hi: (bi, 0, hi)),
                  pl.BlockSpec((1, nk, HEAD_DIM), lambda bi, hi: (bi, kb, hi)),
                  pl.BlockSpec((1, HEAD_DIM, nk), lambda bi, hi: (bi, hi, kb))],
        out_specs=pl.BlockSpec((1, n, gw), lambda bi, hi: (bi, 0, hi)),
        out_shape=jax.ShapeDtypeStruct((b, n, aw), BF16),
        scratch_shapes=[pltpu.VMEM((group, nk, chunk), F32), pltpu.VMEM((group, nk, chunk), F32)],
        compiler_params=_cparams(("parallel", "arbitrary")),
        name="attention",
    )(q, k_all, vt_all)


def _fourier_kernel(f_ref, cc_ref, sc_ref, cn_ref, sn_ref, w_ref, o_ref, xc_ref, xs_ref, *, scale):
    @pl.when(pl.program_id(1) == 0)
    def _():
        f = f_ref[0]
        xc_ref[...] = jnp.dot(f, cc_ref[...], preferred_element_type=F32).astype(BF16)
        xs_ref[...] = jnp.dot(f, sc_ref[...], preferred_element_type=F32).astype(BF16)

    fr = (jnp.dot(cn_ref[...], xc_ref[...], preferred_element_type=F32)
          - jnp.dot(sn_ref[...], xs_ref[...], preferred_element_type=F32)) * scale
    o_ref[0] = jnp.dot(fr.astype(BF16), w_ref[0], preferred_element_type=F32).astype(BF16)


DFT_TABLE_MINOR = 16


def _dft_tables(n):
    k = jnp.arange(n, dtype=jnp.int32)

    def cos_sin(t):
        ang = ((k[:, None] * t[None, :]) % n).astype(F32) * (2.0 * jnp.pi / n)
        return jnp.cos(ang), jnp.sin(ang)

    if n <= 16 * DFT_TABLE_MINOR:
        return cos_sin(k)
    c1, s1 = cos_sin(jnp.arange(n // DFT_TABLE_MINOR, dtype=jnp.int32) * DFT_TABLE_MINOR)
    c0, s0 = cos_sin(jnp.arange(DFT_TABLE_MINOR, dtype=jnp.int32))
    cos = c1[:, :, None] * c0[:, None, :] - s1[:, :, None] * s0[:, None, :]
    sin = s1[:, :, None] * c0[:, None, :] + c1[:, :, None] * s0[:, None, :]
    return cos.reshape(n, n), sin.reshape(n, n)


def _fourier(f, layer, w_fourier_bf, cn, sn, cc_bd, sc_bd):
    b, n, fw = f.shape
    tm = _row_tile(n, 512)
    scale = float((n * (fw // N_FOURIER_HEADS)) ** -0.5)
    return pl.pallas_call(
        functools.partial(_fourier_kernel, scale=scale),
        grid=(b, n // tm),
        in_specs=[
            pl.BlockSpec((1, n, fw), lambda bi, ti: (bi, 0, 0)),
            pl.BlockSpec((fw, fw), lambda bi, ti: (0, 0)),
            pl.BlockSpec((fw, fw), lambda bi, ti: (0, 0)),
            pl.BlockSpec((tm, n), lambda bi, ti: (ti, 0)),
            pl.BlockSpec((tm, n), lambda bi, ti: (ti, 0)),
            pl.BlockSpec((1, fw, fw), lambda bi, ti: (layer, 0, 0)),
        ],
        out_specs=pl.BlockSpec((1, tm, fw), lambda bi, ti: (bi, ti, 0)),
        out_shape=jax.ShapeDtypeStruct((b, n, fw), BF16),
        scratch_shapes=[pltpu.VMEM((n, fw), BF16), pltpu.VMEM((n, fw), BF16)],
        compiler_params=_cparams(("parallel", "arbitrary")),
        name="fourier",
    )(f, cc_bd, sc_bd, cn, sn, w_fourier_bf)


POOL_HALO = 16
POOL_CHUNK = 128


def _pool_kernel(p_ref, w_ref, s_ref, o_ref, buf_ref, *, n, pw):
    gdim = pw // len(POOL_WINDOWS)
    zeros = jnp.zeros((POOL_HALO, pw), F32)
    buf_ref[0:POOL_HALO, :] = zeros
    buf_ref[POOL_HALO + n:POOL_HALO + n + POOL_HALO, :] = zeros
    buf_ref[POOL_HALO:POOL_HALO + n, :] = p_ref[0]

    lane = lax.broadcasted_iota(jnp.int32, (1, pw), 1)
    half = jnp.zeros((1, pw), jnp.int32)
    for gi, win in enumerate(POOL_WINDOWS):
        half = jnp.where((lane >= gi * gdim) & (lane < (gi + 1) * gdim), win // 2, half)
    max_half = max(POOL_WINDOWS) // 2
    ch = min(POOL_CHUNK, n)
    for c0 in range(0, n, ch):
        acc = jnp.zeros((ch, pw), F32)
        for s in range(-max_half, max_half):
            term = buf_ref[POOL_HALO + c0 + s:POOL_HALO + c0 + s + ch, :]
            inside = (s >= -half) & (s < half)
            acc = acc + jnp.where(inside, term, 0.0)
        t = c0 + lax.broadcasted_iota(jnp.int32, (ch, pw), 0)
        cnt = (jnp.minimum(t + half, n) - jnp.maximum(t - half, 0)).astype(F32)
        dlt = acc / cnt - buf_ref[POOL_HALO + c0:POOL_HALO + c0 + ch, :]
        y = jnp.dot(dlt.astype(BF16), w_ref[0], preferred_element_type=F32) * s_ref[0]
        o_ref[0, c0:c0 + ch, :] = y.astype(BF16)


def _pool(p, layer, w_pool_bd, pool_scale):
    b, n, pw = p.shape
    return pl.pallas_call(
        functools.partial(_pool_kernel, n=n, pw=pw),
        grid=(b,),
        in_specs=[
            pl.BlockSpec((1, n, pw), lambda bi: (bi, 0, 0)),
            pl.BlockSpec((1, pw, pw), lambda bi: (layer, 0, 0)),
            pl.BlockSpec((1, 1, pw), lambda bi: (layer, 0, 0)),
        ],
        out_specs=pl.BlockSpec((1, n, pw), lambda bi: (bi, 0, 0)),
        out_shape=jax.ShapeDtypeStruct((b, n, pw), BF16),
        scratch_shapes=[pltpu.VMEM((n + 2 * POOL_HALO, pw), F32)],
        compiler_params=_cparams(("parallel",)),
        name="pool",
    )(p, w_pool_bd, pool_scale)


def _outproj_kernel(fo_ref, po_ref, ao_ref, x_ref, modc_ref, modp_ref, g_ref, w_ref, wr_ref,
                    xn_ref, h_ref, aff_ref, xn0_ref, xn1_ref, *, d, fw, pw):
    step = pl.program_id(0)

    @pl.when(step == 0)
    def _():
        xn1_ref[...] = jnp.zeros_like(xn1_ref)

    def project(dst_ref):
        w = w_ref[0]
        ox = jnp.dot(fo_ref[0], w[0:fw], preferred_element_type=F32)
        ox = ox + jnp.dot(po_ref[0], w[fw:fw + pw], preferred_element_type=F32)
        ox = ox + jnp.dot(ao_ref[0], w[fw + pw:], preferred_element_type=F32)
        dst_ref[...] = x_ref[0] + modc_ref[0][:, 2 * d:3 * d] * ox

    def emit(src_ref):
        mod = modp_ref[0]
        xn = src_ref[...]
        xn_ref[0] = xn
        h = _modulate(xn, g_ref[0], mod[:, 3 * d:4 * d], mod[:, 4 * d:5 * d])
        for c in range(d // LANES):
            h_ref[0, :, c, :] = h[:, c * LANES:(c + 1) * LANES]
        logits = jnp.dot(h.astype(BF16), wr_ref[0], preferred_element_type=F32)
        lane = lax.broadcasted_iota(jnp.int32, logits.shape, 1)
        logits = jnp.where(lane < N_EXPERTS, logits, -jnp.inf)
        e = jnp.exp(logits - logits.max(axis=-1, keepdims=True))
        aff = e / e.sum(axis=-1, keepdims=True)
        aff_ref[0] = aff.T[0:N_EXPERTS, :]

    @pl.when(step % 2 == 0)
    def _():
        project(xn0_ref)
        emit(xn1_ref)

    @pl.when(step % 2 == 1)
    def _():
        project(xn1_ref)
        emit(xn0_ref)


def _outproj(fo, po, ao, x, mod3, mod_row, layer, norm_g, w_out_bf, w_router_pad):
    b, n, d = x.shape
    fw = fo.shape[-1]
    pw = po.shape[-1]
    aw = ao.shape[-1]
    n6 = mod3.shape[-1]
    tm = _row_tile(n, 512)
    nt = n // tm
    total = b * nt

    def cur(s):
        return jnp.minimum(s, total - 1)

    def prev(s):
        return jnp.maximum(s - 1, 0)

    def in_spec(w):
        return pl.BlockSpec((1, tm, w), lambda s: (cur(s) // nt, cur(s) % nt, 0))

    return pl.pallas_call(
        functools.partial(_outproj_kernel, d=d, fw=fw, pw=pw),
        grid=(total + 1,),
        in_specs=[
            in_spec(fw), in_spec(pw), in_spec(aw), in_spec(d),
            pl.BlockSpec((1, 1, n6), lambda s: (mod_row(cur(s) // nt), 0, 0)),
            pl.BlockSpec((1, 1, n6), lambda s: (mod_row(prev(s) // nt), 0, 0)),
            pl.BlockSpec((1, 1, d), lambda s: (layer, 0, 0)),
            pl.BlockSpec((1, d, d), lambda s: (layer, 0, 0)),
            pl.BlockSpec((1, d, LANES), lambda s: (layer, 0, 0)),
        ],
        out_specs=(pl.BlockSpec((1, tm, d), lambda s: (prev(s) // nt, prev(s) % nt, 0)),
                   pl.BlockSpec((1, tm, d // LANES, LANES), lambda s: (prev(s) // nt, prev(s) % nt, 0, 0)),
                   pl.BlockSpec((1, N_EXPERTS, tm), lambda s: (prev(s) // nt, 0, prev(s) % nt))),
        out_shape=(jax.ShapeDtypeStruct((b, n, d), F32),
                   jax.ShapeDtypeStruct((b, n, d // LANES, LANES), F32),
                   jax.ShapeDtypeStruct((b, N_EXPERTS, n), F32)),
        scratch_shapes=[pltpu.VMEM((tm, d), F32), pltpu.VMEM((tm, d), F32)],
        compiler_params=_cparams(("arbitrary",)),
        name="outproj",
    )(fo, po, ao, x, mod3, mod3, norm_g, w_out_bf, w_router_pad)


ROUTE_BLOCK = 256
COMBINE_WINDOW = 64
NOT_IN_WINDOW = -(1 << 20)
LHS_ROWS = 16


def _combine_geometry(cap):
    win = min(COMBINE_WINDOW, cap)
    grp = min(N_EXPERTS, ROUTE_BLOCK // win)
    return win, grp


def _route_kernel(aff_ref, idx_ref, gate_ref, key_ref, relw_ref, soff_ref,
                  tri_ref, split_ref, lhs_ref, *, n, cap, blk, win, grp):
    nb = n // blk
    e_n = N_EXPERTS

    @pl.when(pl.program_id(0) == 0)
    def _():
        r = lax.broadcasted_iota(jnp.int32, (blk, blk), 0)
        c = lax.broadcasted_iota(jnp.int32, (blk, blk), 1)
        tri_ref[...] = jnp.where(r < c, 1.0, 0.0).astype(BF16)
        t = lax.broadcasted_iota(jnp.int32, (1, n), 1)
        lhs_ref[...] = jnp.zeros_like(lhs_ref)
        lhs_ref[0:1, :] = (t >> 8).astype(F32)
        lhs_ref[1:2, :] = (t & 255).astype(F32)

    aff = aff_ref[0]

    def search(i, thr_bits):
        cand = thr_bits | lax.shift_left(jnp.int32(1), 30 - i)
        cnt = jnp.sum(jnp.where(aff >= pltpu.bitcast(cand, F32), 1.0, 0.0), axis=-1, keepdims=True)
        return jnp.where(cnt >= cap, cand, thr_bits)

    thr = pltpu.bitcast(lax.fori_loop(0, 31, search, jnp.zeros((e_n, 1), jnp.int32)), F32)
    gt = jnp.where(aff > thr, 1.0, 0.0)
    eq = jnp.where(aff == thr, 1.0, 0.0)
    need = cap - jnp.sum(gt, axis=-1, keepdims=True)
    tri = tri_ref[...]

    sel = []
    off = jnp.zeros((e_n, 1), F32)
    for j in range(nb):
        eqj = eq[:, j * blk:(j + 1) * blk]
        rank = jnp.dot(eqj.astype(BF16), tri, preferred_element_type=F32) + off
        off = off + jnp.sum(eqj, axis=-1, keepdims=True)
        sel.append(jnp.maximum(gt[:, j * blk:(j + 1) * blk], jnp.where(rank < need, eqj, 0.0)))

    lane = lax.broadcasted_iota(jnp.int32, (e_n, LANES), 1)
    q_off = ((lax.broadcasted_iota(jnp.int32, (e_n, 1), 0) % grp) * win).astype(F32)
    soff = jnp.zeros((e_n, LANES), F32)
    off = jnp.zeros((e_n, 1), F32)
    for j in range(nb):
        selj = sel[j]
        pos = jnp.dot(selj.astype(BF16), tri, preferred_element_type=F32) + off
        soff = jnp.where(lane == j, off, soff)
        start = jnp.minimum(jnp.floor(off * (1.0 / 16.0)) * 16.0, float(cap - win))
        rel = pos - start
        chosen = selj > 0.0
        key_ref[0, :, j * blk:(j + 1) * blk] = jnp.where(chosen, pos, -1.0).astype(jnp.int32)
        relw_ref[0, :, j * blk:(j + 1) * blk] = jnp.where(
            chosen & (rel < win), rel + q_off, float(NOT_IN_WINDOW)).astype(jnp.int32)
        off = off + jnp.sum(selj, axis=-1, keepdims=True)
    soff_ref[0] = jnp.where(lane == nb, off, soff).astype(jnp.int32)

    a_hi = aff.astype(BF16).astype(F32)
    r1 = aff - a_hi
    a_mid = r1.astype(BF16).astype(F32)
    split_ref[0] = a_hi
    split_ref[1] = a_mid
    split_ref[2] = (r1 - a_mid).astype(BF16).astype(F32)
    slot = lax.broadcasted_iota(jnp.int32, (cap, n), 0)
    dn = (((1,), (1,)), ((), ()))

    def per_expert(e, carry):
        for k in range(3):
            lhs_ref[2 + k:3 + k, :] = split_ref[k, pl.ds(e, 1), :]
        onehot = jnp.where(key_ref[0, pl.ds(e, 1), :] == slot, 1.0, 0.0).astype(BF16)
        res = lax.dot_general(lhs_ref[...].astype(BF16), onehot, dn, preferred_element_type=F32)
        idx_ref[0, pl.ds(e, 1), :] = (res[0:1] * 256.0 + res[1:2]).astype(jnp.int32)
        gate_ref[0, pl.ds(e, 1), :] = res[2:3] + res[3:4] + res[4:5]
        return carry

    lax.fori_loop(0, e_n, per_expert, 0)


def _route(aff_t, cap):
    b, e, n = aff_t.shape
    blk = min(ROUTE_BLOCK, n)
    assert n % blk == 0 and n // blk < LANES
    win, grp = _combine_geometry(cap)

    def spec(w):
        return pl.BlockSpec((1, e, w), lambda bi: (bi, 0, 0))

    return pl.pallas_call(
        functools.partial(_route_kernel, n=n, cap=cap, blk=blk, win=win, grp=grp),
        grid=(b,),
        in_specs=[spec(n)],
        out_specs=(spec(cap), spec(cap), spec(n), spec(n), spec(LANES)),
        out_shape=(jax.ShapeDtypeStruct((b, e, cap), jnp.int32),
                   jax.ShapeDtypeStruct((b, e, cap), F32),
                   jax.ShapeDtypeStruct((b, e, n), jnp.int32),
                   jax.ShapeDtypeStruct((b, e, n), jnp.int32),
                   jax.ShapeDtypeStruct((b, e, LANES), jnp.int32)),
        scratch_shapes=[pltpu.VMEM((blk, blk), BF16), pltpu.VMEM((3, e, n), F32),
                        pltpu.VMEM((LHS_ROWS, n), F32)],
        compiler_params=_cparams(("arbitrary",)),
        name="moe_route",
    )(aff_t)


def _gather_kernel(idx_ref, h_ref, o_ref, rows_ref, *, cap, n_exp, d):
    def per_expert(e, carry):
        def body(r, c):
            rows_ref[r] = h_ref[0, idx_ref[0, 0, e * cap + r]]
            return c

        lax.fori_loop(0, cap, body, 0, unroll=16)
        tiles = jnp.swapaxes(rows_ref[...], 0, 1)
        for c in range(d // LANES):
            o_ref[0, e, :, c * LANES:(c + 1) * LANES] = tiles[c].astype(BF16)
        return carry

    lax.fori_loop(0, n_exp, per_expert, 0)


def _gather(h3, idx):
    b, n, s, l = h3.shape
    d = s * l
    _, e, cap = idx.shape
    return pl.pallas_call(
        functools.partial(_gather_kernel, cap=cap, n_exp=e, d=d),
        grid=(b,),
        in_specs=[
            pl.BlockSpec((1, 1, e * cap), lambda bi: (bi, 0, 0), memory_space=pltpu.SMEM),
            pl.BlockSpec((1, n, s, l), lambda bi: (bi, 0, 0, 0)),
        ],
        out_specs=pl.BlockSpec((1, e, cap, d), lambda bi: (bi, 0, 0, 0)),
        out_shape=jax.ShapeDtypeStruct((b, e, cap, d), BF16),
        scratch_shapes=[pltpu.VMEM((cap, s, l), F32)],
        compiler_params=_cparams(("arbitrary",)),
        name="moe_gather",
    )(idx.reshape(b, 1, e * cap), h3)


def _ffn_kernel(*refs, bt, n_sets):
    ins = refs[:2 * n_sets]
    wg_ref, wu_ref, wd_ref = refs[2 * n_sets:2 * n_sets + 3]
    outs = refs[2 * n_sets + 3:3 * n_sets + 3]
    wg_bf, wu_bf, wd_bf = refs[3 * n_sets + 3:]

    @pl.when(pl.program_id(1) == 0)
    def _():
        wg_bf[...] = wg_ref[0, 0].astype(BF16)
        wu_bf[...] = wu_ref[0, 0].astype(BF16)
        wd_bf[...] = wd_ref[0, 0].astype(BF16)

    def expert(xs, gate):
        a = jnp.dot(xs, wg_bf[...], preferred_element_type=F32)
        u = jnp.dot(xs, wu_bf[...], preferred_element_type=F32)
        hid = (a * jax.nn.sigmoid(a) * u).astype(BF16)
        return (jnp.dot(hid, wd_bf[...], preferred_element_type=F32) * gate).astype(BF16)

    for si in range(n_sets):
        xs_ref, g_ref, y_ref = ins[2 * si], ins[2 * si + 1], outs[si]
        cap, d = xs_ref.shape[2], xs_ref.shape[3]
        if cap >= 256:
            for bi in range(bt):
                y_ref[bi, 0] = expert(xs_ref[bi, 0], g_ref[bi, 0])
        else:
            xs = xs_ref[:, 0].reshape(bt * cap, d)
            y_ref[:, 0] = expert(xs, g_ref[:, 0].reshape(bt * cap, 1)).reshape(bt, cap, d)


def _ffn(sets, layer, w_gate, w_up, w_down):
    b, e, _, d = sets[0][0].shape
    ff = w_gate.shape[-1]
    bt = min(b, 4)
    assert b % bt == 0
    in_specs = []
    out_specs = []
    out_shape = []
    args = []
    for xs, gate in sets:
        cap = xs.shape[2]
        in_specs.append(pl.BlockSpec((bt, 1, cap, d), lambda ei, ji: (ji, ei, 0, 0)))
        in_specs.append(pl.BlockSpec((bt, 1, cap, 1), lambda ei, ji: (ji, ei, 0, 0)))
        out_specs.append(pl.BlockSpec((bt, 1, cap, d), lambda ei, ji: (ji, ei, 0, 0)))
        out_shape.append(jax.ShapeDtypeStruct(xs.shape, BF16))
        args += [xs, gate]
    in_specs += [
        pl.BlockSpec((1, 1, d, ff), lambda ei, ji: (layer, ei, 0, 0)),
        pl.BlockSpec((1, 1, d, ff), lambda ei, ji: (layer, ei, 0, 0)),
        pl.BlockSpec((1, 1, ff, d), lambda ei, ji: (layer, ei, 0, 0)),
    ]
    return pl.pallas_call(
        functools.partial(_ffn_kernel, bt=bt, n_sets=len(sets)),
        grid=(e, b // bt),
        in_specs=in_specs,
        out_specs=tuple(out_specs),
        out_shape=tuple(out_shape),
        scratch_shapes=[pltpu.VMEM((d, ff), BF16), pltpu.VMEM((d, ff), BF16), pltpu.VMEM((ff, d), BF16)],
        compiler_params=_cparams(("parallel", "arbitrary")),
        name="moe_ffn",
    )(*args, w_gate, w_up, w_down)


def _combine_kernel(soff_ref, relw_ref, key_ref, y_ref, xn_ref, mod_ref, o_ref, rhs_ref,
                    *, cap, d, n_exp, nb, win, grp):
    j = pl.program_id(1)
    stride = nb + 1
    blk = relw_ref.shape[-1]
    dn_t = (((0,), (0,)), ((), ()))

    def window_start(e):
        s = soff_ref[0, 0, e * stride + j]
        return s, jnp.minimum((s >> 4) << 4, cap - win)

    row = lax.broadcasted_iota(jnp.int32, (grp * win, blk), 0)
    acc = jnp.zeros((blk, d), F32)
    for g0 in range(0, n_exp, grp):
        slabs = []
        for q in range(grp):
            e = g0 + q
            _, start = window_start(e)
            rhs_ref[q * win:(q + 1) * win, :] = y_ref[0, e, pl.ds(pl.multiple_of(start, 16), win), :]
            slabs.append(jnp.broadcast_to(relw_ref[0, e:e + 1, :], (win, blk)))
        onehot = jnp.where(jnp.concatenate(slabs, axis=0) == row, 1.0, 0.0).astype(BF16)
        acc = acc + lax.dot_general(onehot, rhs_ref[...], dn_t, preferred_element_type=F32)
    o_ref[0] = acc

    row_w = lax.broadcasted_iota(jnp.int32, (win, blk), 0)

    def further_windows(e, carry):
        s, start0 = window_start(e)
        count = soff_ref[0, 0, e * stride + j + 1] - s
        n_win = (s - start0 + count + win - 1) // win

        def body(k, c):
            lo = start0 + k * win
            st = pl.multiple_of(jnp.minimum(lo, cap - win), 16)
            krow = key_ref[0, pl.ds(e, 1), :]
            onehot = jnp.where((krow - st == row_w) & (krow >= lo), 1.0, 0.0).astype(BF16)
            o_ref[0] = o_ref[0] + lax.dot_general(onehot, y_ref[0, e, pl.ds(st, win), :], dn_t,
                                                  preferred_element_type=F32)
            return c

        lax.fori_loop(1, n_win, body, 0)
        return carry

    lax.fori_loop(0, n_exp, further_windows, 0)
    o_ref[0] = xn_ref[0] + mod_ref[0][:, 5 * d:6 * d] * o_ref[0]


def _combine(y, soff, relw, key, xn, mod3, mod_row):
    b, e, cap, d = y.shape
    n = xn.shape[1]
    n6 = mod3.shape[-1]
    blk = min(ROUTE_BLOCK, n)
    nb = n // blk
    win, grp = _combine_geometry(cap)
    soff_s = soff[:, :, :nb + 1].reshape(b, 1, e * (nb + 1))
    return pl.pallas_call(
        functools.partial(_combine_kernel, cap=cap, d=d, n_exp=e, nb=nb, win=win, grp=grp),
        grid=(b, nb),
        in_specs=[
            pl.BlockSpec((1, 1, e * (nb + 1)), lambda bi, ji: (bi, 0, 0), memory_space=pltpu.SMEM),
            pl.BlockSpec((1, e, blk), lambda bi, ji: (bi, 0, ji)),
            pl.BlockSpec((1, e, blk), lambda bi, ji: (bi, 0, ji)),
            pl.BlockSpec((1, e, cap, d), lambda bi, ji: (bi, 0, 0, 0)),
            pl.BlockSpec((1, blk, d), lambda bi, ji: (bi, ji, 0)),
            pl.BlockSpec((1, 1, n6), lambda bi, ji: (mod_row(bi), 0, 0)),
        ],
        out_specs=pl.BlockSpec((1, blk, d), lambda bi, ji: (bi, ji, 0)),
        out_shape=jax.ShapeDtypeStruct((b, n, d), F32),
        scratch_shapes=[pltpu.VMEM((grp * win, d), BF16)],
        compiler_params=_cparams(("parallel", "arbitrary")),
        name="moe_combine",
    )(soff_s, relw, key, y, xn, mod3)


def _moe_dispatch(h3, aff_t):
    n = h3.shape[1]
    cap = EC_CAPACITY_FACTOR * n // N_EXPERTS
    idx, gate, key, relw, soff = _route(aff_t, cap)
    xs = _gather(h3, idx)
    return xs, gate.reshape(*gate.shape, 1), (soff, relw, key)


def _rope_tables(n):
    t = jnp.arange(n, dtype=jnp.int32)
    row = (t // GRID_W).astype(F32)
    col = (t % GRID_W).astype(F32)
    inv = ROPE_THETA ** (-jnp.arange(0, ROPE_HALF, 2, dtype=F32) / ROPE_HALF)
    ar = row[:, None] * inv[None, :]
    ac = col[:, None] * inv[None, :]
    ang = jnp.concatenate([ar, ar, ac, ac], axis=-1)
    lane = jnp.arange(HEAD_DIM)
    sign = jnp.where((lane % ROPE_HALF) < ROPE_HALF // 2, -1.0, 1.0).astype(F32)
    return jnp.cos(ang), jnp.sin(ang) * sign[None, :]


def _block_diag(blocks):
    g, a, c = blocks.shape
    eye = jnp.eye(g, dtype=blocks.dtype)
    return (eye[:, None, :, None] * blocks[:, :, None, :]).reshape(g * a, g * c)


def kernel(x, c, ctx, c_ctx, ada_w, ada_b, norm1_g, norm2_g, w_in, w_fourier, w_pool, pool_scale,
           q_norm_g, k_norm_g, w_out, w_router, w_gate, w_up, w_down):
    b, n, d = x.shape
    lc = ctx.shape[1]
    depth = ada_w.shape[0]
    fw = d // 4

    rows = -(-(b + 1) // SUBLANES) * SUBLANES
    cond = jnp.zeros((rows, d), F32).at[:b].set(c).at[b].set(c_ctx)
    mod = _ada_mod(cond, ada_w, ada_b)

    def lat_row(bi):
        return bi

    def ctx_row(bi):
        return b

    norm1_g = norm1_g.reshape(depth, 1, d)
    norm2_g = norm2_g.reshape(depth, 1, d)
    q_norm_g = q_norm_g.reshape(depth, 1, HEAD_DIM)
    k_norm_g = k_norm_g.reshape(depth, 1, HEAD_DIM)
    pool_scale = pool_scale.reshape(depth, 1, d // 4)
    w_in_bf = w_in.astype(BF16)
    w_out_bf = w_out.astype(BF16)
    w_fourier_bf = w_fourier.astype(BF16)
    w_router_pad = jnp.zeros((depth, d, LANES), BF16).at[:, :, :N_EXPERTS].set(w_router.astype(BF16))
    w_pool_bd = jax.vmap(_block_diag)(w_pool).astype(BF16)

    cos_t, sin_t = _rope_tables(n)
    cos_c, sin_c = _rope_tables(lc)
    cn, sn = (t.astype(BF16) for t in _dft_tables(n))
    cnc, snc = (t.astype(BF16) for t in _dft_tables(lc))
    hd = fw // N_FOURIER_HEADS
    cch, sch = _dft_tables(hd)
    cc_bd = _block_diag(jnp.broadcast_to(cch, (N_FOURIER_HEADS, hd, hd))).astype(BF16)
    sc_bd = _block_diag(jnp.broadcast_to(sch, (N_FOURIER_HEADS, hd, hd))).astype(BF16)

    kvw = N_KV_HEADS * HEAD_DIM
    kv = [jnp.zeros((b, n + lc, kvw), BF16), jnp.zeros((b, kvw, n + lc), BF16)]
    for i in range(depth):
        last = i == depth - 1
        mod3 = mod[i].reshape(rows, 1, 6 * d)

        fc, pc, qc, *kv = _inproj(ctx, mod3, ctx_row, i, norm1_g, w_in_bf, q_norm_g, k_norm_g,
                                  cos_c, sin_c, False, n + lc, n, kv)
        fx, px, qx, *kv = _inproj(x, mod3, lat_row, i, norm1_g, w_in_bf, q_norm_g, k_norm_g,
                                  cos_t, sin_t, True, n + lc, 0, kv)

        ax = _attention(qx, *kv, 0, n + lc)
        fox = _fourier(fx, i, w_fourier_bf, cn, sn, cc_bd, sc_bd)
        pox = _pool(px, i, w_pool_bd, pool_scale)
        xn, hx, affx = _outproj(fox, pox, ax, x, mod3, lat_row, i, norm2_g, w_out_bf, w_router_pad)
        xs, gx, route_x = _moe_dispatch(hx, affx)

        if last:
            (yx,) = _ffn([(xs, gx)], i, w_gate, w_up, w_down)
        else:
            ac = _attention(qc, *kv, n, lc)
            foc = _fourier(fc, i, w_fourier_bf, cnc, snc, cc_bd, sc_bd)
            poc = _pool(pc, i, w_pool_bd, pool_scale)
            cn_, hc, affc = _outproj(foc, poc, ac, ctx, mod3, ctx_row, i, norm2_g, w_out_bf, w_router_pad)
            xsc, gc, route_c = _moe_dispatch(hc, affc)
            yx, yc = _ffn([(xs, gx), (xsc, gc)], i, w_gate, w_up, w_down)
            ctx = _combine(yc, *route_c, cn_, mod3, ctx_row)
        x = _combine(yx, *route_x, xn, mod3, lat_row)
    return x
```

```python
import functools

import jax
import jax.numpy as jnp
from jax import lax
from jax.experimental import pallas as pl
from jax.experimental.pallas import tpu as pltpu

GRID_W = 64
EPS = 1e-6
N_FOURIER_HEADS = 4
POOL_WINDOWS = (2, 4, 8, 16)
HEAD_DIM = 128
N_KV_HEADS = 2
ROPE_HALF = HEAD_DIM // 2
ROPE_THETA = 10000.0
N_EXPERTS = 16
EC_CAPACITY_FACTOR = 2

LANES = 128
SUBLANES = 8
VMEM_LIMIT_BYTES = 56 * 1024 * 1024

LOG2_E = 1.4426950408889634

F32 = jnp.float32
BF16 = jnp.bfloat16
HIGHEST = lax.Precision.HIGHEST


def _cparams(semantics):
    return pltpu.CompilerParams(dimension_semantics=semantics, vmem_limit_bytes=VMEM_LIMIT_BYTES)


def _row_tile(n, target):
    t = min(n, target)
    assert n % t == 0
    return t


def _ada_kernel(c_ref, w_ref, b_ref, o_ref):
    c = c_ref[...]
    s = c * jax.nn.sigmoid(c)
    o_ref[0] = jnp.dot(s, w_ref[0], preferred_element_type=F32, precision=HIGHEST) + b_ref[0]


def _ada_mod(cond, ada_w, ada_b):
    depth, d, n6 = ada_w.shape
    r = cond.shape[0]
    tn = _row_tile(n6, 1536)
    return pl.pallas_call(
        _ada_kernel,
        grid=(depth, n6 // tn),
        in_specs=[
            pl.BlockSpec((r, d), lambda i, j: (0, 0)),
            pl.BlockSpec((1, d, tn), lambda i, j: (i, 0, j)),
            pl.BlockSpec((1, 1, tn), lambda i, j: (i, 0, j)),
        ],
        out_specs=pl.BlockSpec((1, r, tn), lambda i, j: (i, 0, j)),
        out_shape=jax.ShapeDtypeStruct((depth, r, n6), F32),
        compiler_params=_cparams(("arbitrary", "arbitrary")),
        name="ada_mod",
    )(cond, ada_w, ada_b.reshape(depth, 1, n6))


def _modulate(xf, g, shift, scale):
    ms = jnp.mean(xf * xf, axis=-1, keepdims=True)
    y = xf * lax.rsqrt(ms + EPS) * g
    return y * (1.0 + scale) + shift


def _head_rms(xh, g):
    ms = jnp.mean(xh * xh, axis=-1, keepdims=True)
    return xh * lax.rsqrt(ms + EPS) * g


def _rope(xh, cos, sin_signed, lo_half):
    partner = jnp.where(lo_half, pltpu.roll(xh, HEAD_DIM - ROPE_HALF // 2, 1), pltpu.roll(xh, ROPE_HALF // 2, 1))
    return xh * cos + partner * sin_signed


def _inproj_kernel(*refs, d, fw, pw, aw, kvw, use_rope, n_alias):
    x_ref, mod_ref, g_ref, w_ref, qg_ref, kg_ref, cos_ref, sin_ref = refs[:8]
    f_ref, p_ref, q_ref, k_ref, vt_ref, pr0_ref, pr1_ref = refs[8 + n_alias:]
    step = pl.program_id(0)

    @pl.when(step == 0)
    def _():
        pr1_ref[...] = jnp.zeros_like(pr1_ref)

    def project(dst_ref):
        mod = mod_ref[0]
        h = _modulate(x_ref[0], g_ref[0], mod[:, 0:d], mod[:, d:2 * d]).astype(BF16)
        dst_ref[...] = jnp.dot(h, w_ref[0], preferred_element_type=F32)

    def emit(src_ref):
        f_ref[0] = src_ref[:, 0:fw].astype(BF16)
        p_ref[0] = src_ref[:, fw:fw + pw]
        q_off = fw + pw
        k_off = q_off + aw
        v_off = k_off + kvw
        if use_rope:
            cos = cos_ref[...]
            sin = sin_ref[...]
            lane = lax.broadcasted_iota(jnp.int32, cos.shape, 1)
            lo_half = (lane % ROPE_HALF) < (ROPE_HALF // 2)
        q_scale = HEAD_DIM ** -0.5 * LOG2_E
        for j in range(aw // HEAD_DIM):
            qh = _head_rms(src_ref[:, q_off + j * HEAD_DIM:q_off + (j + 1) * HEAD_DIM], qg_ref[0])
            if use_rope:
                qh = _rope(qh, cos, sin, lo_half)
            q_ref[0, :, j * HEAD_DIM:(j + 1) * HEAD_DIM] = (qh * q_scale).astype(BF16)
        for j in range(kvw // HEAD_DIM):
            kh = _head_rms(src_ref[:, k_off + j * HEAD_DIM:k_off + (j + 1) * HEAD_DIM], kg_ref[0])
            if use_rope:
                kh = _rope(kh, cos, sin, lo_half)
            k_ref[0, :, j * HEAD_DIM:(j + 1) * HEAD_DIM] = kh.astype(BF16)
        vt_ref[0] = src_ref[:, v_off:v_off + kvw].T.astype(BF16)

    @pl.when(step % 2 == 0)
    def _():
        project(pr0_ref)
        emit(pr1_ref)

    @pl.when(step % 2 == 1)
    def _():
        project(pr1_ref)
        emit(pr0_ref)


def _inproj(x, mod3, mod_row, layer, norm_g, w_in_bf, qg, kg, cos_t, sin_t, use_rope, nk, key_row0, kv_dst=None):
    b, n, d = x.shape
    in_w = w_in_bf.shape[-1]
    fw = d // 4
    pw = d // 4
    aw = d // 2
    kvw = N_KV_HEADS * HEAD_DIM
    tm = _row_tile(n, 512)
    nt = n // tm
    total = b * nt
    assert key_row0 % tm == 0
    kt0 = key_row0 // tm
    n6 = mod3.shape[-1]
    n_alias = 0 if kv_dst is None else 2
    kern = functools.partial(_inproj_kernel, d=d, fw=fw, pw=pw, aw=aw, kvw=kvw, use_rope=use_rope, n_alias=n_alias)
    out_shapes = (
        jax.ShapeDtypeStruct((b, n, fw), BF16),
        jax.ShapeDtypeStruct((b, n, pw), F32),
        jax.ShapeDtypeStruct((b, n, aw), BF16),
        jax.ShapeDtypeStruct((b, nk, kvw), BF16),
        jax.ShapeDtypeStruct((b, kvw, nk), BF16),
    )

    def cur(s):
        return jnp.minimum(s, total - 1)

    def prev(s):
        return jnp.maximum(s - 1, 0)

    def row_spec(w):
        return pl.BlockSpec((1, tm, w), lambda s: (prev(s) // nt, prev(s) % nt, 0))

    in_specs = [
        pl.BlockSpec((1, tm, d), lambda s: (cur(s) // nt, cur(s) % nt, 0)),
        pl.BlockSpec((1, 1, n6), lambda s: (mod_row(cur(s) // nt), 0, 0)),
        pl.BlockSpec((1, 1, d), lambda s: (layer, 0, 0)),
        pl.BlockSpec((1, d, in_w), lambda s: (layer, 0, 0)),
        pl.BlockSpec((1, 1, HEAD_DIM), lambda s: (layer, 0, 0)),
        pl.BlockSpec((1, 1, HEAD_DIM), lambda s: (layer, 0, 0)),
        pl.BlockSpec((tm, HEAD_DIM), lambda s: (prev(s) % nt, 0)),
        pl.BlockSpec((tm, HEAD_DIM), lambda s: (prev(s) % nt, 0)),
    ] + [pl.BlockSpec(memory_space=pl.ANY)] * n_alias
    return pl.pallas_call(
        kern,
        grid=(total + 1,),
        in_specs=in_specs,
        out_specs=(row_spec(fw), row_spec(pw), row_spec(aw),
                   pl.BlockSpec((1, tm, kvw), lambda s: (prev(s) // nt, kt0 + prev(s) % nt, 0)),
                   pl.BlockSpec((1, kvw, tm), lambda s: (prev(s) // nt, 0, kt0 + prev(s) % nt))),
        out_shape=out_shapes,
        scratch_shapes=[pltpu.VMEM((tm, in_w), F32), pltpu.VMEM((tm, in_w), F32)],
        input_output_aliases={} if kv_dst is None else {8: 3, 9: 4},
        compiler_params=_cparams(("arbitrary",)),
        name="inproj",
    )(x, mod3, norm_g, w_in_bf, qg, kg, cos_t, sin_t, *(kv_dst or ()))


ATTN_CHUNK = 256
ATTN_SLAB = 64


def _attn_kernel(q_ref, k_ref, vt_ref, o_ref, s0_ref, s1_ref, *, group, n_chunks, chunk):
    dn = (((1,), (1,)), ((), ()))

    def scores(c, s_ref):
        r0 = pl.multiple_of(c * chunk, chunk)
        for g in range(group):
            q = q_ref[0, pl.ds(r0, chunk), g * HEAD_DIM:(g + 1) * HEAD_DIM]
            s_ref[g] = lax.dot_general(k_ref[0], q, dn, preferred_element_type=F32)

    def finish(c, s_ref):
        r0 = pl.multiple_of(c * chunk, chunk)
        for g in range(group):
            s = s_ref[g]
            nk = s.shape[0]
            slab = ATTN_SLAB if nk % ATTN_SLAB == 0 else nk
            m = s.reshape(nk // slab, slab, chunk).max(axis=0).max(axis=0, keepdims=True)
            p = jnp.exp2(s - m)
            l = p.reshape(nk // slab, slab, chunk).sum(axis=0).sum(axis=0, keepdims=True)
            ot = jnp.dot(vt_ref[0], p.astype(BF16), preferred_element_type=F32)
            o_ref[0, pl.ds(r0, chunk), g * HEAD_DIM:(g + 1) * HEAD_DIM] = (ot / l).T.astype(BF16)

    scores(0, s0_ref)
    if n_chunks == 1:
        finish(0, s0_ref)
        return

    def body(i, carry):
        c = 2 * i
        scores(c + 1, s1_ref)
        finish(c, s0_ref)
        scores(c + 2, s0_ref)
        finish(c + 1, s1_ref)
        return carry

    lax.fori_loop(0, n_chunks // 2 - 1, body, 0)
    scores(n_chunks - 1, s1_ref)
    finish(n_chunks - 2, s0_ref)
    finish(n_chunks - 1, s1_ref)


def _attention(q, k_all, vt_all, key_row0, nk):
    b, n, aw = q.shape
    assert key_row0 % nk == 0
    kb = key_row0 // nk
    group = aw // HEAD_DIM // N_KV_HEADS
    gw = group * HEAD_DIM
    chunk = min(ATTN_CHUNK, n)
    n_chunks = n // chunk
    assert n % chunk == 0 and (n_chunks == 1 or n_chunks % 2 == 0)
    return pl.pallas_call(
        functools.partial(_attn_kernel, group=group, n_chunks=n_chunks, chunk=chunk),
        grid=(b, N_KV_HEADS),
        in_specs=[pl.BlockSpec((1, n, gw), lambda bi, hi: (bi, 0, hi)),
                  pl.BlockSpec((1, nk, HEAD_DIM), lambda bi, hi: (bi, kb, hi)),
                  pl.BlockSpec((1, HEAD_DIM, nk), lambda bi, hi: (bi, hi, kb))],
        out_specs=pl.BlockSpec((1, n, gw), lambda bi, hi: (bi, 0, hi)),
        out_shape=jax.ShapeDtypeStruct((b, n, aw), BF16),
        scratch_shapes=[pltpu.VMEM((group, nk, chunk), F32), pltpu.VMEM((group, nk, chunk), F32)],
        compiler_params=_cparams(("parallel", "arbitrary")),
        name="attention",
    )(q, k_all, vt_all)


def _fourier_kernel(f_ref, cc_ref, sc_ref, cn_ref, sn_ref, w_ref, o_ref, xc_ref, xs_ref, *, scale):
    @pl.when(pl.program_id(1) == 0)
    def _():
        f = f_ref[0]
        xc_ref[...] = jnp.dot(f, cc_ref[...], preferred_element_type=F32).astype(BF16)
        xs_ref[...] = jnp.dot(f, sc_ref[...], preferred_element_type=F32).astype(BF16)

    fr = (jnp.dot(cn_ref[...], xc_ref[...], preferred_element_type=F32)
          - jnp.dot(sn_ref[...], xs_ref[...], preferred_element_type=F32)) * scale
    o_ref[0] = jnp.dot(fr.astype(BF16), w_ref[0], preferred_element_type=F32).astype(BF16)


DFT_TABLE_MINOR = 16


def _dft_tables(n):
    k = jnp.arange(n, dtype=jnp.int32)

    def cos_sin(t):
        ang = ((k[:, None] * t[None, :]) % n).astype(F32) * (2.0 * jnp.pi / n)
        return jnp.cos(ang), jnp.sin(ang)

    if n <= 16 * DFT_TABLE_MINOR:
        return cos_sin(k)
    c1, s1 = cos_sin(jnp.arange(n // DFT_TABLE_MINOR, dtype=jnp.int32) * DFT_TABLE_MINOR)
    c0, s0 = cos_sin(jnp.arange(DFT_TABLE_MINOR, dtype=jnp.int32))
    cos = c1[:, :, None] * c0[:, None, :] - s1[:, :, None] * s0[:, None, :]
    sin = s1[:, :, None] * c0[:, None, :] + c1[:, :, None] * s0[:, None, :]
    return cos.reshape(n, n), sin.reshape(n, n)


def _fourier(f, layer, w_fourier_bf, cn, sn, cc_bd, sc_bd):
    b, n, fw = f.shape
    tm = _row_tile(n, 512)
    scale = float((n * (fw // N_FOURIER_HEADS)) ** -0.5)
    return pl.pallas_call(
        functools.partial(_fourier_kernel, scale=scale),
        grid=(b, n // tm),
        in_specs=[
            pl.BlockSpec((1, n, fw), lambda bi, ti: (bi, 0, 0)),
            pl.BlockSpec((fw, fw), lambda bi, ti: (0, 0)),
            pl.BlockSpec((fw, fw), lambda bi, ti: (0, 0)),
            pl.BlockSpec((tm, n), lambda bi, ti: (ti, 0)),
            pl.BlockSpec((tm, n), lambda bi, ti: (ti, 0)),
            pl.BlockSpec((1, fw, fw), lambda bi, ti: (layer, 0, 0)),
        ],
        out_specs=pl.BlockSpec((1, tm, fw), lambda bi, ti: (bi, ti, 0)),
        out_shape=jax.ShapeDtypeStruct((b, n, fw), BF16),
        scratch_shapes=[pltpu.VMEM((n, fw), BF16), pltpu.VMEM((n, fw), BF16)],
        compiler_params=_cparams(("parallel", "arbitrary")),
        name="fourier",
    )(f, cc_bd, sc_bd, cn, sn, w_fourier_bf)


POOL_HALO = 16
POOL_EDGE = 8


def _pool_kernel(p_ref, cnt_ref, w_ref, s_ref, o_ref, x_ref, s2_ref, s4_ref, s8_ref, *, n, pw):
    gdim = pw // len(POOL_WINDOWS)
    lo, hi = POOL_EDGE, n + 2 * POOL_HALO - POOL_EDGE
    for ref in (x_ref, s2_ref, s4_ref, s8_ref):
        ref[0:POOL_HALO, :] = jnp.zeros((POOL_HALO, pw), F32)
        ref[n + POOL_HALO:n + 2 * POOL_HALO, :] = jnp.zeros((POOL_HALO, pw), F32)
    x_ref[POOL_HALO:POOL_HALO + n, :] = p_ref[0]

    s2_ref[lo:hi, :] = x_ref[lo - 1:hi - 1, :] + x_ref[lo:hi, :]
    s4_ref[lo:hi, :] = s2_ref[lo - 1:hi - 1, :] + s2_ref[lo + 1:hi + 1, :]
    s8_ref[lo:hi, :] = s4_ref[lo - 2:hi - 2, :] + s4_ref[lo + 2:hi + 2, :]
    a, b = POOL_HALO, POOL_HALO + n
    s16 = s8_ref[a - 4:b - 4, :] + s8_ref[a + 4:b + 4, :]
    lane = lax.broadcasted_iota(jnp.int32, (1, pw), 1)
    acc = jnp.where(lane < gdim, s2_ref[a:b, :],
                    jnp.where(lane < 2 * gdim, s4_ref[a:b, :], jnp.where(lane < 3 * gdim, s8_ref[a:b, :], s16)))
    dlt = acc / cnt_ref[...] - x_ref[a:b, :]
    y = jnp.dot(dlt.astype(BF16), w_ref[0], preferred_element_type=F32) * s_ref[0]
    o_ref[0] = y.astype(BF16)


def _pool_counts(n, pw):
    gdim = pw // len(POOL_WINDOWS)
    half = jnp.repeat(jnp.asarray([w // 2 for w in POOL_WINDOWS], jnp.int32), gdim)[None, :]
    t = jnp.arange(n, dtype=jnp.int32)[:, None]
    return (jnp.minimum(t + half, n) - jnp.maximum(t - half, 0)).astype(F32)


def _pool(p, cnt, layer, w_pool_bd, pool_scale):
    b, n, pw = p.shape
    assert POOL_WINDOWS == (2, 4, 8, 16)
    buf = pltpu.VMEM((n + 2 * POOL_HALO, pw), F32)
    return pl.pallas_call(
        functools.partial(_pool_kernel, n=n, pw=pw),
        grid=(b,),
        in_specs=[
            pl.BlockSpec((1, n, pw), lambda bi: (bi, 0, 0)),
            pl.BlockSpec((n, pw), lambda bi: (0, 0)),
            pl.BlockSpec((1, pw, pw), lambda bi: (layer, 0, 0)),
            pl.BlockSpec((1, 1, pw), lambda bi: (layer, 0, 0)),
        ],
        out_specs=pl.BlockSpec((1, n, pw), lambda bi: (bi, 0, 0)),
        out_shape=jax.ShapeDtypeStruct((b, n, pw), BF16),
        scratch_shapes=[buf, buf, buf, buf],
        compiler_params=_cparams(("parallel",)),
        name="pool",
    )(p, cnt, w_pool_bd, pool_scale)


def _outproj_kernel(fo_ref, po_ref, ao_ref, x_ref, modc_ref, modp_ref, g_ref, w_ref, wr_ref,
                    xn_ref, h_ref, aff_ref, xn0_ref, xn1_ref, *, d, fw, pw):
    step = pl.program_id(0)

    @pl.when(step == 0)
    def _():
        xn1_ref[...] = jnp.zeros_like(xn1_ref)

    def project(dst_ref):
        w = w_ref[0]
        ox = jnp.dot(fo_ref[0], w[0:fw], preferred_element_type=F32)
        ox = ox + jnp.dot(po_ref[0], w[fw:fw + pw], preferred_element_type=F32)
        ox = ox + jnp.dot(ao_ref[0], w[fw + pw:], preferred_element_type=F32)
        dst_ref[...] = x_ref[0] + modc_ref[0][:, 2 * d:3 * d] * ox

    def emit(src_ref):
        mod = modp_ref[0]
        xn = src_ref[...]
        xn_ref[0] = xn
        h = _modulate(xn, g_ref[0], mod[:, 3 * d:4 * d], mod[:, 4 * d:5 * d])
        h_ref[0] = jnp.swapaxes(jnp.stack([h[:, c * LANES:(c + 1) * LANES] for c in range(d // LANES)]), 0, 1)
        logits = jnp.dot(h.astype(BF16), wr_ref[0], preferred_element_type=F32)
        lane = lax.broadcasted_iota(jnp.int32, logits.shape, 1)
        logits = jnp.where(lane < N_EXPERTS, logits, -jnp.inf)
        e = jnp.exp(logits - logits.max(axis=-1, keepdims=True))
        aff = e / e.sum(axis=-1, keepdims=True)
        aff_ref[0] = aff.T[0:N_EXPERTS, :]

    @pl.when(step % 2 == 0)
    def _():
        project(xn0_ref)
        emit(xn1_ref)

    @pl.when(step % 2 == 1)
    def _():
        project(xn1_ref)
        emit(xn0_ref)


def _outproj(fo, po, ao, x, mod3, mod_row, layer, norm_g, w_out_bf, w_router_pad):
    b, n, d = x.shape
    fw = fo.shape[-1]
    pw = po.shape[-1]
    aw = ao.shape[-1]
    n6 = mod3.shape[-1]
    tm = _row_tile(n, 512)
    nt = n // tm
    total = b * nt

    def cur(s):
        return jnp.minimum(s, total - 1)

    def prev(s):
        return jnp.maximum(s - 1, 0)

    def in_spec(w):
        return pl.BlockSpec((1, tm, w), lambda s: (cur(s) // nt, cur(s) % nt, 0))

    return pl.pallas_call(
        functools.partial(_outproj_kernel, d=d, fw=fw, pw=pw),
        grid=(total + 1,),
        in_specs=[
            in_spec(fw), in_spec(pw), in_spec(aw), in_spec(d),
            pl.BlockSpec((1, 1, n6), lambda s: (mod_row(cur(s) // nt), 0, 0)),
            pl.BlockSpec((1, 1, n6), lambda s: (mod_row(prev(s) // nt), 0, 0)),
            pl.BlockSpec((1, 1, d), lambda s: (layer, 0, 0)),
            pl.BlockSpec((1, d, d), lambda s: (layer, 0, 0)),
            pl.BlockSpec((1, d, LANES), lambda s: (layer, 0, 0)),
        ],
        out_specs=(pl.BlockSpec((1, tm, d), lambda s: (prev(s) // nt, prev(s) % nt, 0)),
                   pl.BlockSpec((1, tm, d // LANES, LANES), lambda s: (prev(s) // nt, prev(s) % nt, 0, 0)),
                   pl.BlockSpec((1, N_EXPERTS, tm), lambda s: (prev(s) // nt, 0, prev(s) % nt))),
        out_shape=(jax.ShapeDtypeStruct((b, n, d), F32),
                   jax.ShapeDtypeStruct((b, n, d // LANES, LANES), F32),
                   jax.ShapeDtypeStruct((b, N_EXPERTS, n), F32)),
        scratch_shapes=[pltpu.VMEM((tm, d), F32), pltpu.VMEM((tm, d), F32)],
        compiler_params=_cparams(("arbitrary",)),
        name="outproj",
    )(fo, po, ao, x, mod3, mod3, norm_g, w_out_bf, w_router_pad)


ROUTE_BLOCK = 256
COMBINE_WINDOW = 64
NOT_IN_WINDOW = -(1 << 20)
LHS_ROWS = 64


def _combine_geometry(cap):
    win = min(COMBINE_WINDOW, cap)
    return win, N_EXPERTS


def _route_kernel(aff_ref, idx_ref, gate_ref, key_ref, relw_ref, soff_ref,
                  tri_ref, lhs_ref, *, n, cap, blk, win, grp):
    nb = n // blk
    e_n = N_EXPERTS

    @pl.when(pl.program_id(0) == 0)
    def _():
        r = lax.broadcasted_iota(jnp.int32, (blk, blk), 0)
        c = lax.broadcasted_iota(jnp.int32, (blk, blk), 1)
        tri_ref[...] = jnp.where(r < c, 1.0, 0.0).astype(BF16)
        t = lax.broadcasted_iota(jnp.int32, (1, n), 1)
        lhs_ref[3 * e_n:, :] = jnp.zeros((LHS_ROWS - 3 * e_n, n), F32)
        lhs_ref[3 * e_n:3 * e_n + 1, :] = (t >> 8).astype(F32)
        lhs_ref[3 * e_n + 1:3 * e_n + 2, :] = (t & 255).astype(F32)

    aff = aff_ref[0]

    def count_ge(bits):
        return jnp.sum(jnp.where(aff >= pltpu.bitcast(bits, F32), 1.0, 0.0), axis=-1, keepdims=True)

    def search(i, thr_bits):
        lo = lax.shift_left(jnp.int32(1), 29 - 2 * i)
        hi = lo + lo
        take_hi = count_ge(thr_bits | hi) >= cap
        take_both = count_ge(thr_bits | hi | lo) >= cap
        take_lo = count_ge(thr_bits | lo) >= cap
        with_hi = jnp.where(take_both, thr_bits | hi | lo, thr_bits | hi)
        without_hi = jnp.where(take_lo, thr_bits | lo, thr_bits)
        return jnp.where(take_hi, with_hi, without_hi)

    thr_bits = lax.fori_loop(0, 15, search, jnp.zeros((e_n, 1), jnp.int32))
    thr_bits = jnp.where(count_ge(thr_bits | 1) >= cap, thr_bits | 1, thr_bits)
    thr = pltpu.bitcast(thr_bits, F32)
    gt = jnp.where(aff > thr, 1.0, 0.0)
    eq = jnp.where(aff == thr, 1.0, 0.0)
    need = cap - jnp.sum(gt, axis=-1, keepdims=True)
    tri = tri_ref[...]

    sel = []
    off = jnp.zeros((e_n, 1), F32)
    for j in range(nb):
        eqj = eq[:, j * blk:(j + 1) * blk]
        rank = jnp.dot(eqj.astype(BF16), tri, preferred_element_type=F32) + off
        off = off + jnp.sum(eqj, axis=-1, keepdims=True)
        sel.append(jnp.maximum(gt[:, j * blk:(j + 1) * blk], jnp.where(rank < need, eqj, 0.0)))

    lane = lax.broadcasted_iota(jnp.int32, (e_n, LANES), 1)
    q_off = ((lax.broadcasted_iota(jnp.int32, (e_n, 1), 0) % grp) * win).astype(F32)
    soff = jnp.zeros((e_n, LANES), F32)
    off = jnp.zeros((e_n, 1), F32)
    for j in range(nb):
        selj = sel[j]
        pos = jnp.dot(selj.astype(BF16), tri, preferred_element_type=F32) + off
        soff = jnp.where(lane == j, off, soff)
        start = jnp.minimum(jnp.floor(off * (1.0 / 16.0)) * 16.0, float(cap - win))
        rel = pos - start
        chosen = selj > 0.0
        key_ref[0, :, j * blk:(j + 1) * blk] = jnp.where(chosen, pos, -1.0).astype(jnp.int32)
        relw_ref[0, :, j * blk:(j + 1) * blk] = jnp.where(
            chosen & (rel < win), rel + q_off, float(NOT_IN_WINDOW)).astype(jnp.int32)
        off = off + jnp.sum(selj, axis=-1, keepdims=True)
    soff_ref[0] = jnp.where(lane == nb, off, soff).astype(jnp.int32)

    a_hi = aff.astype(BF16).astype(F32)
    r1 = aff - a_hi
    a_mid = r1.astype(BF16).astype(F32)
    lhs_ref[0:e_n, :] = a_hi
    lhs_ref[e_n:2 * e_n, :] = a_mid
    lhs_ref[2 * e_n:3 * e_n, :] = (r1 - a_mid).astype(BF16).astype(F32)
    lhs = lhs_ref[...].astype(BF16)
    slot = lax.broadcasted_iota(jnp.int32, (cap, n), 0)
    dn = (((1,), (1,)), ((), ()))
    for e in range(e_n):
        onehot = jnp.where(key_ref[0, e:e + 1, :] == slot, 1.0, 0.0).astype(BF16)
        res = lax.dot_general(lhs, onehot, dn, preferred_element_type=F32)
        idx_ref[0, e:e + 1, :] = (res[3 * e_n:3 * e_n + 1] * 256.0 + res[3 * e_n + 1:3 * e_n + 2]).astype(jnp.int32)
        gate_ref[0, e:e + 1, :] = res[e:e + 1] + res[e_n + e:e_n + e + 1] + res[2 * e_n + e:2 * e_n + e + 1]


def _route(aff_t, cap):
    b, e, n = aff_t.shape
    blk = min(ROUTE_BLOCK, n)
    assert n % blk == 0 and n // blk < LANES
    win, grp = _combine_geometry(cap)

    def spec(w):
        return pl.BlockSpec((1, e, w), lambda bi: (bi, 0, 0))

    return pl.pallas_call(
        functools.partial(_route_kernel, n=n, cap=cap, blk=blk, win=win, grp=grp),
        grid=(b,),
        in_specs=[spec(n)],
        out_specs=(spec(cap), spec(cap), spec(n), spec(n), spec(LANES)),
        out_shape=(jax.ShapeDtypeStruct((b, e, cap), jnp.int32),
                   jax.ShapeDtypeStruct((b, e, cap), F32),
                   jax.ShapeDtypeStruct((b, e, n), jnp.int32),
                   jax.ShapeDtypeStruct((b, e, n), jnp.int32),
                   jax.ShapeDtypeStruct((b, e, LANES), jnp.int32)),
        scratch_shapes=[pltpu.VMEM((blk, blk), BF16), pltpu.VMEM((LHS_ROWS, n), F32)],
        compiler_params=_cparams(("arbitrary",)),
        name="moe_route",
    )(aff_t)


def _gather_kernel(idx_ref, h_ref, o_ref, rows_ref, *, cap, n_exp, d):
    def per_expert(e, carry):
        def body(r, c):
            rows_ref[r] = h_ref[0, idx_ref[0, 0, e * cap + r]]
            return c

        lax.fori_loop(0, cap, body, 0, unroll=16)
        tiles = jnp.swapaxes(rows_ref[...], 0, 1)
        for c in range(d // LANES):
            o_ref[0, e, :, c * LANES:(c + 1) * LANES] = tiles[c].astype(BF16)
        return carry

    lax.fori_loop(0, n_exp, per_expert, 0)


def _gather(h3, idx):
    b, n, s, l = h3.shape
    d = s * l
    _, e, cap = idx.shape
    return pl.pallas_call(
        functools.partial(_gather_kernel, cap=cap, n_exp=e, d=d),
        grid=(b,),
        in_specs=[
            pl.BlockSpec((1, 1, e * cap), lambda bi: (bi, 0, 0), memory_space=pltpu.SMEM),
            pl.BlockSpec((1, n, s, l), lambda bi: (bi, 0, 0, 0)),
        ],
        out_specs=pl.BlockSpec((1, e, cap, d), lambda bi: (bi, 0, 0, 0)),
        out_shape=jax.ShapeDtypeStruct((b, e, cap, d), BF16),
        scratch_shapes=[pltpu.VMEM((cap, s, l), F32)],
        compiler_params=_cparams(("arbitrary",)),
        name="moe_gather",
    )(idx.reshape(b, 1, e * cap), h3)


def _ffn_kernel(*refs, bt, n_sets):
    ins = refs[:2 * n_sets]
    wg_ref, wu_ref, wd_ref = refs[2 * n_sets:2 * n_sets + 3]
    outs = refs[2 * n_sets + 3:3 * n_sets + 3]
    wg_bf, wu_bf, wd_bf = refs[3 * n_sets + 3:]

    @pl.when(pl.program_id(1) == 0)
    def _():
        wg_bf[...] = wg_ref[0, 0].astype(BF16)
        wu_bf[...] = wu_ref[0, 0].astype(BF16)
        wd_bf[...] = wd_ref[0, 0].astype(BF16)

    def expert(xs, gate):
        a = jnp.dot(xs, wg_bf[...], preferred_element_type=F32)
        u = jnp.dot(xs, wu_bf[...], preferred_element_type=F32)
        hid = (a * jax.nn.sigmoid(a) * u).astype(BF16)
        return (jnp.dot(hid, wd_bf[...], preferred_element_type=F32) * gate).astype(BF16)

    for si in range(n_sets):
        xs_ref, g_ref, y_ref = ins[2 * si], ins[2 * si + 1], outs[si]
        cap, d = xs_ref.shape[2], xs_ref.shape[3]
        if cap >= 256:
            for bi in range(bt):
                y_ref[bi, 0] = expert(xs_ref[bi, 0], g_ref[bi, 0])
        else:
            xs = xs_ref[:, 0].reshape(bt * cap, d)
            y_ref[:, 0] = expert(xs, g_ref[:, 0].reshape(bt * cap, 1)).reshape(bt, cap, d)


def _ffn(sets, layer, w_gate, w_up, w_down):
    b, e, _, d = sets[0][0].shape
    ff = w_gate.shape[-1]
    bt = min(b, 4)
    assert b % bt == 0
    in_specs = []
    out_specs = []
    out_shape = []
    args = []
    for xs, gate in sets:
        cap = xs.shape[2]
        in_specs.append(pl.BlockSpec((bt, 1, cap, d), lambda ei, ji: (ji, ei, 0, 0)))
        in_specs.append(pl.BlockSpec((bt, 1, cap, 1), lambda ei, ji: (ji, ei, 0, 0)))
        out_specs.append(pl.BlockSpec((bt, 1, cap, d), lambda ei, ji: (ji, ei, 0, 0)))
        out_shape.append(jax.ShapeDtypeStruct(xs.shape, BF16))
        args += [xs, gate]
    in_specs += [
        pl.BlockSpec((1, 1, d, ff), lambda ei, ji: (layer, ei, 0, 0)),
        pl.BlockSpec((1, 1, d, ff), lambda ei, ji: (layer, ei, 0, 0)),
        pl.BlockSpec((1, 1, ff, d), lambda ei, ji: (layer, ei, 0, 0)),
    ]
    return pl.pallas_call(
        functools.partial(_ffn_kernel, bt=bt, n_sets=len(sets)),
        grid=(e, b // bt),
        in_specs=in_specs,
        out_specs=tuple(out_specs),
        out_shape=tuple(out_shape),
        scratch_shapes=[pltpu.VMEM((d, ff), BF16), pltpu.VMEM((d, ff), BF16), pltpu.VMEM((ff, d), BF16)],
        compiler_params=_cparams(("parallel", "arbitrary")),
        name="moe_ffn",
    )(*args, w_gate, w_up, w_down)


def _combine_kernel(soff_ref, relw_ref, key_ref, y_ref, xn_ref, mod_ref, o_ref, rhs_ref,
                    *, cap, d, n_exp, nb, win, grp):
    j = pl.program_id(1)
    stride = nb + 1
    blk = relw_ref.shape[-1]
    dn_t = (((0,), (0,)), ((), ()))

    def window_start(e):
        s = soff_ref[0, 0, e * stride + j]
        return s, jnp.minimum((s >> 4) << 4, cap - win)

    row = lax.broadcasted_iota(jnp.int32, (grp * win, blk), 0)
    acc = jnp.zeros((blk, d), F32)
    for g0 in range(0, n_exp, grp):
        slabs = []
        for q in range(grp):
            e = g0 + q
            _, start = window_start(e)
            rhs_ref[q * win:(q + 1) * win, :] = y_ref[0, e, pl.ds(pl.multiple_of(start, 16), win), :]
            slabs.append(jnp.broadcast_to(relw_ref[0, e:e + 1, :], (win, blk)))
        onehot = jnp.where(jnp.concatenate(slabs, axis=0) == row, 1.0, 0.0).astype(BF16)
        acc = acc + lax.dot_general(onehot, rhs_ref[...], dn_t, preferred_element_type=F32)
    o_ref[0] = acc

    row_w = lax.broadcasted_iota(jnp.int32, (win, blk), 0)

    def further_windows(e, carry):
        s, start0 = window_start(e)
        count = soff_ref[0, 0, e * stride + j + 1] - s
        n_win = (s - start0 + count + win - 1) // win

        def body(k, c):
            lo = start0 + k * win
            st = pl.multiple_of(jnp.minimum(lo, cap - win), 16)
            krow = key_ref[0, pl.ds(e, 1), :]
            onehot = jnp.where((krow - st == row_w) & (krow >= lo), 1.0, 0.0).astype(BF16)
            o_ref[0] = o_ref[0] + lax.dot_general(onehot, y_ref[0, e, pl.ds(st, win), :], dn_t,
                                                  preferred_element_type=F32)
            return c

        lax.fori_loop(1, n_win, body, 0)
        return carry

    lax.fori_loop(0, n_exp, further_windows, 0)
    o_ref[0] = xn_ref[0] + mod_ref[0][:, 5 * d:6 * d] * o_ref[0]


def _combine(y, soff, relw, key, xn, mod3, mod_row):
    b, e, cap, d = y.shape
    n = xn.shape[1]
    n6 = mod3.shape[-1]
    blk = min(ROUTE_BLOCK, n)
    nb = n // blk
    win, grp = _combine_geometry(cap)
    soff_s = soff[:, :, :nb + 1].reshape(b, 1, e * (nb + 1))
    return pl.pallas_call(
        functools.partial(_combine_kernel, cap=cap, d=d, n_exp=e, nb=nb, win=win, grp=grp),
        grid=(b, nb),
        in_specs=[
            pl.BlockSpec((1, 1, e * (nb + 1)), lambda bi, ji: (bi, 0, 0), memory_space=pltpu.SMEM),
            pl.BlockSpec((1, e, blk), lambda bi, ji: (bi, 0, ji)),
            pl.BlockSpec((1, e, blk), lambda bi, ji: (bi, 0, ji)),
            pl.BlockSpec((1, e, cap, d), lambda bi, ji: (bi, 0, 0, 0)),
            pl.BlockSpec((1, blk, d), lambda bi, ji: (bi, ji, 0)),
            pl.BlockSpec((1, 1, n6), lambda bi, ji: (mod_row(bi), 0, 0)),
        ],
        out_specs=pl.BlockSpec((1, blk, d), lambda bi, ji: (bi, ji, 0)),
        out_shape=jax.ShapeDtypeStruct((b, n, d), F32),
        scratch_shapes=[pltpu.VMEM((grp * win, d), BF16)],
        compiler_params=_cparams(("parallel", "arbitrary")),
        name="moe_combine",
    )(soff_s, relw, key, y, xn, mod3)


def _moe_dispatch(h3, aff_t):
    n = h3.shape[1]
    cap = EC_CAPACITY_FACTOR * n // N_EXPERTS
    idx, gate, key, relw, soff = _route(aff_t, cap)
    xs = _gather(h3, idx)
    return xs, gate.reshape(*gate.shape, 1), (soff, relw, key)


def _rope_tables(n):
    t = jnp.arange(n, dtype=jnp.int32)
    row = (t // GRID_W).astype(F32)
    col = (t % GRID_W).astype(F32)
    inv = ROPE_THETA ** (-jnp.arange(0, ROPE_HALF, 2, dtype=F32) / ROPE_HALF)
    ar = row[:, None] * inv[None, :]
    ac = col[:, None] * inv[None, :]
    ang = jnp.concatenate([ar, ar, ac, ac], axis=-1)
    lane = jnp.arange(HEAD_DIM)
    sign = jnp.where((lane % ROPE_HALF) < ROPE_HALF // 2, -1.0, 1.0).astype(F32)
    return jnp.cos(ang), jnp.sin(ang) * sign[None, :]


def _block_diag(blocks):
    g, a, c = blocks.shape
    eye = jnp.eye(g, dtype=blocks.dtype)
    return (eye[:, None, :, None] * blocks[:, :, None, :]).reshape(g * a, g * c)


def kernel(x, c, ctx, c_ctx, ada_w, ada_b, norm1_g, norm2_g, w_in, w_fourier, w_pool, pool_scale,
           q_norm_g, k_norm_g, w_out, w_router, w_gate, w_up, w_down):
    b, n, d = x.shape
    lc = ctx.shape[1]
    depth = ada_w.shape[0]
    fw = d // 4

    rows = -(-(b + 1) // SUBLANES) * SUBLANES
    cond = jnp.zeros((rows, d), F32).at[:b].set(c).at[b].set(c_ctx)
    mod = _ada_mod(cond, ada_w, ada_b)

    def lat_row(bi):
        return bi

    def ctx_row(bi):
        return b

    norm1_g = norm1_g.reshape(depth, 1, d)
    norm2_g = norm2_g.reshape(depth, 1, d)
    q_norm_g = q_norm_g.reshape(depth, 1, HEAD_DIM)
    k_norm_g = k_norm_g.reshape(depth, 1, HEAD_DIM)
    pool_scale = pool_scale.reshape(depth, 1, d // 4)
    w_in_bf = w_in.astype(BF16)
    w_out_bf = w_out.astype(BF16)
    w_fourier_bf = w_fourier.astype(BF16)
    w_router_pad = jnp.zeros((depth, d, LANES), BF16).at[:, :, :N_EXPERTS].set(w_router.astype(BF16))
    w_pool_bd = jax.vmap(_block_diag)(w_pool).astype(BF16)

    cnt_t = _pool_counts(n, d // 4)
    cnt_c = _pool_counts(lc, d // 4)
    cos_t, sin_t = _rope_tables(n)
    cos_c, sin_c = _rope_tables(lc)
    cn, sn = (t.astype(BF16) for t in _dft_tables(n))
    cnc, snc = (t.astype(BF16) for t in _dft_tables(lc))
    hd = fw // N_FOURIER_HEADS
    cch, sch = _dft_tables(hd)
    cc_bd = _block_diag(jnp.broadcast_to(cch, (N_FOURIER_HEADS, hd, hd))).astype(BF16)
    sc_bd = _block_diag(jnp.broadcast_to(sch, (N_FOURIER_HEADS, hd, hd))).astype(BF16)

    kvw = N_KV_HEADS * HEAD_DIM
    kv = [jnp.zeros((b, n + lc, kvw), BF16), jnp.zeros((b, kvw, n + lc), BF16)]
    for i in range(depth):
        last = i == depth - 1
        mod3 = mod[i].reshape(rows, 1, 6 * d)

        fc, pc, qc, *kv = _inproj(ctx, mod3, ctx_row, i, norm1_g, w_in_bf, q_norm_g, k_norm_g,
                                  cos_c, sin_c, False, n + lc, n, kv)
        fx, px, qx, *kv = _inproj(x, mod3, lat_row, i, norm1_g, w_in_bf, q_norm_g, k_norm_g,
                                  cos_t, sin_t, True, n + lc, 0, kv)

        ax = _attention(qx, *kv, 0, n + lc)
        fox = _fourier(fx, i, w_fourier_bf, cn, sn, cc_bd, sc_bd)
        pox = _pool(px, cnt_t, i, w_pool_bd, pool_scale)
        xn, hx, affx = _outproj(fox, pox, ax, x, mod3, lat_row, i, norm2_g, w_out_bf, w_router_pad)
        xs, gx, route_x = _moe_dispatch(hx, affx)

        if last:
            (yx,) = _ffn([(xs, gx)], i, w_gate, w_up, w_down)
        else:
            ac = _attention(qc, *kv, n, lc)
            foc = _fourier(fc, i, w_fourier_bf, cnc, snc, cc_bd, sc_bd)
            poc = _pool(pc, cnt_c, i, w_pool_bd, pool_scale)
            cn_, hc, affc = _outproj(foc, poc, ac, ctx, mod3, ctx_row, i, norm2_g, w_out_bf, w_router_pad)
            xsc, gc, route_c = _moe_dispatch(hc, affc)
            yx, yc = _ffn([(xs, gx), (xsc, gc)], i, w_gate, w_up, w_down)
            ctx = _combine(yc, *route_c, cn_, mod3, ctx_row)
        x = _combine(yx, *route_x, xn, mod3, lat_row)
    return x
```

```python
import functools

import jax
import jax.numpy as jnp
from jax import lax
from jax.experimental import pallas as pl
from jax.experimental.pallas import tpu as pltpu

GRID_W = 64
EPS = 1e-6
N_FOURIER_HEADS = 4
POOL_WINDOWS = (2, 4, 8, 16)
HEAD_DIM = 128
N_KV_HEADS = 2
ROPE_HALF = HEAD_DIM // 2
ROPE_THETA = 10000.0
N_EXPERTS = 16
EC_CAPACITY_FACTOR = 2

LANES = 128
SUBLANES = 8
VMEM_LIMIT_BYTES = 56 * 1024 * 1024

LOG2_E = 1.4426950408889634

F32 = jnp.float32
BF16 = jnp.bfloat16
HIGHEST = lax.Precision.HIGHEST


def _cparams(semantics):
    return pltpu.CompilerParams(dimension_semantics=semantics, vmem_limit_bytes=VMEM_LIMIT_BYTES)


def _row_tile(n, target):
    t = min(n, target)
    assert n % t == 0
    return t


def _ada_kernel(c_ref, w_ref, b_ref, o_ref):
    c = c_ref[...]
    s = c * jax.nn.sigmoid(c)
    o_ref[0] = jnp.dot(s, w_ref[0], preferred_element_type=F32, precision=HIGHEST) + b_ref[0]


def _ada_mod(cond, ada_w, ada_b):
    depth, d, n6 = ada_w.shape
    r = cond.shape[0]
    tn = _row_tile(n6, 1536)
    return pl.pallas_call(
        _ada_kernel,
        grid=(depth, n6 // tn),
        in_specs=[
            pl.BlockSpec((r, d), lambda i, j: (0, 0)),
            pl.BlockSpec((1, d, tn), lambda i, j: (i, 0, j)),
            pl.BlockSpec((1, 1, tn), lambda i, j: (i, 0, j)),
        ],
        out_specs=pl.BlockSpec((1, r, tn), lambda i, j: (i, 0, j)),
        out_shape=jax.ShapeDtypeStruct((depth, r, n6), F32),
        compiler_params=_cparams(("arbitrary", "arbitrary")),
        name="ada_mod",
    )(cond, ada_w, ada_b.reshape(depth, 1, n6))


def _modulate(xf, g, shift, scale):
    ms = jnp.mean(xf * xf, axis=-1, keepdims=True)
    y = xf * lax.rsqrt(ms + EPS) * g
    return y * (1.0 + scale) + shift


def _head_rms(xh, g):
    ms = jnp.mean(xh * xh, axis=-1, keepdims=True)
    return xh * lax.rsqrt(ms + EPS) * g


def _rope(xh, cos, sin_signed, lo_half):
    partner = jnp.where(lo_half, pltpu.roll(xh, HEAD_DIM - ROPE_HALF // 2, 1), pltpu.roll(xh, ROPE_HALF // 2, 1))
    return xh * cos + partner * sin_signed


def _inproj_kernel(x_ref, mod_ref, g_ref, w_ref, qg_ref, kg_ref, cos_ref, sin_ref,
                   f_ref, p_ref, q_ref, k_ref, vt_ref, pr0_ref, pr1_ref, *, d, fw, pw, aw, kvw, use_rope):
    step = pl.program_id(0)

    @pl.when(step == 0)
    def _():
        pr1_ref[...] = jnp.zeros_like(pr1_ref)

    def project(dst_ref):
        mod = mod_ref[0]
        h = _modulate(x_ref[0], g_ref[0], mod[:, 0:d], mod[:, d:2 * d]).astype(BF16)
        dst_ref[...] = jnp.dot(h, w_ref[0], preferred_element_type=F32)

    def emit(src_ref):
        f_ref[0] = src_ref[:, 0:fw].astype(BF16)
        p_ref[0] = src_ref[:, fw:fw + pw]
        q_off = fw + pw
        k_off = q_off + aw
        v_off = k_off + kvw
        if use_rope:
            cos = cos_ref[...]
            sin = sin_ref[...]
            lane = lax.broadcasted_iota(jnp.int32, cos.shape, 1)
            lo_half = (lane % ROPE_HALF) < (ROPE_HALF // 2)
        q_scale = HEAD_DIM ** -0.5 * LOG2_E
        for j in range(aw // HEAD_DIM):
            qh = _head_rms(src_ref[:, q_off + j * HEAD_DIM:q_off + (j + 1) * HEAD_DIM], qg_ref[0])
            if use_rope:
                qh = _rope(qh, cos, sin, lo_half)
            q_ref[0, :, j * HEAD_DIM:(j + 1) * HEAD_DIM] = (qh * q_scale).astype(BF16)
        for j in range(kvw // HEAD_DIM):
            kh = _head_rms(src_ref[:, k_off + j * HEAD_DIM:k_off + (j + 1) * HEAD_DIM], kg_ref[0])
            if use_rope:
                kh = _rope(kh, cos, sin, lo_half)
            k_ref[0, :, j * HEAD_DIM:(j + 1) * HEAD_DIM] = kh.astype(BF16)
        vt_ref[0] = src_ref[:, v_off:v_off + kvw].T.astype(BF16)

    @pl.when(step % 2 == 0)
    def _():
        project(pr0_ref)
        emit(pr1_ref)

    @pl.when(step % 2 == 1)
    def _():
        project(pr1_ref)
        emit(pr0_ref)


def _inproj(x, mod3, mod_row, layer, norm_g, w_in_bf, qg, kg, cos_t, sin_t, use_rope):
    b, n, d = x.shape
    in_w = w_in_bf.shape[-1]
    fw = d // 4
    pw = d // 4
    aw = d // 2
    kvw = N_KV_HEADS * HEAD_DIM
    tm = _row_tile(n, 512)
    nt = n // tm
    total = b * nt
    n6 = mod3.shape[-1]
    kern = functools.partial(_inproj_kernel, d=d, fw=fw, pw=pw, aw=aw, kvw=kvw, use_rope=use_rope)
    out_shapes = (
        jax.ShapeDtypeStruct((b, n, fw), BF16),
        jax.ShapeDtypeStruct((b, n, pw), F32),
        jax.ShapeDtypeStruct((b, n, aw), BF16),
        jax.ShapeDtypeStruct((b, n, kvw), BF16),
        jax.ShapeDtypeStruct((b, kvw, n), BF16),
    )

    def cur(s):
        return jnp.minimum(s, total - 1)

    def prev(s):
        return jnp.maximum(s - 1, 0)

    def row_spec(w):
        return pl.BlockSpec((1, tm, w), lambda s: (prev(s) // nt, prev(s) % nt, 0))

    return pl.pallas_call(
        kern,
        grid=(total + 1,),
        in_specs=[
            pl.BlockSpec((1, tm, d), lambda s: (cur(s) // nt, cur(s) % nt, 0)),
            pl.BlockSpec((1, 1, n6), lambda s: (mod_row(cur(s) // nt), 0, 0)),
            pl.BlockSpec((1, 1, d), lambda s: (layer, 0, 0)),
            pl.BlockSpec((1, d, in_w), lambda s: (layer, 0, 0)),
            pl.BlockSpec((1, 1, HEAD_DIM), lambda s: (layer, 0, 0)),
            pl.BlockSpec((1, 1, HEAD_DIM), lambda s: (layer, 0, 0)),
            pl.BlockSpec((tm, HEAD_DIM), lambda s: (prev(s) % nt, 0)),
            pl.BlockSpec((tm, HEAD_DIM), lambda s: (prev(s) % nt, 0)),
        ],
        out_specs=(row_spec(fw), row_spec(pw), row_spec(aw), row_spec(kvw),
                   pl.BlockSpec((1, kvw, tm), lambda s: (prev(s) // nt, 0, prev(s) % nt))),
        out_shape=out_shapes,
        scratch_shapes=[pltpu.VMEM((tm, in_w), F32), pltpu.VMEM((tm, in_w), F32)],
        compiler_params=_cparams(("arbitrary",)),
        name="inproj",
    )(x, mod3, norm_g, w_in_bf, qg, kg, cos_t, sin_t)


ATTN_CHUNK = 256
ATTN_SLAB = 64


def _attn_kernel(*refs, n_src, group, n_chunks, chunk):
    q_ref = refs[0]
    k_srcs = refs[1:1 + n_src]
    vt_srcs = refs[1 + n_src:1 + 2 * n_src]
    o_ref, s0_ref, s1_ref = refs[1 + 2 * n_src:4 + 2 * n_src]
    if n_src == 1:
        k_ref, vt_ref = k_srcs[0].at[0], vt_srcs[0].at[0]
    else:
        k_ref, vt_ref = refs[4 + 2 * n_src:]
        row = 0
        for ks, vts in zip(k_srcs, vt_srcs):
            rows = ks.shape[1]
            k_ref[row:row + rows, :] = ks[0]
            vt_ref[:, row:row + rows] = vts[0]
            row += rows
    dn = (((1,), (1,)), ((), ()))

    def scores(c, s_ref):
        r0 = pl.multiple_of(c * chunk, chunk)
        for g in range(group):
            q = q_ref[0, pl.ds(r0, chunk), g * HEAD_DIM:(g + 1) * HEAD_DIM]
            s_ref[g] = lax.dot_general(k_ref[...], q, dn, preferred_element_type=F32)

    def finish(c, s_ref):
        r0 = pl.multiple_of(c * chunk, chunk)
        for g in range(group):
            s = s_ref[g]
            nk = s.shape[0]
            slab = ATTN_SLAB if nk % ATTN_SLAB == 0 else nk
            m = s.reshape(nk // slab, slab, chunk).max(axis=0).max(axis=0, keepdims=True)
            p = jnp.exp2(s - m)
            l = p.reshape(nk // slab, slab, chunk).sum(axis=0).sum(axis=0, keepdims=True)
            ot = jnp.dot(vt_ref[...], p.astype(BF16), preferred_element_type=F32)
            o_ref[0, pl.ds(r0, chunk), g * HEAD_DIM:(g + 1) * HEAD_DIM] = (ot / l).T.astype(BF16)

    scores(0, s0_ref)
    if n_chunks == 1:
        finish(0, s0_ref)
        return

    def body(i, carry):
        c = 2 * i
        scores(c + 1, s1_ref)
        finish(c, s0_ref)
        scores(c + 2, s0_ref)
        finish(c + 1, s1_ref)
        return carry

    lax.fori_loop(0, n_chunks // 2 - 1, body, 0)
    scores(n_chunks - 1, s1_ref)
    finish(n_chunks - 2, s0_ref)
    finish(n_chunks - 1, s1_ref)


def _attention(q, ks, vts):
    b, n, aw = q.shape
    nk = sum(k.shape[1] for k in ks)
    n_src = len(ks)
    group = aw // HEAD_DIM // N_KV_HEADS
    gw = group * HEAD_DIM
    chunk = min(ATTN_CHUNK, n)
    n_chunks = n // chunk
    assert n % chunk == 0 and (n_chunks == 1 or n_chunks % 2 == 0)
    scratch = [pltpu.VMEM((group, nk, chunk), F32), pltpu.VMEM((group, nk, chunk), F32)]
    if n_src > 1:
        scratch += [pltpu.VMEM((nk, HEAD_DIM), BF16), pltpu.VMEM((HEAD_DIM, nk), BF16)]
    return pl.pallas_call(
        functools.partial(_attn_kernel, n_src=n_src, group=group, n_chunks=n_chunks, chunk=chunk),
        grid=(b, N_KV_HEADS),
        in_specs=[pl.BlockSpec((1, n, gw), lambda bi, hi: (bi, 0, hi))]
        + [pl.BlockSpec((1, k.shape[1], HEAD_DIM), lambda bi, hi: (bi, 0, hi)) for k in ks]
        + [pl.BlockSpec((1, HEAD_DIM, v.shape[2]), lambda bi, hi: (bi, hi, 0)) for v in vts],
        out_specs=pl.BlockSpec((1, n, gw), lambda bi, hi: (bi, 0, hi)),
        out_shape=jax.ShapeDtypeStruct((b, n, aw), BF16),
        scratch_shapes=scratch,
        compiler_params=_cparams(("parallel", "arbitrary")),
        name="attention",
    )(q, *ks, *vts)


def _fourier_kernel(f_ref, cc_ref, sc_ref, cn_ref, sn_ref, w_ref, o_ref, xc_ref, xs_ref, *, scale):
    bi = pl.program_id(1)

    @pl.when(pl.program_id(0) == 0)
    def _():
        f = f_ref[0]
        xc_ref[bi] = jnp.dot(f, cc_ref[...], preferred_element_type=F32).astype(BF16)
        xs_ref[bi] = jnp.dot(f, sc_ref[...], preferred_element_type=F32).astype(BF16)

    fr = (jnp.dot(cn_ref[...], xc_ref[bi], preferred_element_type=F32)
          - jnp.dot(sn_ref[...], xs_ref[bi], preferred_element_type=F32)) * scale
    o_ref[0] = jnp.dot(fr.astype(BF16), w_ref[0], preferred_element_type=F32).astype(BF16)


DFT_TABLE_MINOR = 16


def _dft_tables(n):
    k = jnp.arange(n, dtype=jnp.int32)

    def cos_sin(t):
        ang = ((k[:, None] * t[None, :]) % n).astype(F32) * (2.0 * jnp.pi / n)
        return jnp.cos(ang), jnp.sin(ang)

    if n <= 16 * DFT_TABLE_MINOR:
        return cos_sin(k)
    c1, s1 = cos_sin(jnp.arange(n // DFT_TABLE_MINOR, dtype=jnp.int32) * DFT_TABLE_MINOR)
    c0, s0 = cos_sin(jnp.arange(DFT_TABLE_MINOR, dtype=jnp.int32))
    cos = c1[:, :, None] * c0[:, None, :] - s1[:, :, None] * s0[:, None, :]
    sin = s1[:, :, None] * c0[:, None, :] + c1[:, :, None] * s0[:, None, :]
    return cos.reshape(n, n), sin.reshape(n, n)


def _fourier(f, layer, w_fourier_bf, cn, sn, cc_bd, sc_bd):
    b, n, fw = f.shape
    tm = _row_tile(n, 512)
    scale = float((n * (fw // N_FOURIER_HEADS)) ** -0.5)
    return pl.pallas_call(
        functools.partial(_fourier_kernel, scale=scale),
        grid=(n // tm, b),
        in_specs=[
            pl.BlockSpec((1, n, fw), lambda ti, bi: (jnp.where(ti == 0, bi, b - 1), 0, 0)),
            pl.BlockSpec((fw, fw), lambda ti, bi: (0, 0)),
            pl.BlockSpec((fw, fw), lambda ti, bi: (0, 0)),
            pl.BlockSpec((tm, n), lambda ti, bi: (ti, 0)),
            pl.BlockSpec((tm, n), lambda ti, bi: (ti, 0)),
            pl.BlockSpec((1, fw, fw), lambda ti, bi: (layer, 0, 0)),
        ],
        out_specs=pl.BlockSpec((1, tm, fw), lambda ti, bi: (bi, ti, 0)),
        out_shape=jax.ShapeDtypeStruct((b, n, fw), BF16),
        scratch_shapes=[pltpu.VMEM((b, n, fw), BF16), pltpu.VMEM((b, n, fw), BF16)],
        compiler_params=_cparams(("arbitrary", "arbitrary")),
        name="fourier",
    )(f, cc_bd, sc_bd, cn, sn, w_fourier_bf)


POOL_HALO = 16
POOL_EDGE = 8


def _pool_kernel(p_ref, cnt_ref, w_ref, s_ref, o_ref, x_ref, s2_ref, s4_ref, s8_ref, *, n, pw):
    gdim = pw // len(POOL_WINDOWS)
    lo, hi = POOL_EDGE, n + 2 * POOL_HALO - POOL_EDGE
    for ref in (x_ref, s2_ref, s4_ref, s8_ref):
        ref[0:POOL_HALO, :] = jnp.zeros((POOL_HALO, pw), F32)
        ref[n + POOL_HALO:n + 2 * POOL_HALO, :] = jnp.zeros((POOL_HALO, pw), F32)
    x_ref[POOL_HALO:POOL_HALO + n, :] = p_ref[0]

    s2_ref[lo:hi, :] = x_ref[lo - 1:hi - 1, :] + x_ref[lo:hi, :]
    s4_ref[lo:hi, :] = s2_ref[lo - 1:hi - 1, :] + s2_ref[lo + 1:hi + 1, :]
    s8_ref[lo:hi, :] = s4_ref[lo - 2:hi - 2, :] + s4_ref[lo + 2:hi + 2, :]
    a, b = POOL_HALO, POOL_HALO + n
    s16 = s8_ref[a - 4:b - 4, :] + s8_ref[a + 4:b + 4, :]
    lane = lax.broadcasted_iota(jnp.int32, (1, pw), 1)
    acc = jnp.where(lane < gdim, s2_ref[a:b, :],
                    jnp.where(lane < 2 * gdim, s4_ref[a:b, :], jnp.where(lane < 3 * gdim, s8_ref[a:b, :], s16)))
    dlt = acc / cnt_ref[...] - x_ref[a:b, :]
    y = jnp.dot(dlt.astype(BF16), w_ref[0], preferred_element_type=F32) * s_ref[0]
    o_ref[0] = y.astype(BF16)


def _pool_counts(n, pw):
    gdim = pw // len(POOL_WINDOWS)
    half = jnp.repeat(jnp.asarray([w // 2 for w in POOL_WINDOWS], jnp.int32), gdim)[None, :]
    t = jnp.arange(n, dtype=jnp.int32)[:, None]
    return (jnp.minimum(t + half, n) - jnp.maximum(t - half, 0)).astype(F32)


def _pool(p, cnt, layer, w_pool_bd, pool_scale):
    b, n, pw = p.shape
    assert POOL_WINDOWS == (2, 4, 8, 16)
    buf = pltpu.VMEM((n + 2 * POOL_HALO, pw), F32)
    return pl.pallas_call(
        functools.partial(_pool_kernel, n=n, pw=pw),
        grid=(b,),
        in_specs=[
            pl.BlockSpec((1, n, pw), lambda bi: (bi, 0, 0)),
            pl.BlockSpec((n, pw), lambda bi: (0, 0)),
            pl.BlockSpec((1, pw, pw), lambda bi: (layer, 0, 0)),
            pl.BlockSpec((1, 1, pw), lambda bi: (layer, 0, 0)),
        ],
        out_specs=pl.BlockSpec((1, n, pw), lambda bi: (bi, 0, 0)),
        out_shape=jax.ShapeDtypeStruct((b, n, pw), BF16),
        scratch_shapes=[buf, buf, buf, buf],
        compiler_params=_cparams(("parallel",)),
        name="pool",
    )(p, cnt, w_pool_bd, pool_scale)


def _outproj_kernel(fo_ref, po_ref, ao_ref, x_ref, modc_ref, modp_ref, g_ref, w_ref, wr_ref,
                    xn_ref, h_ref, aff_ref, xn0_ref, xn1_ref, *, d, fw, pw):
    step = pl.program_id(0)

    @pl.when(step == 0)
    def _():
        xn1_ref[...] = jnp.zeros_like(xn1_ref)

    def project(dst_ref):
        w = w_ref[0]
        ox = jnp.dot(fo_ref[0], w[0:fw], preferred_element_type=F32)
        ox = ox + jnp.dot(po_ref[0], w[fw:fw + pw], preferred_element_type=F32)
        ox = ox + jnp.dot(ao_ref[0], w[fw + pw:], preferred_element_type=F32)
        dst_ref[...] = x_ref[0] + modc_ref[0][:, 2 * d:3 * d] * ox

    def emit(src_ref):
        mod = modp_ref[0]
        xn = src_ref[...]
        xn_ref[0] = xn
        h = _modulate(xn, g_ref[0], mod[:, 3 * d:4 * d], mod[:, 4 * d:5 * d])
        h_ref[0] = jnp.swapaxes(jnp.stack([h[:, c * LANES:(c + 1) * LANES] for c in range(d // LANES)]), 0, 1)
        logits = jnp.dot(h.astype(BF16), wr_ref[0], preferred_element_type=F32)
        lane = lax.broadcasted_iota(jnp.int32, logits.shape, 1)
        logits = jnp.where(lane < N_EXPERTS, logits, -jnp.inf)
        e = jnp.exp(logits - logits.max(axis=-1, keepdims=True))
        aff = e / e.sum(axis=-1, keepdims=True)
        aff_ref[0] = aff.T[0:N_EXPERTS, :]

    @pl.when(step % 2 == 0)
    def _():
        project(xn0_ref)
        emit(xn1_ref)

    @pl.when(step % 2 == 1)
    def _():
        project(xn1_ref)
        emit(xn0_ref)


def _outproj(fo, po, ao, x, mod3, mod_row, layer, norm_g, w_out_bf, w_router_pad):
    b, n, d = x.shape
    fw = fo.shape[-1]
    pw = po.shape[-1]
    aw = ao.shape[-1]
    n6 = mod3.shape[-1]
    tm = _row_tile(n, 512)
    nt = n // tm
    total = b * nt

    def cur(s):
        return jnp.minimum(s, total - 1)

    def prev(s):
        return jnp.maximum(s - 1, 0)

    def in_spec(w):
        return pl.BlockSpec((1, tm, w), lambda s: (cur(s) // nt, cur(s) % nt, 0))

    return pl.pallas_call(
        functools.partial(_outproj_kernel, d=d, fw=fw, pw=pw),
        grid=(total + 1,),
        in_specs=[
            in_spec(fw), in_spec(pw), in_spec(aw), in_spec(d),
            pl.BlockSpec((1, 1, n6), lambda s: (mod_row(cur(s) // nt), 0, 0)),
            pl.BlockSpec((1, 1, n6), lambda s: (mod_row(prev(s) // nt), 0, 0)),
            pl.BlockSpec((1, 1, d), lambda s: (layer, 0, 0)),
            pl.BlockSpec((1, d, d), lambda s: (layer, 0, 0)),
            pl.BlockSpec((1, d, LANES), lambda s: (layer, 0, 0)),
        ],
        out_specs=(pl.BlockSpec((1, tm, d), lambda s: (prev(s) // nt, prev(s) % nt, 0)),
                   pl.BlockSpec((1, tm, d // LANES, LANES), lambda s: (prev(s) // nt, prev(s) % nt, 0, 0)),
                   pl.BlockSpec((1, N_EXPERTS, tm), lambda s: (prev(s) // nt, 0, prev(s) % nt))),
        out_shape=(jax.ShapeDtypeStruct((b, n, d), F32),
                   jax.ShapeDtypeStruct((b, n, d // LANES, LANES), F32),
                   jax.ShapeDtypeStruct((b, N_EXPERTS, n), F32)),
        scratch_shapes=[pltpu.VMEM((tm, d), F32), pltpu.VMEM((tm, d), F32)],
        compiler_params=_cparams(("arbitrary",)),
        name="outproj",
    )(fo, po, ao, x, mod3, mod3, norm_g, w_out_bf, w_router_pad)


ROUTE_BLOCK = 256
COMBINE_WINDOW = 64
NOT_IN_WINDOW = -(1 << 20)
FLAG_LANE0 = LANES // 2
LHS_ROWS = 64


def _combine_geometry(cap):
    win = min(COMBINE_WINDOW, cap)
    return win, N_EXPERTS


def _route_kernel(aff_ref, idx_ref, gate_ref, key_ref, relw_ref, soff_ref,
                  tri_ref, lhs_ref, *, n, cap, blk, win, grp):
    nb = n // blk
    e_n = N_EXPERTS

    @pl.when(pl.program_id(0) == 0)
    def _():
        r = lax.broadcasted_iota(jnp.int32, (blk, blk), 0)
        c = lax.broadcasted_iota(jnp.int32, (blk, blk), 1)
        tri_ref[...] = jnp.where(r < c, 1.0, 0.0).astype(BF16)
        t = lax.broadcasted_iota(jnp.int32, (1, n), 1)
        lhs_ref[3 * e_n:, :] = jnp.zeros((LHS_ROWS - 3 * e_n, n), F32)
        lhs_ref[3 * e_n:3 * e_n + 1, :] = (t >> 8).astype(F32)
        lhs_ref[3 * e_n + 1:3 * e_n + 2, :] = (t & 255).astype(F32)

    aff = aff_ref[0]

    def count_ge(bits):
        return jnp.sum(jnp.where(aff >= pltpu.bitcast(bits, F32), 1.0, 0.0), axis=-1, keepdims=True)

    def search(i, thr_bits):
        lo = lax.shift_left(jnp.int32(1), 29 - 2 * i)
        hi = lo + lo
        take_hi = count_ge(thr_bits | hi) >= cap
        take_both = count_ge(thr_bits | hi | lo) >= cap
        take_lo = count_ge(thr_bits | lo) >= cap
        with_hi = jnp.where(take_both, thr_bits | hi | lo, thr_bits | hi)
        without_hi = jnp.where(take_lo, thr_bits | lo, thr_bits)
        return jnp.where(take_hi, with_hi, without_hi)

    thr_bits = lax.fori_loop(0, 15, search, jnp.zeros((e_n, 1), jnp.int32))
    thr_bits = jnp.where(count_ge(thr_bits | 1) >= cap, thr_bits | 1, thr_bits)
    thr = pltpu.bitcast(thr_bits, F32)
    gt = jnp.where(aff > thr, 1.0, 0.0)
    eq = jnp.where(aff == thr, 1.0, 0.0)
    need = cap - jnp.sum(gt, axis=-1, keepdims=True)
    tri = tri_ref[...]

    sel = []
    off = jnp.zeros((e_n, 1), F32)
    for j in range(nb):
        eqj = eq[:, j * blk:(j + 1) * blk]
        rank = jnp.dot(eqj.astype(BF16), tri, preferred_element_type=F32) + off
        off = off + jnp.sum(eqj, axis=-1, keepdims=True)
        sel.append(jnp.maximum(gt[:, j * blk:(j + 1) * blk], jnp.where(rank < need, eqj, 0.0)))

    lane = lax.broadcasted_iota(jnp.int32, (e_n, LANES), 1)
    q_off = ((lax.broadcasted_iota(jnp.int32, (e_n, 1), 0) % grp) * win).astype(F32)
    soff = jnp.zeros((e_n, LANES), F32)
    off = jnp.zeros((e_n, 1), F32)
    for j in range(nb):
        selj = sel[j]
        pos = jnp.dot(selj.astype(BF16), tri, preferred_element_type=F32) + off
        soff = jnp.where(lane == j, off, soff)
        start = jnp.minimum(jnp.floor(off * (1.0 / 16.0)) * 16.0, float(cap - win))
        rel = pos - start
        chosen = selj > 0.0
        key_ref[0, :, j * blk:(j + 1) * blk] = jnp.where(chosen, pos, -1.0).astype(jnp.int32)
        relw_ref[0, :, j * blk:(j + 1) * blk] = jnp.where(
            chosen & (rel < win), rel + q_off, float(NOT_IN_WINDOW)).astype(jnp.int32)
        count = jnp.sum(selj, axis=-1, keepdims=True)
        spills = jnp.max(jnp.where(off - start + count > win, 1.0, 0.0), axis=0, keepdims=True)
        soff = jnp.where(lane == FLAG_LANE0 + j, spills, soff)
        off = off + count
    soff_ref[0] = jnp.where(lane == nb, off, soff).astype(jnp.int32)

    a_hi = aff.astype(BF16).astype(F32)
    r1 = aff - a_hi
    a_mid = r1.astype(BF16).astype(F32)
    lhs_ref[0:e_n, :] = a_hi
    lhs_ref[e_n:2 * e_n, :] = a_mid
    lhs_ref[2 * e_n:3 * e_n, :] = (r1 - a_mid).astype(BF16).astype(F32)
    lhs = lhs_ref[...].astype(BF16)
    slot = lax.broadcasted_iota(jnp.int32, (cap, n), 0)
    dn = (((1,), (1,)), ((), ()))
    for e in range(e_n):
        onehot = jnp.where(key_ref[0, e:e + 1, :] == slot, 1.0, 0.0).astype(BF16)
        res = lax.dot_general(lhs, onehot, dn, preferred_element_type=F32)
        idx_ref[0, e:e + 1, :] = (res[3 * e_n:3 * e_n + 1] * 256.0 + res[3 * e_n + 1:3 * e_n + 2]).astype(jnp.int32)
        gate_ref[0, e:e + 1, :] = res[e:e + 1] + res[e_n + e:e_n + e + 1] + res[2 * e_n + e:2 * e_n + e + 1]


def _route(aff_t, cap):
    b, e, n = aff_t.shape
    blk = min(ROUTE_BLOCK, n)
    assert n % blk == 0 and n // blk < FLAG_LANE0
    win, grp = _combine_geometry(cap)

    def spec(w):
        return pl.BlockSpec((1, e, w), lambda bi: (bi, 0, 0))

    return pl.pallas_call(
        functools.partial(_route_kernel, n=n, cap=cap, blk=blk, win=win, grp=grp),
        grid=(b,),
        in_specs=[spec(n)],
        out_specs=(spec(cap), spec(cap), spec(n), spec(n), spec(LANES)),
        out_shape=(jax.ShapeDtypeStruct((b, e, cap), jnp.int32),
                   jax.ShapeDtypeStruct((b, e, cap), F32),
                   jax.ShapeDtypeStruct((b, e, n), jnp.int32),
                   jax.ShapeDtypeStruct((b, e, n), jnp.int32),
                   jax.ShapeDtypeStruct((b, e, LANES), jnp.int32)),
        scratch_shapes=[pltpu.VMEM((blk, blk), BF16), pltpu.VMEM((LHS_ROWS, n), F32)],
        compiler_params=_cparams(("arbitrary",)),
        name="moe_route",
    )(aff_t)


def _gather_kernel(idx_ref, h_ref, o_ref, rows_ref, *, cap, n_exp, d):
    def per_expert(e, carry):
        def body(r, c):
            rows_ref[r] = h_ref[0, idx_ref[0, 0, e * cap + r]]
            return c

        lax.fori_loop(0, cap, body, 0, unroll=16)
        tiles = jnp.swapaxes(rows_ref[...], 0, 1)
        for c in range(d // LANES):
            o_ref[0, e, :, c * LANES:(c + 1) * LANES] = tiles[c].astype(BF16)
        return carry

    lax.fori_loop(0, n_exp, per_expert, 0)


def _gather(h3, idx):
    b, n, s, l = h3.shape
    d = s * l
    _, e, cap = idx.shape
    return pl.pallas_call(
        functools.partial(_gather_kernel, cap=cap, n_exp=e, d=d),
        grid=(b,),
        in_specs=[
            pl.BlockSpec((1, 1, e * cap), lambda bi: (bi, 0, 0), memory_space=pltpu.SMEM),
            pl.BlockSpec((1, n, s, l), lambda bi: (bi, 0, 0, 0)),
        ],
        out_specs=pl.BlockSpec((1, e, cap, d), lambda bi: (bi, 0, 0, 0)),
        out_shape=jax.ShapeDtypeStruct((b, e, cap, d), BF16),
        scratch_shapes=[pltpu.VMEM((cap, s, l), F32)],
        compiler_params=_cparams(("arbitrary",)),
        name="moe_gather",
    )(idx.reshape(b, 1, e * cap), h3)


def _ffn_kernel(*refs, bt, n_sets):
    ins = refs[:2 * n_sets]
    wg_ref, wu_ref, wd_ref = refs[2 * n_sets:2 * n_sets + 3]
    outs = refs[2 * n_sets + 3:3 * n_sets + 3]
    wg_bf, wu_bf, wd_bf = refs[3 * n_sets + 3:]

    @pl.when(pl.program_id(1) == 0)
    def _():
        wg_bf[...] = wg_ref[0, 0].astype(BF16)
        wu_bf[...] = wu_ref[0, 0].astype(BF16)
        wd_bf[...] = wd_ref[0, 0].astype(BF16)

    def expert(xs, gate):
        a = jnp.dot(xs, wg_bf[...], preferred_element_type=F32)
        u = jnp.dot(xs, wu_bf[...], preferred_element_type=F32)
        hid = (a * jax.nn.sigmoid(a) * u).astype(BF16)
        return (jnp.dot(hid, wd_bf[...], preferred_element_type=F32) * gate).astype(BF16)

    for si in range(n_sets):
        xs_ref, g_ref, y_ref = ins[2 * si], ins[2 * si + 1], outs[si]
        cap, d = xs_ref.shape[2], xs_ref.shape[3]
        if cap >= 256:
            for bi in range(bt):
                y_ref[bi, 0] = expert(xs_ref[bi, 0], g_ref[bi, 0])
        else:
            xs = xs_ref[:, 0].reshape(bt * cap, d)
            y_ref[:, 0] = expert(xs, g_ref[:, 0].reshape(bt * cap, 1)).reshape(bt, cap, d)


def _ffn(sets, layer, w_gate, w_up, w_down):
    b, e, _, d = sets[0][0].shape
    ff = w_gate.shape[-1]
    bt = min(b, 4)
    assert b % bt == 0
    in_specs = []
    out_specs = []
    out_shape = []
    args = []
    for xs, gate in sets:
        cap = xs.shape[2]
        in_specs.append(pl.BlockSpec((bt, 1, cap, d), lambda ei, ji: (ji, ei, 0, 0)))
        in_specs.append(pl.BlockSpec((bt, 1, cap, 1), lambda ei, ji: (ji, ei, 0, 0)))
        out_specs.append(pl.BlockSpec((bt, 1, cap, d), lambda ei, ji: (ji, ei, 0, 0)))
        out_shape.append(jax.ShapeDtypeStruct(xs.shape, BF16))
        args += [xs, gate]
    in_specs += [
        pl.BlockSpec((1, 1, d, ff), lambda ei, ji: (layer, ei, 0, 0)),
        pl.BlockSpec((1, 1, d, ff), lambda ei, ji: (layer, ei, 0, 0)),
        pl.BlockSpec((1, 1, ff, d), lambda ei, ji: (layer, ei, 0, 0)),
    ]
    return pl.pallas_call(
        functools.partial(_ffn_kernel, bt=bt, n_sets=len(sets)),
        grid=(e, b // bt),
        in_specs=in_specs,
        out_specs=tuple(out_specs),
        out_shape=tuple(out_shape),
        scratch_shapes=[pltpu.VMEM((d, ff), BF16), pltpu.VMEM((d, ff), BF16), pltpu.VMEM((ff, d), BF16)],
        compiler_params=_cparams(("parallel", "arbitrary")),
        name="moe_ffn",
    )(*args, w_gate, w_up, w_down)


def _combine_kernel(soff_ref, relw_ref, key_ref, y_ref, xn_ref, mod_ref, o_ref, rhs_ref,
                    *, cap, d, n_exp, nb, win, grp):
    j = pl.program_id(1)
    stride = nb + 1
    blk = relw_ref.shape[-1]
    dn_t = (((0,), (0,)), ((), ()))

    def window_start(e):
        s = soff_ref[0, 0, e * stride + j]
        return s, jnp.minimum((s >> 4) << 4, cap - win)

    row = lax.broadcasted_iota(jnp.int32, (grp * win, blk), 0)
    acc = jnp.zeros((blk, d), F32)
    for g0 in range(0, n_exp, grp):
        slabs = []
        for q in range(grp):
            e = g0 + q
            _, start = window_start(e)
            rhs_ref[q * win:(q + 1) * win, :] = y_ref[0, e, pl.ds(pl.multiple_of(start, 16), win), :]
            slabs.append(jnp.broadcast_to(relw_ref[0, e:e + 1, :], (win, blk)))
        onehot = jnp.where(jnp.concatenate(slabs, axis=0) == row, 1.0, 0.0).astype(BF16)
        acc = acc + lax.dot_general(onehot, rhs_ref[...], dn_t, preferred_element_type=F32)
    o_ref[0] = acc

    row_w = lax.broadcasted_iota(jnp.int32, (win, blk), 0)

    def further_windows(e, carry):
        s, start0 = window_start(e)
        count = soff_ref[0, 0, e * stride + j + 1] - s
        n_win = (s - start0 + count + win - 1) // win

        def body(k, c):
            lo = start0 + k * win
            st = pl.multiple_of(jnp.minimum(lo, cap - win), 16)
            krow = key_ref[0, pl.ds(e, 1), :]
            onehot = jnp.where((krow - st == row_w) & (krow >= lo), 1.0, 0.0).astype(BF16)
            o_ref[0] = o_ref[0] + lax.dot_general(onehot, y_ref[0, e, pl.ds(st, win), :], dn_t,
                                                  preferred_element_type=F32)
            return c

        lax.fori_loop(1, n_win, body, 0)
        return carry

    @pl.when(soff_ref[0, 0, n_exp * stride + j] > 0)
    def _():
        lax.fori_loop(0, n_exp, further_windows, 0)

    o_ref[0] = xn_ref[0] + mod_ref[0][:, 5 * d:6 * d] * o_ref[0]


def _combine(y, soff, relw, key, xn, mod3, mod_row):
    b, e, cap, d = y.shape
    n = xn.shape[1]
    n6 = mod3.shape[-1]
    blk = min(ROUTE_BLOCK, n)
    nb = n // blk
    win, grp = _combine_geometry(cap)
    soff_s = jnp.concatenate([soff[:, :, :nb + 1].reshape(b, 1, e * (nb + 1)),
                              soff[:, :1, FLAG_LANE0:FLAG_LANE0 + nb]], axis=-1)
    return pl.pallas_call(
        functools.partial(_combine_kernel, cap=cap, d=d, n_exp=e, nb=nb, win=win, grp=grp),
        grid=(b, nb),
        in_specs=[
            pl.BlockSpec((1, 1, e * (nb + 1) + nb), lambda bi, ji: (bi, 0, 0), memory_space=pltpu.SMEM),
            pl.BlockSpec((1, e, blk), lambda bi, ji: (bi, 0, ji)),
            pl.BlockSpec((1, e, blk), lambda bi, ji: (bi, 0, ji)),
            pl.BlockSpec((1, e, cap, d), lambda bi, ji: (bi, 0, 0, 0)),
            pl.BlockSpec((1, blk, d), lambda bi, ji: (bi, ji, 0)),
            pl.BlockSpec((1, 1, n6), lambda bi, ji: (mod_row(bi), 0, 0)),
        ],
        out_specs=pl.BlockSpec((1, blk, d), lambda bi, ji: (bi, ji, 0)),
        out_shape=jax.ShapeDtypeStruct((b, n, d), F32),
        scratch_shapes=[pltpu.VMEM((grp * win, d), BF16)],
        compiler_params=_cparams(("parallel", "arbitrary")),
        name="moe_combine",
    )(soff_s, relw, key, y, xn, mod3)


def _moe_dispatch(h3, aff_t):
    n = h3.shape[1]
    cap = EC_CAPACITY_FACTOR * n // N_EXPERTS
    idx, gate, key, relw, soff = _route(aff_t, cap)
    xs = _gather(h3, idx)
    return xs, gate.reshape(*gate.shape, 1), (soff, relw, key)


def _rope_tables(n):
    t = jnp.arange(n, dtype=jnp.int32)
    row = (t // GRID_W).astype(F32)
    col = (t % GRID_W).astype(F32)
    inv = ROPE_THETA ** (-jnp.arange(0, ROPE_HALF, 2, dtype=F32) / ROPE_HALF)
    ar = row[:, None] * inv[None, :]
    ac = col[:, None] * inv[None, :]
    ang = jnp.concatenate([ar, ar, ac, ac], axis=-1)
    lane = jnp.arange(HEAD_DIM)
    sign = jnp.where((lane % ROPE_HALF) < ROPE_HALF // 2, -1.0, 1.0).astype(F32)
    return jnp.cos(ang), jnp.sin(ang) * sign[None, :]


def _block_diag(blocks):
    g, a, c = blocks.shape
    eye = jnp.eye(g, dtype=blocks.dtype)
    return (eye[:, None, :, None] * blocks[:, :, None, :]).reshape(g * a, g * c)


def kernel(x, c, ctx, c_ctx, ada_w, ada_b, norm1_g, norm2_g, w_in, w_fourier, w_pool, pool_scale,
           q_norm_g, k_norm_g, w_out, w_router, w_gate, w_up, w_down):
    b, n, d = x.shape
    lc = ctx.shape[1]
    depth = ada_w.shape[0]
    fw = d // 4

    rows = -(-(b + 1) // SUBLANES) * SUBLANES
    cond = jnp.zeros((rows, d), F32).at[:b].set(c).at[b].set(c_ctx)
    mod = _ada_mod(cond, ada_w, ada_b)

    def lat_row(bi):
        return bi

    def ctx_row(bi):
        return b

    norm1_g = norm1_g.reshape(depth, 1, d)
    norm2_g = norm2_g.reshape(depth, 1, d)
    q_norm_g = q_norm_g.reshape(depth, 1, HEAD_DIM)
    k_norm_g = k_norm_g.reshape(depth, 1, HEAD_DIM)
    pool_scale = pool_scale.reshape(depth, 1, d // 4)
    w_in_bf = w_in.astype(BF16)
    w_out_bf = w_out.astype(BF16)
    w_fourier_bf = w_fourier.astype(BF16)
    w_router_pad = jnp.zeros((depth, d, LANES), BF16).at[:, :, :N_EXPERTS].set(w_router.astype(BF16))
    w_pool_bd = jax.vmap(_block_diag)(w_pool).astype(BF16)

    cnt_t = _pool_counts(n, d // 4)
    cnt_c = _pool_counts(lc, d // 4)
    cos_t, sin_t = _rope_tables(n)
    cos_c, sin_c = _rope_tables(lc)
    cn, sn = (t.astype(BF16) for t in _dft_tables(n))
    cnc, snc = (t.astype(BF16) for t in _dft_tables(lc))
    hd = fw // N_FOURIER_HEADS
    cch, sch = _dft_tables(hd)
    cc_bd = _block_diag(jnp.broadcast_to(cch, (N_FOURIER_HEADS, hd, hd))).astype(BF16)
    sc_bd = _block_diag(jnp.broadcast_to(sch, (N_FOURIER_HEADS, hd, hd))).astype(BF16)

    for i in range(depth):
        last = i == depth - 1
        mod3 = mod[i].reshape(rows, 1, 6 * d)

        fx, px, qx, kx, vtx = _inproj(x, mod3, lat_row, i, norm1_g, w_in_bf, q_norm_g, k_norm_g,
                                      cos_t, sin_t, True)
        fc, pc, qc, kc, vtc = _inproj(ctx, mod3, ctx_row, i, norm1_g, w_in_bf, q_norm_g, k_norm_g,
                                      cos_c, sin_c, False)

        ax = _attention(qx, [kx, kc], [vtx, vtc])
        fox = _fourier(fx, i, w_fourier_bf, cn, sn, cc_bd, sc_bd)
        pox = _pool(px, cnt_t, i, w_pool_bd, pool_scale)
        xn, hx, affx = _outproj(fox, pox, ax, x, mod3, lat_row, i, norm2_g, w_out_bf, w_router_pad)
        xs, gx, route_x = _moe_dispatch(hx, affx)

        if last:
            (yx,) = _ffn([(xs, gx)], i, w_gate, w_up, w_down)
        else:
            ac = _attention(qc, [kc], [vtc])
            foc = _fourier(fc, i, w_fourier_bf, cnc, snc, cc_bd, sc_bd)
            poc = _pool(pc, cnt_c, i, w_pool_bd, pool_scale)
            cn_, hc, affc = _outproj(foc, poc, ac, ctx, mod3, ctx_row, i, norm2_g, w_out_bf, w_router_pad)
            xsc, gc, route_c = _moe_dispatch(hc, affc)
            yx, yc = _ffn([(xs, gx), (xsc, gc)], i, w_gate, w_up, w_down)
            ctx = _combine(yc, *route_c, cn_, mod3, ctx_row)
        x = _combine(yx, *route_x, xn, mod3, lat_row)
    return x
```

```python
import functools

import jax
import jax.numpy as jnp
from jax import lax
from jax.experimental import pallas as pl
from jax.experimental.pallas import tpu as pltpu

GRID_W = 64
EPS = 1e-6
N_FOURIER_HEADS = 4
POOL_WINDOWS = (2, 4, 8, 16)
HEAD_DIM = 128
N_KV_HEADS = 2
ROPE_HALF = HEAD_DIM // 2
ROPE_THETA = 10000.0
N_EXPERTS = 16
EC_CAPACITY_FACTOR = 2

LANES = 128
SUBLANES = 8
VMEM_LIMIT_BYTES = 56 * 1024 * 1024

LOG2_E = 1.4426950408889634

F32 = jnp.float32
BF16 = jnp.bfloat16
HIGHEST = lax.Precision.HIGHEST


def _cparams(semantics):
    return pltpu.CompilerParams(dimension_semantics=semantics, vmem_limit_bytes=VMEM_LIMIT_BYTES)


def _row_tile(n, target):
    t = min(n, target)
    assert n % t == 0
    return t


def _ada_kernel(c_ref, w_ref, b_ref, o_ref):
    c = c_ref[...]
    s = c * jax.nn.sigmoid(c)
    o_ref[0] = jnp.dot(s, w_ref[0], preferred_element_type=F32, precision=HIGHEST) + b_ref[0]


def _ada_mod(cond, ada_w, ada_b):
    depth, d, n6 = ada_w.shape
    r = cond.shape[0]
    tn = _row_tile(n6, 1536)
    return pl.pallas_call(
        _ada_kernel,
        grid=(depth, n6 // tn),
        in_specs=[
            pl.BlockSpec((r, d), lambda i, j: (0, 0)),
            pl.BlockSpec((1, d, tn), lambda i, j: (i, 0, j)),
            pl.BlockSpec((1, 1, tn), lambda i, j: (i, 0, j)),
        ],
        out_specs=pl.BlockSpec((1, r, tn), lambda i, j: (i, 0, j)),
        out_shape=jax.ShapeDtypeStruct((depth, r, n6), F32),
        compiler_params=_cparams(("arbitrary", "arbitrary")),
        name="ada_mod",
    )(cond, ada_w, ada_b.reshape(depth, 1, n6))


def _modulate(xf, g, shift, scale):
    ms = jnp.mean(xf * xf, axis=-1, keepdims=True)
    y = xf * lax.rsqrt(ms + EPS) * g
    return y * (1.0 + scale) + shift


def _head_rms(xh, g):
    ms = jnp.mean(xh * xh, axis=-1, keepdims=True)
    return xh * lax.rsqrt(ms + EPS) * g


def _rope(xh, cos, sin_signed, lo_half):
    partner = jnp.where(lo_half, pltpu.roll(xh, HEAD_DIM - ROPE_HALF // 2, 1), pltpu.roll(xh, ROPE_HALF // 2, 1))
    return xh * cos + partner * sin_signed


def _inproj_kernel(x_ref, mod_ref, g_ref, w_ref, qg_ref, kg_ref, cos_ref, sin_ref,
                   f_ref, p_ref, q_ref, k_ref, vt_ref, pr0_ref, pr1_ref, *, d, fw, pw, aw, kvw, use_rope):
    step = pl.program_id(0)

    @pl.when(step == 0)
    def _():
        pr1_ref[...] = jnp.zeros_like(pr1_ref)

    def project(dst_ref):
        mod = mod_ref[0]
        h = _modulate(x_ref[0], g_ref[0], mod[:, 0:d], mod[:, d:2 * d]).astype(BF16)
        dst_ref[...] = jnp.dot(h, w_ref[0], preferred_element_type=F32)

    def emit(src_ref):
        f_ref[0] = src_ref[:, 0:fw].astype(BF16)
        p_ref[0] = src_ref[:, fw:fw + pw]
        q_off = fw + pw
        k_off = q_off + aw
        v_off = k_off + kvw
        if use_rope:
            cos = cos_ref[...]
            sin = sin_ref[...]
            lane = lax.broadcasted_iota(jnp.int32, cos.shape, 1)
            lo_half = (lane % ROPE_HALF) < (ROPE_HALF // 2)
        q_scale = HEAD_DIM ** -0.5 * LOG2_E
        for j in range(aw // HEAD_DIM):
            qh = _head_rms(src_ref[:, q_off + j * HEAD_DIM:q_off + (j + 1) * HEAD_DIM], qg_ref[0])
            if use_rope:
                qh = _rope(qh, cos, sin, lo_half)
            q_ref[0, :, j * HEAD_DIM:(j + 1) * HEAD_DIM] = (qh * q_scale).astype(BF16)
        for j in range(kvw // HEAD_DIM):
            kh = _head_rms(src_ref[:, k_off + j * HEAD_DIM:k_off + (j + 1) * HEAD_DIM], kg_ref[0])
            if use_rope:
                kh = _rope(kh, cos, sin, lo_half)
            k_ref[0, :, j * HEAD_DIM:(j + 1) * HEAD_DIM] = kh.astype(BF16)
        vt_ref[0] = src_ref[:, v_off:v_off + kvw].T.astype(BF16)

    @pl.when(step % 2 == 0)
    def _():
        project(pr0_ref)
        emit(pr1_ref)

    @pl.when(step % 2 == 1)
    def _():
        project(pr1_ref)
        emit(pr0_ref)


def _inproj(x, mod3, mod_row, layer, norm_g, w_in_bf, qg, kg, cos_t, sin_t, use_rope):
    b, n, d = x.shape
    in_w = w_in_bf.shape[-1]
    fw = d // 4
    pw = d // 4
    aw = d // 2
    kvw = N_KV_HEADS * HEAD_DIM
    tm = _row_tile(n, 512)
    nt = n // tm
    total = b * nt
    n6 = mod3.shape[-1]
    kern = functools.partial(_inproj_kernel, d=d, fw=fw, pw=pw, aw=aw, kvw=kvw, use_rope=use_rope)
    out_shapes = (
        jax.ShapeDtypeStruct((b, n, fw), BF16),
        jax.ShapeDtypeStruct((b, n, pw), F32),
        jax.ShapeDtypeStruct((b, n, aw), BF16),
        jax.ShapeDtypeStruct((b, n, kvw), BF16),
        jax.ShapeDtypeStruct((b, kvw, n), BF16),
    )

    def cur(s):
        return jnp.minimum(s, total - 1)

    def prev(s):
        return jnp.maximum(s - 1, 0)

    def row_spec(w):
        return pl.BlockSpec((1, tm, w), lambda s: (prev(s) // nt, prev(s) % nt, 0))

    return pl.pallas_call(
        kern,
        grid=(total + 1,),
        in_specs=[
            pl.BlockSpec((1, tm, d), lambda s: (cur(s) // nt, cur(s) % nt, 0)),
            pl.BlockSpec((1, 1, n6), lambda s: (mod_row(cur(s) // nt), 0, 0)),
            pl.BlockSpec((1, 1, d), lambda s: (layer, 0, 0)),
            pl.BlockSpec((1, d, in_w), lambda s: (layer, 0, 0)),
            pl.BlockSpec((1, 1, HEAD_DIM), lambda s: (layer, 0, 0)),
            pl.BlockSpec((1, 1, HEAD_DIM), lambda s: (layer, 0, 0)),
            pl.BlockSpec((tm, HEAD_DIM), lambda s: (prev(s) % nt, 0)),
            pl.BlockSpec((tm, HEAD_DIM), lambda s: (prev(s) % nt, 0)),
        ],
        out_specs=(row_spec(fw), row_spec(pw), row_spec(aw), row_spec(kvw),
                   pl.BlockSpec((1, kvw, tm), lambda s: (prev(s) // nt, 0, prev(s) % nt))),
        out_shape=out_shapes,
        scratch_shapes=[pltpu.VMEM((tm, in_w), F32), pltpu.VMEM((tm, in_w), F32)],
        compiler_params=_cparams(("arbitrary",)),
        name="inproj",
    )(x, mod3, norm_g, w_in_bf, qg, kg, cos_t, sin_t)


ATTN_CHUNK = 256
ATTN_SLAB = 64


def _attn_kernel(*refs, n_src, group, n_chunks, chunk):
    q_ref = refs[0]
    k_srcs = refs[1:1 + n_src]
    vt_srcs = refs[1 + n_src:1 + 2 * n_src]
    o_ref, s0_ref, s1_ref = refs[1 + 2 * n_src:4 + 2 * n_src]
    if n_src == 1:
        k_ref, vt_ref = k_srcs[0].at[0], vt_srcs[0].at[0]
    else:
        k_ref, vt_ref = refs[4 + 2 * n_src:]
        row = 0
        for ks, vts in zip(k_srcs, vt_srcs):
            rows = ks.shape[1]
            k_ref[row:row + rows, :] = ks[0]
            vt_ref[:, row:row + rows] = vts[0]
            row += rows
    dn = (((1,), (1,)), ((), ()))

    def scores(c, s_ref):
        r0 = pl.multiple_of(c * chunk, chunk)
        for g in range(group):
            q = q_ref[0, pl.ds(r0, chunk), g * HEAD_DIM:(g + 1) * HEAD_DIM]
            s_ref[g] = lax.dot_general(k_ref[...], q, dn, preferred_element_type=F32)

    def finish(c, s_ref):
        r0 = pl.multiple_of(c * chunk, chunk)
        for g in range(group):
            s = s_ref[g]
            nk = s.shape[0]
            slab = ATTN_SLAB if nk % ATTN_SLAB == 0 else nk
            m = s.reshape(nk // slab, slab, chunk).max(axis=0).max(axis=0, keepdims=True)
            p = jnp.exp2(s - m)
            l = p.reshape(nk // slab, slab, chunk).sum(axis=0).sum(axis=0, keepdims=True)
            ot = jnp.dot(vt_ref[...], p.astype(BF16), preferred_element_type=F32)
            o_ref[0, pl.ds(r0, chunk), g * HEAD_DIM:(g + 1) * HEAD_DIM] = (ot / l).T.astype(BF16)

    scores(0, s0_ref)
    if n_chunks == 1:
        finish(0, s0_ref)
        return

    def body(i, carry):
        c = 2 * i
        scores(c + 1, s1_ref)
        finish(c, s0_ref)
        scores(c + 2, s0_ref)
        finish(c + 1, s1_ref)
        return carry

    lax.fori_loop(0, n_chunks // 2 - 1, body, 0)
    scores(n_chunks - 1, s1_ref)
    finish(n_chunks - 2, s0_ref)
    finish(n_chunks - 1, s1_ref)


def _attention(q, ks, vts):
    b, n, aw = q.shape
    nk = sum(k.shape[1] for k in ks)
    n_src = len(ks)
    group = aw // HEAD_DIM // N_KV_HEADS
    gw = group * HEAD_DIM
    chunk = min(ATTN_CHUNK, n)
    n_chunks = n // chunk
    assert n % chunk == 0 and (n_chunks == 1 or n_chunks % 2 == 0)
    scratch = [pltpu.VMEM((group, nk, chunk), F32), pltpu.VMEM((group, nk, chunk), F32)]
    if n_src > 1:
        scratch += [pltpu.VMEM((nk, HEAD_DIM), BF16), pltpu.VMEM((HEAD_DIM, nk), BF16)]
    return pl.pallas_call(
        functools.partial(_attn_kernel, n_src=n_src, group=group, n_chunks=n_chunks, chunk=chunk),
        grid=(b, N_KV_HEADS),
        in_specs=[pl.BlockSpec((1, n, gw), lambda bi, hi: (bi, 0, hi))]
        + [pl.BlockSpec((1, k.shape[1], HEAD_DIM), lambda bi, hi: (bi, 0, hi)) for k in ks]
        + [pl.BlockSpec((1, HEAD_DIM, v.shape[2]), lambda bi, hi: (bi, hi, 0)) for v in vts],
        out_specs=pl.BlockSpec((1, n, gw), lambda bi, hi: (bi, 0, hi)),
        out_shape=jax.ShapeDtypeStruct((b, n, aw), BF16),
        scratch_shapes=scratch,
        compiler_params=_cparams(("parallel", "arbitrary")),
        name="attention",
    )(q, *ks, *vts)


def _fourier_kernel(f_ref, cc_ref, sc_ref, cn_ref, sn_ref, w_ref, o_ref, xc_ref, xs_ref, *, scale):
    bi = pl.program_id(1)

    @pl.when(pl.program_id(0) == 0)
    def _():
        f = f_ref[0]
        xc_ref[bi] = jnp.dot(f, cc_ref[...], preferred_element_type=F32).astype(BF16)
        xs_ref[bi] = jnp.dot(f, sc_ref[...], preferred_element_type=F32).astype(BF16)

    fr = (jnp.dot(cn_ref[...], xc_ref[bi], preferred_element_type=F32)
          - jnp.dot(sn_ref[...], xs_ref[bi], preferred_element_type=F32)) * scale
    o_ref[0] = jnp.dot(fr.astype(BF16), w_ref[0], preferred_element_type=F32).astype(BF16)


DFT_TABLE_MINOR = 16


def _dft_tables(n):
    k = jnp.arange(n, dtype=jnp.int32)

    def cos_sin(t):
        ang = ((k[:, None] * t[None, :]) % n).astype(F32) * (2.0 * jnp.pi / n)
        return jnp.cos(ang), jnp.sin(ang)

    if n <= 16 * DFT_TABLE_MINOR:
        return cos_sin(k)
    c1, s1 = cos_sin(jnp.arange(n // DFT_TABLE_MINOR, dtype=jnp.int32) * DFT_TABLE_MINOR)
    c0, s0 = cos_sin(jnp.arange(DFT_TABLE_MINOR, dtype=jnp.int32))
    c1, s1 = (jnp.repeat(t, DFT_TABLE_MINOR, axis=1) for t in (c1, s1))
    c0, s0 = (jnp.tile(t, (1, n // DFT_TABLE_MINOR)) for t in (c0, s0))
    return c1 * c0 - s1 * s0, s1 * c0 + c1 * s0


def _fourier(f, layer, w_fourier_bf, cn, sn, cc_bd, sc_bd):
    b, n, fw = f.shape
    tm = _row_tile(n, 512)
    scale = float((n * (fw // N_FOURIER_HEADS)) ** -0.5)
    return pl.pallas_call(
        functools.partial(_fourier_kernel, scale=scale),
        grid=(n // tm, b),
        in_specs=[
            pl.BlockSpec((1, n, fw), lambda ti, bi: (jnp.where(ti == 0, bi, b - 1), 0, 0)),
            pl.BlockSpec((fw, fw), lambda ti, bi: (0, 0)),
            pl.BlockSpec((fw, fw), lambda ti, bi: (0, 0)),
            pl.BlockSpec((tm, n), lambda ti, bi: (ti, 0)),
            pl.BlockSpec((tm, n), lambda ti, bi: (ti, 0)),
            pl.BlockSpec((1, fw, fw), lambda ti, bi: (layer, 0, 0)),
        ],
        out_specs=pl.BlockSpec((1, tm, fw), lambda ti, bi: (bi, ti, 0)),
        out_shape=jax.ShapeDtypeStruct((b, n, fw), BF16),
        scratch_shapes=[pltpu.VMEM((b, n, fw), BF16), pltpu.VMEM((b, n, fw), BF16)],
        compiler_params=_cparams(("arbitrary", "arbitrary")),
        name="fourier",
    )(f, cc_bd, sc_bd, cn, sn, w_fourier_bf)


POOL_HALO = 16
POOL_EDGE = 8


def _pool_kernel(p_ref, cnt_ref, w_ref, s_ref, o_ref, x_ref, s2_ref, s4_ref, s8_ref, *, n, pw):
    gdim = pw // len(POOL_WINDOWS)
    lo, hi = POOL_EDGE, n + 2 * POOL_HALO - POOL_EDGE
    for ref in (x_ref, s2_ref, s4_ref, s8_ref):
        ref[0:POOL_HALO, :] = jnp.zeros((POOL_HALO, pw), F32)
        ref[n + POOL_HALO:n + 2 * POOL_HALO, :] = jnp.zeros((POOL_HALO, pw), F32)
    x_ref[POOL_HALO:POOL_HALO + n, :] = p_ref[0]

    s2_ref[lo:hi, :] = x_ref[lo - 1:hi - 1, :] + x_ref[lo:hi, :]
    s4_ref[lo:hi, :] = s2_ref[lo - 1:hi - 1, :] + s2_ref[lo + 1:hi + 1, :]
    s8_ref[lo:hi, :] = s4_ref[lo - 2:hi - 2, :] + s4_ref[lo + 2:hi + 2, :]
    a, b = POOL_HALO, POOL_HALO + n
    s16 = s8_ref[a - 4:b - 4, :] + s8_ref[a + 4:b + 4, :]
    lane = lax.broadcasted_iota(jnp.int32, (1, pw), 1)
    acc = jnp.where(lane < gdim, s2_ref[a:b, :],
                    jnp.where(lane < 2 * gdim, s4_ref[a:b, :], jnp.where(lane < 3 * gdim, s8_ref[a:b, :], s16)))
    dlt = acc / cnt_ref[...] - x_ref[a:b, :]
    y = jnp.dot(dlt.astype(BF16), w_ref[0], preferred_element_type=F32) * s_ref[0]
    o_ref[0] = y.astype(BF16)


def _pool_counts(n, pw):
    gdim = pw // len(POOL_WINDOWS)
    half = jnp.repeat(jnp.asarray([w // 2 for w in POOL_WINDOWS], jnp.int32), gdim)[None, :]
    t = jnp.arange(n, dtype=jnp.int32)[:, None]
    return (jnp.minimum(t + half, n) - jnp.maximum(t - half, 0)).astype(F32)


def _pool(p, cnt, layer, w_pool_bd, pool_scale):
    b, n, pw = p.shape
    assert POOL_WINDOWS == (2, 4, 8, 16)
    buf = pltpu.VMEM((n + 2 * POOL_HALO, pw), F32)
    return pl.pallas_call(
        functools.partial(_pool_kernel, n=n, pw=pw),
        grid=(b,),
        in_specs=[
            pl.BlockSpec((1, n, pw), lambda bi: (bi, 0, 0)),
            pl.BlockSpec((n, pw), lambda bi: (0, 0)),
            pl.BlockSpec((1, pw, pw), lambda bi: (layer, 0, 0)),
            pl.BlockSpec((1, 1, pw), lambda bi: (layer, 0, 0)),
        ],
        out_specs=pl.BlockSpec((1, n, pw), lambda bi: (bi, 0, 0)),
        out_shape=jax.ShapeDtypeStruct((b, n, pw), BF16),
        scratch_shapes=[buf, buf, buf, buf],
        compiler_params=_cparams(("parallel",)),
        name="pool",
    )(p, cnt, w_pool_bd, pool_scale)


def _outproj_kernel(fo_ref, po_ref, ao_ref, x_ref, modc_ref, modp_ref, g_ref, w_ref, wr_ref,
                    xn_ref, h_ref, aff_ref, xn0_ref, xn1_ref, *, d, fw, pw):
    step = pl.program_id(0)

    @pl.when(step == 0)
    def _():
        xn1_ref[...] = jnp.zeros_like(xn1_ref)

    def project(dst_ref):
        w = w_ref[0]
        ox = jnp.dot(fo_ref[0], w[0:fw], preferred_element_type=F32)
        ox = ox + jnp.dot(po_ref[0], w[fw:fw + pw], preferred_element_type=F32)
        ox = ox + jnp.dot(ao_ref[0], w[fw + pw:], preferred_element_type=F32)
        dst_ref[...] = x_ref[0] + modc_ref[0][:, 2 * d:3 * d] * ox

    def emit(src_ref):
        mod = modp_ref[0]
        xn = src_ref[...]
        xn_ref[0] = xn
        h = _modulate(xn, g_ref[0], mod[:, 3 * d:4 * d], mod[:, 4 * d:5 * d])
        logits = jnp.dot(h.astype(BF16), wr_ref[0], preferred_element_type=F32)
        lane = lax.broadcasted_iota(jnp.int32, logits.shape, 1)
        logits = jnp.where(lane < N_EXPERTS, logits, -jnp.inf)
        e = jnp.exp(logits - logits.max(axis=-1, keepdims=True))
        aff = e / e.sum(axis=-1, keepdims=True)
        aff_ref[0] = aff.T[0:N_EXPERTS, :]
        a_hi = aff.astype(BF16).astype(F32)
        r1 = aff - a_hi
        a_mid = r1.astype(BF16).astype(F32)
        a_lo = (r1 - a_mid).astype(BF16).astype(F32)
        tail = a_hi + pltpu.roll(a_mid, N_EXPERTS, 1) + pltpu.roll(a_lo, 2 * N_EXPERTS, 1)
        h_ref[0, :, 0:d] = h.astype(BF16)
        h_ref[0, :, d:d + LANES] = tail.astype(BF16)

    @pl.when(step % 2 == 0)
    def _():
        project(xn0_ref)
        emit(xn1_ref)

    @pl.when(step % 2 == 1)
    def _():
        project(xn1_ref)
        emit(xn0_ref)


def _outproj(fo, po, ao, x, mod3, mod_row, layer, norm_g, w_out_bf, w_router_pad):
    b, n, d = x.shape
    fw = fo.shape[-1]
    pw = po.shape[-1]
    aw = ao.shape[-1]
    n6 = mod3.shape[-1]
    tm = _row_tile(n, 512)
    nt = n // tm
    total = b * nt

    def cur(s):
        return jnp.minimum(s, total - 1)

    def prev(s):
        return jnp.maximum(s - 1, 0)

    def in_spec(w):
        return pl.BlockSpec((1, tm, w), lambda s: (cur(s) // nt, cur(s) % nt, 0))

    return pl.pallas_call(
        functools.partial(_outproj_kernel, d=d, fw=fw, pw=pw),
        grid=(total + 1,),
        in_specs=[
            in_spec(fw), in_spec(pw), in_spec(aw), in_spec(d),
            pl.BlockSpec((1, 1, n6), lambda s: (mod_row(cur(s) // nt), 0, 0)),
            pl.BlockSpec((1, 1, n6), lambda s: (mod_row(prev(s) // nt), 0, 0)),
            pl.BlockSpec((1, 1, d), lambda s: (layer, 0, 0)),
            pl.BlockSpec((1, d, d), lambda s: (layer, 0, 0)),
            pl.BlockSpec((1, d, LANES), lambda s: (layer, 0, 0)),
        ],
        out_specs=(pl.BlockSpec((1, tm, d), lambda s: (prev(s) // nt, prev(s) % nt, 0)),
                   pl.BlockSpec((1, tm, d + LANES), lambda s: (prev(s) // nt, prev(s) % nt, 0)),
                   pl.BlockSpec((1, N_EXPERTS, tm), lambda s: (prev(s) // nt, 0, prev(s) % nt))),
        out_shape=(jax.ShapeDtypeStruct((b, n, d), F32),
                   jax.ShapeDtypeStruct((b, n, d + LANES), BF16),
                   jax.ShapeDtypeStruct((b, N_EXPERTS, n), F32)),
        scratch_shapes=[pltpu.VMEM((tm, d), F32), pltpu.VMEM((tm, d), F32)],
        compiler_params=_cparams(("arbitrary",)),
        name="outproj",
    )(fo, po, ao, x, mod3, mod3, norm_g, w_out_bf, w_router_pad)


ROUTE_BLOCK = 256
COMBINE_WINDOW = 64
NOT_IN_WINDOW = -(1 << 20)
FLAG_LANE0 = LANES // 2


def _combine_geometry(cap):
    win = min(COMBINE_WINDOW, cap)
    return win, N_EXPERTS


def _route_kernel(aff_ref, key_ref, relw_ref, soff_ref, tri_ref, *, n, cap, blk, win, grp):
    nb = n // blk
    e_n = N_EXPERTS

    @pl.when(pl.program_id(0) == 0)
    def _():
        r = lax.broadcasted_iota(jnp.int32, (blk, blk), 0)
        c = lax.broadcasted_iota(jnp.int32, (blk, blk), 1)
        tri_ref[...] = jnp.where(r < c, 1.0, 0.0).astype(BF16)

    aff = aff_ref[0]

    def count_ge(bits):
        return jnp.sum(jnp.where(aff >= pltpu.bitcast(bits, F32), 1.0, 0.0), axis=-1, keepdims=True)

    def search(i, thr_bits):
        lo = lax.shift_left(jnp.int32(1), 29 - 2 * i)
        hi = lo + lo
        take_hi = count_ge(thr_bits | hi) >= cap
        take_both = count_ge(thr_bits | hi | lo) >= cap
        take_lo = count_ge(thr_bits | lo) >= cap
        with_hi = jnp.where(take_both, thr_bits | hi | lo, thr_bits | hi)
        without_hi = jnp.where(take_lo, thr_bits | lo, thr_bits)
        return jnp.where(take_hi, with_hi, without_hi)

    thr_bits = lax.fori_loop(0, 15, search, jnp.zeros((e_n, 1), jnp.int32))
    thr_bits = jnp.where(count_ge(thr_bits | 1) >= cap, thr_bits | 1, thr_bits)
    thr = pltpu.bitcast(thr_bits, F32)
    gt = jnp.where(aff > thr, 1.0, 0.0)
    eq = jnp.where(aff == thr, 1.0, 0.0)
    need = cap - jnp.sum(gt, axis=-1, keepdims=True)
    tri = tri_ref[...]

    sel = []
    off = jnp.zeros((e_n, 1), F32)
    for j in range(nb):
        eqj = eq[:, j * blk:(j + 1) * blk]
        rank = jnp.dot(eqj.astype(BF16), tri, preferred_element_type=F32) + off
        off = off + jnp.sum(eqj, axis=-1, keepdims=True)
        sel.append(jnp.maximum(gt[:, j * blk:(j + 1) * blk], jnp.where(rank < need, eqj, 0.0)))

    lane = lax.broadcasted_iota(jnp.int32, (e_n, LANES), 1)
    q_off = ((lax.broadcasted_iota(jnp.int32, (e_n, 1), 0) % grp) * win).astype(F32)
    soff = jnp.zeros((e_n, LANES), F32)
    off = jnp.zeros((e_n, 1), F32)
    for j in range(nb):
        selj = sel[j]
        pos = jnp.dot(selj.astype(BF16), tri, preferred_element_type=F32) + off
        soff = jnp.where(lane == j, off, soff)
        start = jnp.minimum(jnp.floor(off * (1.0 / 16.0)) * 16.0, float(cap - win))
        rel = pos - start
        chosen = selj > 0.0
        key_ref[0, :, j * blk:(j + 1) * blk] = jnp.where(chosen, pos, -1.0).astype(jnp.int32)
        relw_ref[0, :, j * blk:(j + 1) * blk] = jnp.where(
            chosen & (rel < win), rel + q_off, float(NOT_IN_WINDOW)).astype(jnp.int32)
        count = jnp.sum(selj, axis=-1, keepdims=True)
        spills = jnp.max(jnp.where(off - start + count > win, 1.0, 0.0), axis=0, keepdims=True)
        soff = jnp.where(lane == FLAG_LANE0 + j, spills, soff)
        off = off + count
    soff_ref[0] = jnp.where(lane == nb, off, soff).astype(jnp.int32)


def _route(aff_t, cap):
    b, e, n = aff_t.shape
    blk = min(ROUTE_BLOCK, n)
    assert n % blk == 0 and n // blk < FLAG_LANE0
    win, grp = _combine_geometry(cap)

    def spec(w):
        return pl.BlockSpec((1, e, w), lambda bi: (bi, 0, 0))

    return pl.pallas_call(
        functools.partial(_route_kernel, n=n, cap=cap, blk=blk, win=win, grp=grp),
        grid=(b,),
        in_specs=[spec(n)],
        out_specs=(spec(n), spec(n), spec(LANES)),
        out_shape=(jax.ShapeDtypeStruct((b, e, n), jnp.int32),
                   jax.ShapeDtypeStruct((b, e, n), jnp.int32),
                   jax.ShapeDtypeStruct((b, e, LANES), jnp.int32)),
        scratch_shapes=[pltpu.VMEM((blk, blk), BF16)],
        compiler_params=_cparams(("arbitrary",)),
        name="moe_route",
    )(aff_t)


def _window_start(soff_ref, e, j, stride, cap, win):
    s = soff_ref[0, 0, e * stride + j]
    return s, jnp.minimum((s >> 4) << 4, cap - win)


def _gather_kernel(soff_ref, relw_ref, key_ref, h_ref, o_ref, *, cap, n_exp, nb, win):
    j = pl.program_id(1)
    stride = nb + 1
    blk = relw_ref.shape[-1]

    @pl.when(j == 0)
    def _():
        o_ref[...] = jnp.zeros_like(o_ref)

    row = lax.broadcasted_iota(jnp.int32, (n_exp * win, blk), 0)
    slabs = [jnp.broadcast_to(relw_ref[0, e:e + 1, :], (win, blk)) for e in range(n_exp)]
    onehot = jnp.where(jnp.concatenate(slabs, axis=0) == row, 1.0, 0.0).astype(BF16)
    rows = jnp.dot(onehot, h_ref[0], preferred_element_type=F32).astype(BF16)
    for e in range(n_exp):
        _, start = _window_start(soff_ref, e, j, stride, cap, win)
        dst = o_ref.at[0, e, pl.ds(pl.multiple_of(start, 16), win), :]
        dst[...] = dst[...] + rows[e * win:(e + 1) * win]

    row_w = lax.broadcasted_iota(jnp.int32, (win, blk), 0)

    def further_windows(e, carry):
        s, start0 = _window_start(soff_ref, e, j, stride, cap, win)
        count = soff_ref[0, 0, e * stride + j + 1] - s
        n_win = (s - start0 + count + win - 1) // win

        def body(k, c):
            lo = start0 + k * win
            st = pl.multiple_of(jnp.minimum(lo, cap - win), 16)
            krow = key_ref[0, pl.ds(e, 1), :]
            hot = jnp.where((krow - st == row_w) & (krow >= lo), 1.0, 0.0).astype(BF16)
            dst = o_ref.at[0, e, pl.ds(st, win), :]
            dst[...] = dst[...] + jnp.dot(hot, h_ref[0], preferred_element_type=F32).astype(BF16)
            return c

        lax.fori_loop(1, n_win, body, 0)
        return carry

    @pl.when(soff_ref[0, 0, n_exp * stride + j] > 0)
    def _():
        lax.fori_loop(0, n_exp, further_windows, 0)


def _route_scalars(soff, nb):
    b, e, _ = soff.shape
    return jnp.concatenate([soff[:, :, :nb + 1].reshape(b, 1, e * (nb + 1)),
                            soff[:, :1, FLAG_LANE0:FLAG_LANE0 + nb]], axis=-1)


def _gather(h_aug, soff, relw, key, cap):
    b, n, dw = h_aug.shape
    e = relw.shape[1]
    blk = min(ROUTE_BLOCK, n)
    nb = n // blk
    win, _ = _combine_geometry(cap)
    return pl.pallas_call(
        functools.partial(_gather_kernel, cap=cap, n_exp=e, nb=nb, win=win),
        grid=(b, nb),
        in_specs=[
            pl.BlockSpec((1, 1, e * (nb + 1) + nb), lambda bi, ji: (bi, 0, 0), memory_space=pltpu.SMEM),
            pl.BlockSpec((1, e, blk), lambda bi, ji: (bi, 0, ji)),
            pl.BlockSpec((1, e, blk), lambda bi, ji: (bi, 0, ji)),
            pl.BlockSpec((1, blk, dw), lambda bi, ji: (bi, ji, 0)),
        ],
        out_specs=pl.BlockSpec((1, e, cap, dw), lambda bi, ji: (bi, 0, 0, 0)),
        out_shape=jax.ShapeDtypeStruct((b, e, cap, dw), BF16),
        compiler_params=_cparams(("parallel", "arbitrary")),
        name="moe_gather",
    )(_route_scalars(soff, nb), relw, key, h_aug)


def _ffn_kernel(*refs, bt, n_sets, d):
    ins = refs[:n_sets]
    wg_ref, wu_ref, wd_ref = refs[n_sets:n_sets + 3]
    outs = refs[n_sets + 3:2 * n_sets + 3]
    wg_bf, wu_bf, wd_bf = refs[2 * n_sets + 3:]
    ei = pl.program_id(0)

    @pl.when(pl.program_id(1) == 0)
    def _():
        wg_bf[...] = wg_ref[0, 0].astype(BF16)
        wu_bf[...] = wu_ref[0, 0].astype(BF16)
        wd_bf[...] = wd_ref[0, 0].astype(BF16)

    def expert(rows):
        xs = rows[:, 0:d]
        tail = rows[:, d:d + LANES].astype(F32)
        lane = lax.broadcasted_iota(jnp.int32, tail.shape, 1)
        mine = (lane == ei) | (lane == ei + N_EXPERTS) | (lane == ei + 2 * N_EXPERTS)
        gate = jnp.sum(jnp.where(mine, tail, 0.0), axis=-1, keepdims=True)
        a = jnp.dot(xs, wg_bf[...], preferred_element_type=F32)
        u = jnp.dot(xs, wu_bf[...], preferred_element_type=F32)
        hid = (a * jax.nn.sigmoid(a) * u).astype(BF16)
        return (jnp.dot(hid, wd_bf[...], preferred_element_type=F32) * gate).astype(BF16)

    for xs_ref, y_ref in zip(ins, outs):
        cap, dw = xs_ref.shape[2], xs_ref.shape[3]
        if cap >= 256:
            for bi in range(bt):
                y_ref[bi, 0] = expert(xs_ref[bi, 0])
        else:
            y_ref[:, 0] = expert(xs_ref[:, 0].reshape(bt * cap, dw)).reshape(bt, cap, d)


def _ffn(sets, layer, w_gate, w_up, w_down):
    b, e, _, dw = sets[0].shape
    d = w_gate.shape[-2]
    ff = w_gate.shape[-1]
    bt = min(b, 4)
    assert b % bt == 0 and dw == d + LANES
    in_specs = [pl.BlockSpec((bt, 1, xs.shape[2], dw), lambda ei, ji: (ji, ei, 0, 0)) for xs in sets]
    out_specs = [pl.BlockSpec((bt, 1, xs.shape[2], d), lambda ei, ji: (ji, ei, 0, 0)) for xs in sets]
    out_shape = [jax.ShapeDtypeStruct(xs.shape[:3] + (d,), BF16) for xs in sets]
    in_specs += [
        pl.BlockSpec((1, 1, d, ff), lambda ei, ji: (layer, ei, 0, 0)),
        pl.BlockSpec((1, 1, d, ff), lambda ei, ji: (layer, ei, 0, 0)),
        pl.BlockSpec((1, 1, ff, d), lambda ei, ji: (layer, ei, 0, 0)),
    ]
    return pl.pallas_call(
        functools.partial(_ffn_kernel, bt=bt, n_sets=len(sets), d=d),
        grid=(e, b // bt),
        in_specs=in_specs,
        out_specs=tuple(out_specs),
        out_shape=tuple(out_shape),
        scratch_shapes=[pltpu.VMEM((d, ff), BF16), pltpu.VMEM((d, ff), BF16), pltpu.VMEM((ff, d), BF16)],
        compiler_params=_cparams(("parallel", "arbitrary")),
        name="moe_ffn",
    )(*sets, w_gate, w_up, w_down)


def _combine_kernel(soff_ref, relw_ref, key_ref, y_ref, xn_ref, mod_ref, o_ref, rhs_ref,
                    *, cap, d, n_exp, nb, win, grp):
    j = pl.program_id(1)
    stride = nb + 1
    blk = relw_ref.shape[-1]
    dn_t = (((0,), (0,)), ((), ()))

    def window_start(e):
        return _window_start(soff_ref, e, j, stride, cap, win)

    row = lax.broadcasted_iota(jnp.int32, (grp * win, blk), 0)
    acc = jnp.zeros((blk, d), F32)
    for g0 in range(0, n_exp, grp):
        slabs = []
        for q in range(grp):
            e = g0 + q
            _, start = window_start(e)
            rhs_ref[q * win:(q + 1) * win, :] = y_ref[0, e, pl.ds(pl.multiple_of(start, 16), win), :]
            slabs.append(jnp.broadcast_to(relw_ref[0, e:e + 1, :], (win, blk)))
        onehot = jnp.where(jnp.concatenate(slabs, axis=0) == row, 1.0, 0.0).astype(BF16)
        acc = acc + lax.dot_general(onehot, rhs_ref[...], dn_t, preferred_element_type=F32)
    o_ref[0] = acc

    row_w = lax.broadcasted_iota(jnp.int32, (win, blk), 0)

    def further_windows(e, carry):
        s, start0 = window_start(e)
        count = soff_ref[0, 0, e * stride + j + 1] - s
        n_win = (s - start0 + count + win - 1) // win

        def body(k, c):
            lo = start0 + k * win
            st = pl.multiple_of(jnp.minimum(lo, cap - win), 16)
            krow = key_ref[0, pl.ds(e, 1), :]
            onehot = jnp.where((krow - st == row_w) & (krow >= lo), 1.0, 0.0).astype(BF16)
            o_ref[0] = o_ref[0] + lax.dot_general(onehot, y_ref[0, e, pl.ds(st, win), :], dn_t,
                                                  preferred_element_type=F32)
            return c

        lax.fori_loop(1, n_win, body, 0)
        return carry

    @pl.when(soff_ref[0, 0, n_exp * stride + j] > 0)
    def _():
        lax.fori_loop(0, n_exp, further_windows, 0)

    o_ref[0] = xn_ref[0] + mod_ref[0][:, 5 * d:6 * d] * o_ref[0]


def _combine(y, soff, relw, key, xn, mod3, mod_row):
    b, e, cap, d = y.shape
    n = xn.shape[1]
    n6 = mod3.shape[-1]
    blk = min(ROUTE_BLOCK, n)
    nb = n // blk
    win, grp = _combine_geometry(cap)
    return pl.pallas_call(
        functools.partial(_combine_kernel, cap=cap, d=d, n_exp=e, nb=nb, win=win, grp=grp),
        grid=(b, nb),
        in_specs=[
            pl.BlockSpec((1, 1, e * (nb + 1) + nb), lambda bi, ji: (bi, 0, 0), memory_space=pltpu.SMEM),
            pl.BlockSpec((1, e, blk), lambda bi, ji: (bi, 0, ji)),
            pl.BlockSpec((1, e, blk), lambda bi, ji: (bi, 0, ji)),
            pl.BlockSpec((1, e, cap, d), lambda bi, ji: (bi, 0, 0, 0)),
            pl.BlockSpec((1, blk, d), lambda bi, ji: (bi, ji, 0)),
            pl.BlockSpec((1, 1, n6), lambda bi, ji: (mod_row(bi), 0, 0)),
        ],
        out_specs=pl.BlockSpec((1, blk, d), lambda bi, ji: (bi, ji, 0)),
        out_shape=jax.ShapeDtypeStruct((b, n, d), F32),
        scratch_shapes=[pltpu.VMEM((grp * win, d), BF16)],
        compiler_params=_cparams(("parallel", "arbitrary")),
        name="moe_combine",
    )(_route_scalars(soff, nb), relw, key, y, xn, mod3)


def _moe_dispatch(h_aug, aff_t):
    n = h_aug.shape[1]
    cap = EC_CAPACITY_FACTOR * n // N_EXPERTS
    key, relw, soff = _route(aff_t, cap)
    return _gather(h_aug, soff, relw, key, cap), (soff, relw, key)


def _rope_tables(n):
    t = jnp.arange(n, dtype=jnp.int32)
    row = (t // GRID_W).astype(F32)
    col = (t % GRID_W).astype(F32)
    inv = ROPE_THETA ** (-jnp.arange(0, ROPE_HALF, 2, dtype=F32) / ROPE_HALF)
    ar = row[:, None] * inv[None, :]
    ac = col[:, None] * inv[None, :]
    ang = jnp.concatenate([ar, ar, ac, ac], axis=-1)
    lane = jnp.arange(HEAD_DIM)
    sign = jnp.where((lane % ROPE_HALF) < ROPE_HALF // 2, -1.0, 1.0).astype(F32)
    return jnp.cos(ang), jnp.sin(ang) * sign[None, :]


def _block_diag(blocks):
    g, a, c = blocks.shape
    eye = jnp.eye(g, dtype=blocks.dtype)
    return (eye[:, None, :, None] * blocks[:, :, None, :]).reshape(g * a, g * c)


def kernel(x, c, ctx, c_ctx, ada_w, ada_b, norm1_g, norm2_g, w_in, w_fourier, w_pool, pool_scale,
           q_norm_g, k_norm_g, w_out, w_router, w_gate, w_up, w_down):
    b, n, d = x.shape
    lc = ctx.shape[1]
    depth = ada_w.shape[0]
    fw = d // 4

    rows = -(-(b + 1) // SUBLANES) * SUBLANES
    cond = jnp.zeros((rows, d), F32).at[:b].set(c).at[b].set(c_ctx)
    mod = _ada_mod(cond, ada_w, ada_b)

    def lat_row(bi):
        return bi

    def ctx_row(bi):
        return b

    norm1_g = norm1_g.reshape(depth, 1, d)
    norm2_g = norm2_g.reshape(depth, 1, d)
    q_norm_g = q_norm_g.reshape(depth, 1, HEAD_DIM)
    k_norm_g = k_norm_g.reshape(depth, 1, HEAD_DIM)
    pool_scale = pool_scale.reshape(depth, 1, d // 4)
    w_in_bf = w_in.astype(BF16)
    w_out_bf = w_out.astype(BF16)
    w_fourier_bf = w_fourier.astype(BF16)
    w_router_pad = jnp.zeros((depth, d, LANES), BF16).at[:, :, :N_EXPERTS].set(w_router.astype(BF16))
    w_pool_bd = jax.vmap(_block_diag)(w_pool).astype(BF16)

    cnt_t = _pool_counts(n, d // 4)
    cnt_c = _pool_counts(lc, d // 4)
    cos_t, sin_t = _rope_tables(n)
    cos_c, sin_c = _rope_tables(lc)
    cn, sn = (t.astype(BF16) for t in _dft_tables(n))
    cnc, snc = (t.astype(BF16) for t in _dft_tables(lc))
    hd = fw // N_FOURIER_HEADS
    cch, sch = _dft_tables(hd)
    cc_bd = _block_diag(jnp.broadcast_to(cch, (N_FOURIER_HEADS, hd, hd))).astype(BF16)
    sc_bd = _block_diag(jnp.broadcast_to(sch, (N_FOURIER_HEADS, hd, hd))).astype(BF16)

    for i in range(depth):
        last = i == depth - 1
        mod3 = mod[i].reshape(rows, 1, 6 * d)

        fx, px, qx, kx, vtx = _inproj(x, mod3, lat_row, i, norm1_g, w_in_bf, q_norm_g, k_norm_g,
                                      cos_t, sin_t, True)
        fc, pc, qc, kc, vtc = _inproj(ctx, mod3, ctx_row, i, norm1_g, w_in_bf, q_norm_g, k_norm_g,
                                      cos_c, sin_c, False)

        ax = _attention(qx, [kx, kc], [vtx, vtc])
        fox = _fourier(fx, i, w_fourier_bf, cn, sn, cc_bd, sc_bd)
        pox = _pool(px, cnt_t, i, w_pool_bd, pool_scale)
        xn, hx, affx = _outproj(fox, pox, ax, x, mod3, lat_row, i, norm2_g, w_out_bf, w_router_pad)
        xs, route_x = _moe_dispatch(hx, affx)

        if last:
            (yx,) = _ffn([xs], i, w_gate, w_up, w_down)
        else:
            ac = _attention(qc, [kc], [vtc])
            foc = _fourier(fc, i, w_fourier_bf, cnc, snc, cc_bd, sc_bd)
            poc = _pool(pc, cnt_c, i, w_pool_bd, pool_scale)
            cn_, hc, affc = _outproj(foc, poc, ac, ctx, mod3, ctx_row, i, norm2_g, w_out_bf, w_router_pad)
            xsc, route_c = _moe_dispatch(hc, affc)
            yx, yc = _ffn([xs, xsc], i, w_gate, w_up, w_down)
            ctx = _combine(yc, *route_c, cn_, mod3, ctx_row)
        x = _combine(yx, *route_x, xn, mod3, lat_row)
    return x
```

```python
import functools

import jax
import jax.numpy as jnp
from jax import lax
from jax.experimental import pallas as pl
from jax.experimental.pallas import tpu as pltpu

GRID_W = 64
EPS = 1e-6
N_FOURIER_HEADS = 4
POOL_WINDOWS = (2, 4, 8, 16)
HEAD_DIM = 128
N_KV_HEADS = 2
ROPE_HALF = HEAD_DIM // 2
ROPE_THETA = 10000.0
N_EXPERTS = 16
EC_CAPACITY_FACTOR = 2

LANES = 128
SUBLANES = 8
VMEM_LIMIT_BYTES = 56 * 1024 * 1024

LOG2_E = 1.4426950408889634

F32 = jnp.float32
BF16 = jnp.bfloat16


def _cparams(semantics):
    return pltpu.CompilerParams(dimension_semantics=semantics, vmem_limit_bytes=VMEM_LIMIT_BYTES)


def _row_tile(n, target):
    t = min(n, target)
    assert n % t == 0
    return t


def _ada_kernel(c_ref, w_ref, b_ref, o_ref):
    c = c_ref[...]
    s = c * jax.nn.sigmoid(c)
    r = s.shape[0]
    s_hi = s.astype(BF16)
    s_r1 = s - s_hi.astype(F32)
    s_mid = s_r1.astype(BF16)
    s_lo = (s_r1 - s_mid.astype(F32)).astype(BF16)
    lhs = jnp.concatenate([s_hi, s_mid, s_lo], axis=0)
    w = w_ref[0]
    w_hi = w.astype(BF16)
    w_lo = (w - w_hi.astype(F32)).astype(BF16)
    p_hi = jnp.dot(lhs, w_hi, preferred_element_type=F32)
    p_lo = jnp.dot(lhs, w_lo, preferred_element_type=F32)
    o_ref[0] = (p_lo[r:2 * r] + p_lo[0:r]) + (p_hi[2 * r:3 * r] + p_hi[r:2 * r]) + p_hi[0:r] + b_ref[0]


def _ada_mod(cond, ada_w, ada_b):
    depth, d, n6 = ada_w.shape
    r = cond.shape[0]
    tn = _row_tile(n6, 1536)
    return pl.pallas_call(
        _ada_kernel,
        grid=(depth, n6 // tn),
        in_specs=[
            pl.BlockSpec((r, d), lambda i, j: (0, 0)),
            pl.BlockSpec((1, d, tn), lambda i, j: (i, 0, j)),
            pl.BlockSpec((1, 1, tn), lambda i, j: (i, 0, j)),
        ],
        out_specs=pl.BlockSpec((1, r, tn), lambda i, j: (i, 0, j)),
        out_shape=jax.ShapeDtypeStruct((depth, r, n6), F32),
        compiler_params=_cparams(("arbitrary", "arbitrary")),
        name="ada_mod",
    )(cond, ada_w, ada_b.reshape(depth, 1, n6))


def _modulate(xf, g, shift, scale):
    ms = jnp.mean(xf * xf, axis=-1, keepdims=True)
    y = xf * lax.rsqrt(ms + EPS) * g
    return y * (1.0 + scale) + shift


def _head_rms(xh, g):
    ms = jnp.mean(xh * xh, axis=-1, keepdims=True)
    return xh * lax.rsqrt(ms + EPS) * g


def _rope(xh, cos, sin_signed, lo_half):
    partner = jnp.where(lo_half, pltpu.roll(xh, HEAD_DIM - ROPE_HALF // 2, 1), pltpu.roll(xh, ROPE_HALF // 2, 1))
    return xh * cos + partner * sin_signed


def _inproj_kernel(x_ref, mod_ref, g_ref, w_ref, qg_ref, kg_ref, cos_ref, sin_ref,
                   f_ref, p_ref, q_ref, k_ref, vt_ref, pr0_ref, pr1_ref, *, d, fw, pw, aw, kvw, use_rope):
    step = pl.program_id(0)

    @pl.when(step == 0)
    def _():
        pr1_ref[...] = jnp.zeros_like(pr1_ref)

    def project(dst_ref):
        mod = mod_ref[0]
        h = _modulate(x_ref[0], g_ref[0], mod[:, 0:d], mod[:, d:2 * d]).astype(BF16)
        dst_ref[...] = jnp.dot(h, w_ref[0], preferred_element_type=F32)

    def emit(src_ref):
        f_ref[0] = src_ref[:, 0:fw].astype(BF16)
        p_ref[0] = src_ref[:, fw:fw + pw]
        q_off = fw + pw
        k_off = q_off + aw
        v_off = k_off + kvw
        if use_rope:
            cos = cos_ref[...]
            sin = sin_ref[...]
            lane = lax.broadcasted_iota(jnp.int32, cos.shape, 1)
            lo_half = (lane % ROPE_HALF) < (ROPE_HALF // 2)
        q_scale = HEAD_DIM ** -0.5 * LOG2_E
        for j in range(aw // HEAD_DIM):
            qh = _head_rms(src_ref[:, q_off + j * HEAD_DIM:q_off + (j + 1) * HEAD_DIM], qg_ref[0])
            if use_rope:
                qh = _rope(qh, cos, sin, lo_half)
            q_ref[0, :, j * HEAD_DIM:(j + 1) * HEAD_DIM] = (qh * q_scale).astype(BF16)
        for j in range(kvw // HEAD_DIM):
            kh = _head_rms(src_ref[:, k_off + j * HEAD_DIM:k_off + (j + 1) * HEAD_DIM], kg_ref[0])
            if use_rope:
                kh = _rope(kh, cos, sin, lo_half)
            k_ref[0, :, j * HEAD_DIM:(j + 1) * HEAD_DIM] = kh.astype(BF16)
        vt_ref[0] = src_ref[:, v_off:v_off + kvw].T.astype(BF16)

    @pl.when(step % 2 == 0)
    def _():
        project(pr0_ref)
        emit(pr1_ref)

    @pl.when(step % 2 == 1)
    def _():
        project(pr1_ref)
        emit(pr0_ref)


def _inproj(x, mod3, mod_row, layer, norm_g, w_in_bf, qg, kg, cos_t, sin_t, use_rope):
    b, n, d = x.shape
    in_w = w_in_bf.shape[-1]
    fw = d // 4
    pw = d // 4
    aw = d // 2
    kvw = N_KV_HEADS * HEAD_DIM
    tm = _row_tile(n, 512)
    nt = n // tm
    total = b * nt
    n6 = mod3.shape[-1]
    kern = functools.partial(_inproj_kernel, d=d, fw=fw, pw=pw, aw=aw, kvw=kvw, use_rope=use_rope)
    out_shapes = (
        jax.ShapeDtypeStruct((b, n, fw), BF16),
        jax.ShapeDtypeStruct((b, n, pw), F32),
        jax.ShapeDtypeStruct((b, n, aw), BF16),
        jax.ShapeDtypeStruct((b, n, kvw), BF16),
        jax.ShapeDtypeStruct((b, kvw, n), BF16),
    )

    def cur(s):
        return jnp.minimum(s, total - 1)

    def prev(s):
        return jnp.maximum(s - 1, 0)

    def row_spec(w):
        return pl.BlockSpec((1, tm, w), lambda s: (prev(s) // nt, prev(s) % nt, 0))

    return pl.pallas_call(
        kern,
        grid=(total + 1,),
        in_specs=[
            pl.BlockSpec((1, tm, d), lambda s: (cur(s) // nt, cur(s) % nt, 0)),
            pl.BlockSpec((1, 1, n6), lambda s: (mod_row(cur(s) // nt), 0, 0)),
            pl.BlockSpec((1, 1, d), lambda s: (layer, 0, 0)),
            pl.BlockSpec((1, d, in_w), lambda s: (layer, 0, 0)),
            pl.BlockSpec((1, 1, HEAD_DIM), lambda s: (layer, 0, 0)),
            pl.BlockSpec((1, 1, HEAD_DIM), lambda s: (layer, 0, 0)),
            pl.BlockSpec((tm, HEAD_DIM), lambda s: (prev(s) % nt, 0)),
            pl.BlockSpec((tm, HEAD_DIM), lambda s: (prev(s) % nt, 0)),
        ],
        out_specs=(row_spec(fw), row_spec(pw), row_spec(aw), row_spec(kvw),
                   pl.BlockSpec((1, kvw, tm), lambda s: (prev(s) // nt, 0, prev(s) % nt))),
        out_shape=out_shapes,
        scratch_shapes=[pltpu.VMEM((tm, in_w), F32), pltpu.VMEM((tm, in_w), F32)],
        compiler_params=_cparams(("arbitrary",)),
        name="inproj",
    )(x, mod3, norm_g, w_in_bf, qg, kg, cos_t, sin_t)


ATTN_CHUNK = 256
ATTN_SLAB = 64


def _attn_kernel(*refs, n_src, group, n_chunks, chunk):
    q_ref = refs[0]
    k_srcs = refs[1:1 + n_src]
    vt_srcs = refs[1 + n_src:1 + 2 * n_src]
    o_ref, s0_ref, s1_ref = refs[1 + 2 * n_src:4 + 2 * n_src]
    if n_src == 1:
        k_ref, vt_ref = k_srcs[0].at[0], vt_srcs[0].at[0]
    else:
        k_ref, vt_ref = refs[4 + 2 * n_src:]
        row = 0
        for ks, vts in zip(k_srcs, vt_srcs):
            rows = ks.shape[1]
            k_ref[row:row + rows, :] = ks[0]
            vt_ref[:, row:row + rows] = vts[0]
            row += rows
    dn = (((1,), (1,)), ((), ()))

    def scores(c, s_ref):
        r0 = pl.multiple_of(c * chunk, chunk)
        for g in range(group):
            q = q_ref[0, pl.ds(r0, chunk), g * HEAD_DIM:(g + 1) * HEAD_DIM]
            s_ref[g] = lax.dot_general(k_ref[...], q, dn, preferred_element_type=F32)

    def finish(c, s_ref):
        r0 = pl.multiple_of(c * chunk, chunk)
        for g in range(group):
            s = s_ref[g]
            nk = s.shape[0]
            slab = ATTN_SLAB if nk % ATTN_SLAB == 0 else nk
            m = s.reshape(nk // slab, slab, chunk).max(axis=0).max(axis=0, keepdims=True)
            p = jnp.exp2(s - m)
            l = p.reshape(nk // slab, slab, chunk).sum(axis=0).sum(axis=0, keepdims=True)
            ot = jnp.dot(vt_ref[...], p.astype(BF16), preferred_element_type=F32)
            o_ref[0, pl.ds(r0, chunk), g * HEAD_DIM:(g + 1) * HEAD_DIM] = (ot / l).T.astype(BF16)

    scores(0, s0_ref)
    if n_chunks == 1:
        finish(0, s0_ref)
        return

    def body(i, carry):
        c = 2 * i
        scores(c + 1, s1_ref)
        finish(c, s0_ref)
        scores(c + 2, s0_ref)
        finish(c + 1, s1_ref)
        return carry

    lax.fori_loop(0, n_chunks // 2 - 1, body, 0)
    scores(n_chunks - 1, s1_ref)
    finish(n_chunks - 2, s0_ref)
    finish(n_chunks - 1, s1_ref)


def _attention(q, ks, vts):
    b, n, aw = q.shape
    nk = sum(k.shape[1] for k in ks)
    n_src = len(ks)
    group = aw // HEAD_DIM // N_KV_HEADS
    gw = group * HEAD_DIM
    chunk = min(ATTN_CHUNK, n)
    n_chunks = n // chunk
    assert n % chunk == 0 and (n_chunks == 1 or n_chunks % 2 == 0)
    scratch = [pltpu.VMEM((group, nk, chunk), F32), pltpu.VMEM((group, nk, chunk), F32)]
    if n_src > 1:
        scratch += [pltpu.VMEM((nk, HEAD_DIM), BF16), pltpu.VMEM((HEAD_DIM, nk), BF16)]
    return pl.pallas_call(
        functools.partial(_attn_kernel, n_src=n_src, group=group, n_chunks=n_chunks, chunk=chunk),
        grid=(b, N_KV_HEADS),
        in_specs=[pl.BlockSpec((1, n, gw), lambda bi, hi: (bi, 0, hi))]
        + [pl.BlockSpec((1, k.shape[1], HEAD_DIM), lambda bi, hi: (bi, 0, hi)) for k in ks]
        + [pl.BlockSpec((1, HEAD_DIM, v.shape[2]), lambda bi, hi: (bi, hi, 0)) for v in vts],
        out_specs=pl.BlockSpec((1, n, gw), lambda bi, hi: (bi, 0, hi)),
        out_shape=jax.ShapeDtypeStruct((b, n, aw), BF16),
        scratch_shapes=scratch,
        compiler_params=_cparams(("parallel", "arbitrary")),
        name="attention",
    )(q, *ks, *vts)


def _fourier_kernel(f_ref, cc_ref, sc_ref, cn_ref, sn_ref, w_ref, o_ref, xc_ref, xs_ref, *, scale):
    bi = pl.program_id(1)

    @pl.when(pl.program_id(0) == 0)
    def _():
        f = f_ref[0]
        xc_ref[bi] = jnp.dot(f, cc_ref[...], preferred_element_type=F32).astype(BF16)
        xs_ref[bi] = jnp.dot(f, sc_ref[...], preferred_element_type=F32).astype(BF16)

    fr = (jnp.dot(cn_ref[...], xc_ref[bi], preferred_element_type=F32)
          - jnp.dot(sn_ref[...], xs_ref[bi], preferred_element_type=F32)) * scale
    o_ref[0] = jnp.dot(fr.astype(BF16), w_ref[0], preferred_element_type=F32).astype(BF16)


def _dft_tables(n):
    k = jnp.arange(n, dtype=jnp.int32)

    def cos_sin(t):
        ang = ((k[:, None] * t[None, :]) % n).astype(F32) * (2.0 * jnp.pi / n)
        return jnp.cos(ang), jnp.sin(ang)

    if n <= 2 * LANES or n % LANES:
        return cos_sin(k)
    c_hi, s_hi = (t[:, :, None] for t in cos_sin(jnp.arange(n // LANES, dtype=jnp.int32) * LANES))
    c_lo, s_lo = (t[:, None, :] for t in cos_sin(jnp.arange(LANES, dtype=jnp.int32)))
    return (c_hi * c_lo - s_hi * s_lo).reshape(n, n), (s_hi * c_lo + c_hi * s_lo).reshape(n, n)


def _fourier(f, layer, w_fourier_bf, cn, sn, cc_bd, sc_bd):
    b, n, fw = f.shape
    tm = _row_tile(n, 512)
    scale = float((n * (fw // N_FOURIER_HEADS)) ** -0.5)
    return pl.pallas_call(
        functools.partial(_fourier_kernel, scale=scale),
        grid=(n // tm, b),
        in_specs=[
            pl.BlockSpec((1, n, fw), lambda ti, bi: (jnp.where(ti == 0, bi, b - 1), 0, 0)),
            pl.BlockSpec((fw, fw), lambda ti, bi: (0, 0)),
            pl.BlockSpec((fw, fw), lambda ti, bi: (0, 0)),
            pl.BlockSpec((tm, n), lambda ti, bi: (ti, 0)),
            pl.BlockSpec((tm, n), lambda ti, bi: (ti, 0)),
            pl.BlockSpec((1, fw, fw), lambda ti, bi: (layer, 0, 0)),
        ],
        out_specs=pl.BlockSpec((1, tm, fw), lambda ti, bi: (bi, ti, 0)),
        out_shape=jax.ShapeDtypeStruct((b, n, fw), BF16),
        scratch_shapes=[pltpu.VMEM((b, n, fw), BF16), pltpu.VMEM((b, n, fw), BF16)],
        compiler_params=_cparams(("arbitrary", "arbitrary")),
        name="fourier",
    )(f, cc_bd, sc_bd, cn, sn, w_fourier_bf)


POOL_HALO = 16
POOL_EDGE = 8


def _pool_kernel(p_ref, cnt_ref, w_ref, s_ref, o_ref, x_ref, s2_ref, s4_ref, s8_ref, *, n, pw):
    gdim = pw // len(POOL_WINDOWS)
    lo, hi = POOL_EDGE, n + 2 * POOL_HALO - POOL_EDGE
    for ref in (x_ref, s2_ref, s4_ref, s8_ref):
        ref[0:POOL_HALO, :] = jnp.zeros((POOL_HALO, pw), F32)
        ref[n + POOL_HALO:n + 2 * POOL_HALO, :] = jnp.zeros((POOL_HALO, pw), F32)
    x_ref[POOL_HALO:POOL_HALO + n, :] = p_ref[0]

    s2_ref[lo:hi, :] = x_ref[lo - 1:hi - 1, :] + x_ref[lo:hi, :]
    s4_ref[lo:hi, :] = s2_ref[lo - 1:hi - 1, :] + s2_ref[lo + 1:hi + 1, :]
    s8_ref[lo:hi, :] = s4_ref[lo - 2:hi - 2, :] + s4_ref[lo + 2:hi + 2, :]
    a, b = POOL_HALO, POOL_HALO + n
    s16 = s8_ref[a - 4:b - 4, :] + s8_ref[a + 4:b + 4, :]
    lane = lax.broadcasted_iota(jnp.int32, (1, pw), 1)
    acc = jnp.where(lane < gdim, s2_ref[a:b, :],
                    jnp.where(lane < 2 * gdim, s4_ref[a:b, :], jnp.where(lane < 3 * gdim, s8_ref[a:b, :], s16)))
    dlt = acc / cnt_ref[...] - x_ref[a:b, :]
    y = jnp.dot(dlt.astype(BF16), w_ref[0], preferred_element_type=F32) * s_ref[0]
    o_ref[0] = y.astype(BF16)


def _pool_counts(n, pw):
    gdim = pw // len(POOL_WINDOWS)
    half = jnp.repeat(jnp.asarray([w // 2 for w in POOL_WINDOWS], jnp.int32), gdim)[None, :]
    t = jnp.arange(n, dtype=jnp.int32)[:, None]
    return (jnp.minimum(t + half, n) - jnp.maximum(t - half, 0)).astype(F32)


def _pool(p, cnt, layer, w_pool_bd, pool_scale):
    b, n, pw = p.shape
    assert POOL_WINDOWS == (2, 4, 8, 16)
    buf = pltpu.VMEM((n + 2 * POOL_HALO, pw), F32)
    return pl.pallas_call(
        functools.partial(_pool_kernel, n=n, pw=pw),
        grid=(b,),
        in_specs=[
            pl.BlockSpec((1, n, pw), lambda bi: (bi, 0, 0)),
            pl.BlockSpec((n, pw), lambda bi: (0, 0)),
            pl.BlockSpec((1, pw, pw), lambda bi: (layer, 0, 0)),
            pl.BlockSpec((1, 1, pw), lambda bi: (layer, 0, 0)),
        ],
        out_specs=pl.BlockSpec((1, n, pw), lambda bi: (bi, 0, 0)),
        out_shape=jax.ShapeDtypeStruct((b, n, pw), BF16),
        scratch_shapes=[buf, buf, buf, buf],
        compiler_params=_cparams(("parallel",)),
        name="pool",
    )(p, cnt, w_pool_bd, pool_scale)


def _outproj_kernel(fo_ref, po_ref, ao_ref, x_ref, modc_ref, modp_ref, g_ref, w_ref, wr_ref,
                    xn_ref, h_ref, aff_ref, xn0_ref, xn1_ref, *, d, fw, pw):
    step = pl.program_id(0)

    @pl.when(step == 0)
    def _():
        xn1_ref[...] = jnp.zeros_like(xn1_ref)

    def project(dst_ref):
        w = w_ref[0]
        ox = jnp.dot(fo_ref[0], w[0:fw], preferred_element_type=F32)
        ox = ox + jnp.dot(po_ref[0], w[fw:fw + pw], preferred_element_type=F32)
        ox = ox + jnp.dot(ao_ref[0], w[fw + pw:], preferred_element_type=F32)
        dst_ref[...] = x_ref[0] + modc_ref[0][:, 2 * d:3 * d] * ox

    def emit(src_ref):
        mod = modp_ref[0]
        xn = src_ref[...]
        xn_ref[0] = xn
        h = _modulate(xn, g_ref[0], mod[:, 3 * d:4 * d], mod[:, 4 * d:5 * d])
        logits = jnp.dot(h.astype(BF16), wr_ref[0], preferred_element_type=F32)
        lane = lax.broadcasted_iota(jnp.int32, logits.shape, 1)
        logits = jnp.where(lane < N_EXPERTS, logits, -jnp.inf)
        e = jnp.exp(logits - logits.max(axis=-1, keepdims=True))
        aff = e / e.sum(axis=-1, keepdims=True)
        aff_ref[0] = aff.T[0:N_EXPERTS, :]
        a_hi = aff.astype(BF16).astype(F32)
        r1 = aff - a_hi
        a_mid = r1.astype(BF16).astype(F32)
        a_lo = (r1 - a_mid).astype(BF16).astype(F32)
        tail = a_hi + pltpu.roll(a_mid, N_EXPERTS, 1) + pltpu.roll(a_lo, 2 * N_EXPERTS, 1)
        h_ref[0, :, 0:d] = h.astype(BF16)
        h_ref[0, :, d:d + LANES] = tail.astype(BF16)

    @pl.when(step % 2 == 0)
    def _():
        project(xn0_ref)
        emit(xn1_ref)

    @pl.when(step % 2 == 1)
    def _():
        project(xn1_ref)
        emit(xn0_ref)


def _outproj(fo, po, ao, x, mod3, mod_row, layer, norm_g, w_out_bf, w_router_pad):
    b, n, d = x.shape
    fw = fo.shape[-1]
    pw = po.shape[-1]
    aw = ao.shape[-1]
    n6 = mod3.shape[-1]
    tm = _row_tile(n, 512)
    nt = n // tm
    total = b * nt

    def cur(s):
        return jnp.minimum(s, total - 1)

    def prev(s):
        return jnp.maximum(s - 1, 0)

    def in_spec(w):
        return pl.BlockSpec((1, tm, w), lambda s: (cur(s) // nt, cur(s) % nt, 0))

    return pl.pallas_call(
        functools.partial(_outproj_kernel, d=d, fw=fw, pw=pw),
        grid=(total + 1,),
        in_specs=[
            in_spec(fw), in_spec(pw), in_spec(aw), in_spec(d),
            pl.BlockSpec((1, 1, n6), lambda s: (mod_row(cur(s) // nt), 0, 0)),
            pl.BlockSpec((1, 1, n6), lambda s: (mod_row(prev(s) // nt), 0, 0)),
            pl.BlockSpec((1, 1, d), lambda s: (layer, 0, 0)),
            pl.BlockSpec((1, d, d), lambda s: (layer, 0, 0)),
            pl.BlockSpec((1, d, LANES), lambda s: (layer, 0, 0)),
        ],
        out_specs=(pl.BlockSpec((1, tm, d), lambda s: (prev(s) // nt, prev(s) % nt, 0)),
                   pl.BlockSpec((1, tm, d + LANES), lambda s: (prev(s) // nt, prev(s) % nt, 0)),
                   pl.BlockSpec((1, N_EXPERTS, tm), lambda s: (prev(s) // nt, 0, prev(s) % nt))),
        out_shape=(jax.ShapeDtypeStruct((b, n, d), F32),
                   jax.ShapeDtypeStruct((b, n, d + LANES), BF16),
                   jax.ShapeDtypeStruct((b, N_EXPERTS, n), F32)),
        scratch_shapes=[pltpu.VMEM((tm, d), F32), pltpu.VMEM((tm, d), F32)],
        compiler_params=_cparams(("arbitrary",)),
        name="outproj",
    )(fo, po, ao, x, mod3, mod3, norm_g, w_out_bf, w_router_pad)


ROUTE_BLOCK = 256
COMBINE_WINDOW = 64
NOT_IN_WINDOW = -(1 << 20)
FLAG_LANE0 = LANES // 2


def _combine_geometry(cap):
    win = min(COMBINE_WINDOW, cap)
    return win, N_EXPERTS


def _route_kernel(aff_ref, key_ref, relw_ref, soff_ref, tri_ref, *, n, cap, blk, win, grp):
    nb = n // blk
    e_n = N_EXPERTS

    @pl.when(pl.program_id(0) == 0)
    def _():
        r = lax.broadcasted_iota(jnp.int32, (blk, blk), 0)
        c = lax.broadcasted_iota(jnp.int32, (blk, blk), 1)
        tri_ref[...] = jnp.where(r < c, 1.0, 0.0).astype(BF16)

    aff = aff_ref[0]

    def count_ge(bits):
        return jnp.sum(jnp.where(aff >= pltpu.bitcast(bits, F32), 1.0, 0.0), axis=-1, keepdims=True)

    def search(i, thr_bits):
        lo = lax.shift_left(jnp.int32(1), 29 - 2 * i)
        hi = lo + lo
        take_hi = count_ge(thr_bits | hi) >= cap
        take_both = count_ge(thr_bits | hi | lo) >= cap
        take_lo = count_ge(thr_bits | lo) >= cap
        with_hi = jnp.where(take_both, thr_bits | hi | lo, thr_bits | hi)
        without_hi = jnp.where(take_lo, thr_bits | lo, thr_bits)
        return jnp.where(take_hi, with_hi, without_hi)

    thr_bits = lax.fori_loop(0, 15, search, jnp.zeros((e_n, 1), jnp.int32))
    thr_bits = jnp.where(count_ge(thr_bits | 1) >= cap, thr_bits | 1, thr_bits)
    thr = pltpu.bitcast(thr_bits, F32)
    gt = jnp.where(aff > thr, 1.0, 0.0)
    eq = jnp.where(aff == thr, 1.0, 0.0)
    need = cap - jnp.sum(gt, axis=-1, keepdims=True)
    tri = tri_ref[...]

    sel = []
    off = jnp.zeros((e_n, 1), F32)
    for j in range(nb):
        eqj = eq[:, j * blk:(j + 1) * blk]
        rank = jnp.dot(eqj.astype(BF16), tri, preferred_element_type=F32) + off
        off = off + jnp.sum(eqj, axis=-1, keepdims=True)
        sel.append(jnp.maximum(gt[:, j * blk:(j + 1) * blk], jnp.where(rank < need, eqj, 0.0)))

    lane = lax.broadcasted_iota(jnp.int32, (e_n, LANES), 1)
    q_off = ((lax.broadcasted_iota(jnp.int32, (e_n, 1), 0) % grp) * win).astype(F32)
    soff = jnp.zeros((e_n, LANES), F32)
    off = jnp.zeros((e_n, 1), F32)
    for j in range(nb):
        selj = sel[j]
        pos = jnp.dot(selj.astype(BF16), tri, preferred_element_type=F32) + off
        soff = jnp.where(lane == j, off, soff)
        start = jnp.minimum(jnp.floor(off * (1.0 / 16.0)) * 16.0, float(cap - win))
        rel = pos - start
        chosen = selj > 0.0
        key_ref[0, :, j * blk:(j + 1) * blk] = jnp.where(chosen, pos, -1.0).astype(jnp.int32)
        relw_ref[0, :, j * blk:(j + 1) * blk] = jnp.where(
            chosen & (rel < win), rel + q_off, float(NOT_IN_WINDOW)).astype(jnp.int32)
        count = jnp.sum(selj, axis=-1, keepdims=True)
        spills = jnp.max(jnp.where(off - start + count > win, 1.0, 0.0), axis=0, keepdims=True)
        soff = jnp.where(lane == FLAG_LANE0 + j, spills, soff)
        off = off + count
    soff_ref[0] = jnp.where(lane == nb, off, soff).astype(jnp.int32)


def _route(aff_t, cap):
    b, e, n = aff_t.shape
    blk = min(ROUTE_BLOCK, n)
    assert n % blk == 0 and n // blk < FLAG_LANE0
    win, grp = _combine_geometry(cap)

    def spec(w):
        return pl.BlockSpec((1, e, w), lambda bi: (bi, 0, 0))

    return pl.pallas_call(
        functools.partial(_route_kernel, n=n, cap=cap, blk=blk, win=win, grp=grp),
        grid=(b,),
        in_specs=[spec(n)],
        out_specs=(spec(n), spec(n), spec(LANES)),
        out_shape=(jax.ShapeDtypeStruct((b, e, n), jnp.int32),
                   jax.ShapeDtypeStruct((b, e, n), jnp.int32),
                   jax.ShapeDtypeStruct((b, e, LANES), jnp.int32)),
        scratch_shapes=[pltpu.VMEM((blk, blk), BF16)],
        compiler_params=_cparams(("arbitrary",)),
        name="moe_route",
    )(aff_t)


def _window_start(soff_ref, e, j, stride, cap, win):
    s = soff_ref[0, 0, e * stride + j]
    return s, jnp.minimum((s >> 4) << 4, cap - win)


def _gather_kernel(soff_ref, relw_ref, key_ref, h_ref, o_ref, *, cap, n_exp, nb, win):
    j = pl.program_id(1)
    stride = nb + 1
    blk = relw_ref.shape[-1]

    @pl.when(j == 0)
    def _():
        o_ref[...] = jnp.zeros_like(o_ref)

    row = lax.broadcasted_iota(jnp.int32, (n_exp * win, blk), 0)
    slabs = [jnp.broadcast_to(relw_ref[0, e:e + 1, :], (win, blk)) for e in range(n_exp)]
    onehot = jnp.where(jnp.concatenate(slabs, axis=0) == row, 1.0, 0.0).astype(BF16)
    rows = jnp.dot(onehot, h_ref[0], preferred_element_type=F32).astype(BF16)
    for e in range(n_exp):
        _, start = _window_start(soff_ref, e, j, stride, cap, win)
        dst = o_ref.at[0, e, pl.ds(pl.multiple_of(start, 16), win), :]
        dst[...] = dst[...] + rows[e * win:(e + 1) * win]

    row_w = lax.broadcasted_iota(jnp.int32, (win, blk), 0)

    def further_windows(e, carry):
        s, start0 = _window_start(soff_ref, e, j, stride, cap, win)
        count = soff_ref[0, 0, e * stride + j + 1] - s
        n_win = (s - start0 + count + win - 1) // win

        def body(k, c):
            lo = start0 + k * win
            st = pl.multiple_of(jnp.minimum(lo, cap - win), 16)
            krow = key_ref[0, pl.ds(e, 1), :]
            hot = jnp.where((krow - st == row_w) & (krow >= lo), 1.0, 0.0).astype(BF16)
            dst = o_ref.at[0, e, pl.ds(st, win), :]
            dst[...] = dst[...] + jnp.dot(hot, h_ref[0], preferred_element_type=F32).astype(BF16)
            return c

        lax.fori_loop(1, n_win, body, 0)
        return carry

    @pl.when(soff_ref[0, 0, n_exp * stride + j] > 0)
    def _():
        lax.fori_loop(0, n_exp, further_windows, 0)


def _route_scalars(soff, nb):
    b, e, _ = soff.shape
    return jnp.concatenate([soff[:, :, :nb + 1].reshape(b, 1, e * (nb + 1)),
                            soff[:, :1, FLAG_LANE0:FLAG_LANE0 + nb]], axis=-1)


def _gather(h_aug, soff, relw, key, cap):
    b, n, dw = h_aug.shape
    e = relw.shape[1]
    blk = min(ROUTE_BLOCK, n)
    nb = n // blk
    win, _ = _combine_geometry(cap)
    return pl.pallas_call(
        functools.partial(_gather_kernel, cap=cap, n_exp=e, nb=nb, win=win),
        grid=(b, nb),
        in_specs=[
            pl.BlockSpec((1, 1, e * (nb + 1) + nb), lambda bi, ji: (bi, 0, 0), memory_space=pltpu.SMEM),
            pl.BlockSpec((1, e, blk), lambda bi, ji: (bi, 0, ji)),
            pl.BlockSpec((1, e, blk), lambda bi, ji: (bi, 0, ji)),
            pl.BlockSpec((1, blk, dw), lambda bi, ji: (bi, ji, 0)),
        ],
        out_specs=pl.BlockSpec((1, e, cap, dw), lambda bi, ji: (bi, 0, 0, 0)),
        out_shape=jax.ShapeDtypeStruct((b, e, cap, dw), BF16),
        compiler_params=_cparams(("parallel", "arbitrary")),
        name="moe_gather",
    )(_route_scalars(soff, nb), relw, key, h_aug)


def _ffn_kernel(*refs, bt, n_sets, d):
    ins = refs[:n_sets]
    wg_ref, wu_ref, wd_ref = refs[n_sets:n_sets + 3]
    outs = refs[n_sets + 3:2 * n_sets + 3]
    wg_bf, wu_bf, wd_bf = refs[2 * n_sets + 3:]
    ei = pl.program_id(0)

    @pl.when(pl.program_id(1) == 0)
    def _():
        wg_bf[...] = wg_ref[0, 0].astype(BF16)
        wu_bf[...] = wu_ref[0, 0].astype(BF16)
        wd_bf[...] = wd_ref[0, 0].astype(BF16)

    def expert(rows):
        xs = rows[:, 0:d]
        tail = rows[:, d:d + LANES].astype(F32)
        lane = lax.broadcasted_iota(jnp.int32, tail.shape, 1)
        mine = (lane == ei) | (lane == ei + N_EXPERTS) | (lane == ei + 2 * N_EXPERTS)
        gate = jnp.sum(jnp.where(mine, tail, 0.0), axis=-1, keepdims=True)
        a = jnp.dot(xs, wg_bf[...], preferred_element_type=F32)
        u = jnp.dot(xs, wu_bf[...], preferred_element_type=F32)
        hid = (a * jax.nn.sigmoid(a) * u).astype(BF16)
        return (jnp.dot(hid, wd_bf[...], preferred_element_type=F32) * gate).astype(BF16)

    for xs_ref, y_ref in zip(ins, outs):
        cap, dw = xs_ref.shape[2], xs_ref.shape[3]
        if cap >= 256:
            for bi in range(bt):
                y_ref[bi, 0] = expert(xs_ref[bi, 0])
        else:
            y_ref[:, 0] = expert(xs_ref[:, 0].reshape(bt * cap, dw)).reshape(bt, cap, d)


def _ffn(sets, layer, w_gate, w_up, w_down):
    b, e, _, dw = sets[0].shape
    d = w_gate.shape[-2]
    ff = w_gate.shape[-1]
    bt = min(b, 4)
    assert b % bt == 0 and dw == d + LANES
    in_specs = [pl.BlockSpec((bt, 1, xs.shape[2], dw), lambda ei, ji: (ji, ei, 0, 0)) for xs in sets]
    out_specs = [pl.BlockSpec((bt, 1, xs.shape[2], d), lambda ei, ji: (ji, ei, 0, 0)) for xs in sets]
    out_shape = [jax.ShapeDtypeStruct(xs.shape[:3] + (d,), BF16) for xs in sets]
    in_specs += [
        pl.BlockSpec((1, 1, d, ff), lambda ei, ji: (layer, ei, 0, 0)),
        pl.BlockSpec((1, 1, d, ff), lambda ei, ji: (layer, ei, 0, 0)),
        pl.BlockSpec((1, 1, ff, d), lambda ei, ji: (layer, ei, 0, 0)),
    ]
    return pl.pallas_call(
        functools.partial(_ffn_kernel, bt=bt, n_sets=len(sets), d=d),
        grid=(e, b // bt),
        in_specs=in_specs,
        out_specs=tuple(out_specs),
        out_shape=tuple(out_shape),
        scratch_shapes=[pltpu.VMEM((d, ff), BF16), pltpu.VMEM((d, ff), BF16), pltpu.VMEM((ff, d), BF16)],
        compiler_params=_cparams(("parallel", "arbitrary")),
        name="moe_ffn",
    )(*sets, w_gate, w_up, w_down)


def _combine_kernel(soff_ref, relw_ref, key_ref, y_ref, xn_ref, mod_ref, o_ref, rhs_ref,
                    *, cap, d, n_exp, nb, win, grp):
    j = pl.program_id(1)
    stride = nb + 1
    blk = relw_ref.shape[-1]
    dn_t = (((0,), (0,)), ((), ()))

    def window_start(e):
        return _window_start(soff_ref, e, j, stride, cap, win)

    row = lax.broadcasted_iota(jnp.int32, (grp * win, blk), 0)
    acc = jnp.zeros((blk, d), F32)
    for g0 in range(0, n_exp, grp):
        slabs = []
        for q in range(grp):
            e = g0 + q
            _, start = window_start(e)
            rhs_ref[q * win:(q + 1) * win, :] = y_ref[0, e, pl.ds(pl.multiple_of(start, 16), win), :]
            slabs.append(jnp.broadcast_to(relw_ref[0, e:e + 1, :], (win, blk)))
        onehot = jnp.where(jnp.concatenate(slabs, axis=0) == row, 1.0, 0.0).astype(BF16)
        acc = acc + lax.dot_general(onehot, rhs_ref[...], dn_t, preferred_element_type=F32)
    o_ref[0] = acc

    row_w = lax.broadcasted_iota(jnp.int32, (win, blk), 0)

    def further_windows(e, carry):
        s, start0 = window_start(e)
        count = soff_ref[0, 0, e * stride + j + 1] - s
        n_win = (s - start0 + count + win - 1) // win

        def body(k, c):
            lo = start0 + k * win
            st = pl.multiple_of(jnp.minimum(lo, cap - win), 16)
            krow = key_ref[0, pl.ds(e, 1), :]
            onehot = jnp.where((krow - st == row_w) & (krow >= lo), 1.0, 0.0).astype(BF16)
            o_ref[0] = o_ref[0] + lax.dot_general(onehot, y_ref[0, e, pl.ds(st, win), :], dn_t,
                                                  preferred_element_type=F32)
            return c

        lax.fori_loop(1, n_win, body, 0)
        return carry

    @pl.when(soff_ref[0, 0, n_exp * stride + j] > 0)
    def _():
        lax.fori_loop(0, n_exp, further_windows, 0)

    o_ref[0] = xn_ref[0] + mod_ref[0][:, 5 * d:6 * d] * o_ref[0]


def _combine(y, soff, relw, key, xn, mod3, mod_row):
    b, e, cap, d = y.shape
    n = xn.shape[1]
    n6 = mod3.shape[-1]
    blk = min(ROUTE_BLOCK, n)
    nb = n // blk
    win, grp = _combine_geometry(cap)
    return pl.pallas_call(
        functools.partial(_combine_kernel, cap=cap, d=d, n_exp=e, nb=nb, win=win, grp=grp),
        grid=(b, nb),
        in_specs=[
            pl.BlockSpec((1, 1, e * (nb + 1) + nb), lambda bi, ji: (bi, 0, 0), memory_space=pltpu.SMEM),
            pl.BlockSpec((1, e, blk), lambda bi, ji: (bi, 0, ji)),
            pl.BlockSpec((1, e, blk), lambda bi, ji: (bi, 0, ji)),
            pl.BlockSpec((1, e, cap, d), lambda bi, ji: (bi, 0, 0, 0)),
            pl.BlockSpec((1, blk, d), lambda bi, ji: (bi, ji, 0)),
            pl.BlockSpec((1, 1, n6), lambda bi, ji: (mod_row(bi), 0, 0)),
        ],
        out_specs=pl.BlockSpec((1, blk, d), lambda bi, ji: (bi, ji, 0)),
        out_shape=jax.ShapeDtypeStruct((b, n, d), F32),
        scratch_shapes=[pltpu.VMEM((grp * win, d), BF16)],
        compiler_params=_cparams(("parallel", "arbitrary")),
        name="moe_combine",
    )(_route_scalars(soff, nb), relw, key, y, xn, mod3)


def _moe_dispatch(h_aug, aff_t):
    n = h_aug.shape[1]
    cap = EC_CAPACITY_FACTOR * n // N_EXPERTS
    key, relw, soff = _route(aff_t, cap)
    return _gather(h_aug, soff, relw, key, cap), (soff, relw, key)


def _rope_tables(n):
    t = jnp.arange(n, dtype=jnp.int32)
    row = (t // GRID_W).astype(F32)
    col = (t % GRID_W).astype(F32)
    inv = ROPE_THETA ** (-jnp.arange(0, ROPE_HALF, 2, dtype=F32) / ROPE_HALF)
    ar = row[:, None] * inv[None, :]
    ac = col[:, None] * inv[None, :]
    ang = jnp.concatenate([ar, ar, ac, ac], axis=-1)
    lane = jnp.arange(HEAD_DIM)
    sign = jnp.where((lane % ROPE_HALF) < ROPE_HALF // 2, -1.0, 1.0).astype(F32)
    return jnp.cos(ang), jnp.sin(ang) * sign[None, :]


def _block_diag(blocks):
    g, a, c = blocks.shape
    eye = jnp.eye(g, dtype=blocks.dtype)
    return (eye[:, None, :, None] * blocks[:, :, None, :]).reshape(g * a, g * c)


def kernel(x, c, ctx, c_ctx, ada_w, ada_b, norm1_g, norm2_g, w_in, w_fourier, w_pool, pool_scale,
           q_norm_g, k_norm_g, w_out, w_router, w_gate, w_up, w_down):
    b, n, d = x.shape
    lc = ctx.shape[1]
    depth = ada_w.shape[0]
    fw = d // 4

    rows = -(-(b + 1) // SUBLANES) * SUBLANES
    cond = jnp.zeros((rows, d), F32).at[:b].set(c).at[b].set(c_ctx)
    mod = _ada_mod(cond, ada_w, ada_b)

    def lat_row(bi):
        return bi

    def ctx_row(bi):
        return b

    norm1_g = norm1_g.reshape(depth, 1, d)
    norm2_g = norm2_g.reshape(depth, 1, d)
    q_norm_g = q_norm_g.reshape(depth, 1, HEAD_DIM)
    k_norm_g = k_norm_g.reshape(depth, 1, HEAD_DIM)
    pool_scale = pool_scale.reshape(depth, 1, d // 4)
    w_in_bf = w_in.astype(BF16)
    w_out_bf = w_out.astype(BF16)
    w_fourier_bf = w_fourier.astype(BF16)
    w_router_pad = jnp.zeros((depth, d, LANES), BF16).at[:, :, :N_EXPERTS].set(w_router.astype(BF16))
    w_pool_bd = jax.vmap(_block_diag)(w_pool).astype(BF16)

    cnt_t = _pool_counts(n, d // 4)
    cnt_c = _pool_counts(lc, d // 4)
    cos_t, sin_t = _rope_tables(n)
    cos_c, sin_c = _rope_tables(lc)
    cn, sn = (t.astype(BF16) for t in _dft_tables(n))
    cnc, snc = (t.astype(BF16) for t in _dft_tables(lc))
    hd = fw // N_FOURIER_HEADS
    cch, sch = _dft_tables(hd)
    cc_bd = _block_diag(jnp.broadcast_to(cch, (N_FOURIER_HEADS, hd, hd))).astype(BF16)
    sc_bd = _block_diag(jnp.broadcast_to(sch, (N_FOURIER_HEADS, hd, hd))).astype(BF16)

    for i in range(depth):
        last = i == depth - 1
        mod3 = mod[i].reshape(rows, 1, 6 * d)

        fx, px, qx, kx, vtx = _inproj(x, mod3, lat_row, i, norm1_g, w_in_bf, q_norm_g, k_norm_g,
                                      cos_t, sin_t, True)
        fc, pc, qc, kc, vtc = _inproj(ctx, mod3, ctx_row, i, norm1_g, w_in_bf, q_norm_g, k_norm_g,
                                      cos_c, sin_c, False)

        ax = _attention(qx, [kx, kc], [vtx, vtc])
        fox = _fourier(fx, i, w_fourier_bf, cn, sn, cc_bd, sc_bd)
        pox = _pool(px, cnt_t, i, w_pool_bd, pool_scale)
        xn, hx, affx = _outproj(fox, pox, ax, x, mod3, lat_row, i, norm2_g, w_out_bf, w_router_pad)
        xs, route_x = _moe_dispatch(hx, affx)

        if last:
            (yx,) = _ffn([xs], i, w_gate, w_up, w_down)
        else:
            ac = _attention(qc, [kc], [vtc])
            foc = _fourier(fc, i, w_fourier_bf, cnc, snc, cc_bd, sc_bd)
            poc = _pool(pc, cnt_c, i, w_pool_bd, pool_scale)
            cn_, hc, affc = _outproj(foc, poc, ac, ctx, mod3, ctx_row, i, norm2_g, w_out_bf, w_router_pad)
            xsc, route_c = _moe_dispatch(hc, affc)
            yx, yc = _ffn([xs, xsc], i, w_gate, w_up, w_down)
            ctx = _combine(yc, *route_c, cn_, mod3, ctx_row)
        x = _combine(yx, *route_x, xn, mod3, lat_row)
    return x
```

```python
import functools

import jax
import jax.numpy as jnp
from jax import lax
from jax.experimental import pallas as pl
from jax.experimental.pallas import tpu as pltpu

GRID_W = 64
EPS = 1e-6
N_FOURIER_HEADS = 4
POOL_WINDOWS = (2, 4, 8, 16)
HEAD_DIM = 128
N_KV_HEADS = 2
ROPE_HALF = HEAD_DIM // 2
ROPE_THETA = 10000.0
N_EXPERTS = 16
EC_CAPACITY_FACTOR = 2

LANES = 128
SUBLANES = 8
VMEM_LIMIT_BYTES = 56 * 1024 * 1024

LOG2_E = 1.4426950408889634

F32 = jnp.float32
BF16 = jnp.bfloat16


def _cparams(semantics):
    return pltpu.CompilerParams(dimension_semantics=semantics, vmem_limit_bytes=VMEM_LIMIT_BYTES)


def _row_tile(n, target):
    t = min(n, target)
    assert n % t == 0
    return t


def _ada_kernel(c_ref, w_ref, b_ref, o_ref):
    c = c_ref[...]
    s = c * jax.nn.sigmoid(c)
    r = s.shape[0]
    s_hi = s.astype(BF16)
    s_r1 = s - s_hi.astype(F32)
    s_mid = s_r1.astype(BF16)
    s_lo = (s_r1 - s_mid.astype(F32)).astype(BF16)
    lhs = jnp.concatenate([s_hi, s_mid, s_lo], axis=0)
    w = w_ref[0]
    w_hi = w.astype(BF16)
    w_lo = (w - w_hi.astype(F32)).astype(BF16)
    p_hi = jnp.dot(lhs, w_hi, preferred_element_type=F32)
    p_lo = jnp.dot(lhs, w_lo, preferred_element_type=F32)
    o_ref[0] = (p_lo[r:2 * r] + p_lo[0:r]) + (p_hi[2 * r:3 * r] + p_hi[r:2 * r]) + p_hi[0:r] + b_ref[0]


def _ada_mod(cond, ada_w, ada_b):
    depth, d, n6 = ada_w.shape
    r = cond.shape[0]
    tn = _row_tile(n6, 1536)
    return pl.pallas_call(
        _ada_kernel,
        grid=(depth, n6 // tn),
        in_specs=[
            pl.BlockSpec((r, d), lambda i, j: (0, 0)),
            pl.BlockSpec((1, d, tn), lambda i, j: (i, 0, j)),
            pl.BlockSpec((1, 1, tn), lambda i, j: (i, 0, j)),
        ],
        out_specs=pl.BlockSpec((1, r, tn), lambda i, j: (i, 0, j)),
        out_shape=jax.ShapeDtypeStruct((depth, r, n6), F32),
        compiler_params=_cparams(("arbitrary", "arbitrary")),
        name="ada_mod",
    )(cond, ada_w, ada_b.reshape(depth, 1, n6))


def _modulate(xf, g, shift, scale):
    ms = jnp.mean(xf * xf, axis=-1, keepdims=True)
    y = xf * lax.rsqrt(ms + EPS) * g
    return y * (1.0 + scale) + shift


def _head_rms(xh, g):
    ms = jnp.mean(xh * xh, axis=-1, keepdims=True)
    return xh * lax.rsqrt(ms + EPS) * g


def _rope(xh, cos, sin_signed, lo_half):
    partner = jnp.where(lo_half, pltpu.roll(xh, HEAD_DIM - ROPE_HALF // 2, 1), pltpu.roll(xh, ROPE_HALF // 2, 1))
    return xh * cos + partner * sin_signed


def _inproj_kernel(x_ref, mod_ref, g_ref, w_ref, qg_ref, kg_ref, cos_ref, sin_ref,
                   f_ref, p_ref, q_ref, k_ref, vt_ref, pr0_ref, pr1_ref, *, d, fw, pw, aw, kvw, use_rope):
    step = pl.program_id(0)

    @pl.when(step == 0)
    def _():
        pr1_ref[...] = jnp.zeros_like(pr1_ref)

    def project(dst_ref):
        mod = mod_ref[0]
        h = _modulate(x_ref[0], g_ref[0], mod[:, 0:d], mod[:, d:2 * d]).astype(BF16)
        dst_ref[...] = jnp.dot(h, w_ref[0], preferred_element_type=F32)

    def emit(src_ref):
        f_ref[0] = src_ref[:, 0:fw].astype(BF16)
        p_ref[0] = src_ref[:, fw:fw + pw]
        q_off = fw + pw
        k_off = q_off + aw
        v_off = k_off + kvw
        if use_rope:
            cos = cos_ref[...]
            sin = sin_ref[...]
            lane = lax.broadcasted_iota(jnp.int32, cos.shape, 1)
            lo_half = (lane % ROPE_HALF) < (ROPE_HALF // 2)
        q_scale = HEAD_DIM ** -0.5 * LOG2_E
        for j in range(aw // HEAD_DIM):
            qh = _head_rms(src_ref[:, q_off + j * HEAD_DIM:q_off + (j + 1) * HEAD_DIM], qg_ref[0])
            if use_rope:
                qh = _rope(qh, cos, sin, lo_half)
            q_ref[0, :, j * HEAD_DIM:(j + 1) * HEAD_DIM] = (qh * q_scale).astype(BF16)
        for j in range(kvw // HEAD_DIM):
            kh = _head_rms(src_ref[:, k_off + j * HEAD_DIM:k_off + (j + 1) * HEAD_DIM], kg_ref[0])
            if use_rope:
                kh = _rope(kh, cos, sin, lo_half)
            k_ref[0, :, j * HEAD_DIM:(j + 1) * HEAD_DIM] = kh.astype(BF16)
        vt_ref[0] = src_ref[:, v_off:v_off + kvw].T.astype(BF16)

    @pl.when(step % 2 == 0)
    def _():
        project(pr0_ref)
        emit(pr1_ref)

    @pl.when(step % 2 == 1)
    def _():
        project(pr1_ref)
        emit(pr0_ref)


def _inproj(x, mod3, mod_row, layer, norm_g, w_in_bf, qg, kg, cos_t, sin_t, use_rope):
    b, n, d = x.shape
    in_w = w_in_bf.shape[-1]
    fw = d // 4
    pw = d // 4
    aw = d // 2
    kvw = N_KV_HEADS * HEAD_DIM
    tm = _row_tile(n, 512)
    nt = n // tm
    total = b * nt
    n6 = mod3.shape[-1]
    kern = functools.partial(_inproj_kernel, d=d, fw=fw, pw=pw, aw=aw, kvw=kvw, use_rope=use_rope)
    out_shapes = (
        jax.ShapeDtypeStruct((b, n, fw), BF16),
        jax.ShapeDtypeStruct((b, n, pw), F32),
        jax.ShapeDtypeStruct((b, n, aw), BF16),
        jax.ShapeDtypeStruct((b, n, kvw), BF16),
        jax.ShapeDtypeStruct((b, kvw, n), BF16),
    )

    def cur(s):
        return jnp.minimum(s, total - 1)

    def prev(s):
        return jnp.maximum(s - 1, 0)

    def row_spec(w):
        return pl.BlockSpec((1, tm, w), lambda s: (prev(s) // nt, prev(s) % nt, 0))

    return pl.pallas_call(
        kern,
        grid=(total + 1,),
        in_specs=[
            pl.BlockSpec((1, tm, d), lambda s: (cur(s) // nt, cur(s) % nt, 0)),
            pl.BlockSpec((1, 1, n6), lambda s: (mod_row(cur(s) // nt), 0, 0)),
            pl.BlockSpec((1, 1, d), lambda s: (layer, 0, 0)),
            pl.BlockSpec((1, d, in_w), lambda s: (layer, 0, 0)),
            pl.BlockSpec((1, 1, HEAD_DIM), lambda s: (layer, 0, 0)),
            pl.BlockSpec((1, 1, HEAD_DIM), lambda s: (layer, 0, 0)),
            pl.BlockSpec((tm, HEAD_DIM), lambda s: (prev(s) % nt, 0)),
            pl.BlockSpec((tm, HEAD_DIM), lambda s: (prev(s) % nt, 0)),
        ],
        out_specs=(row_spec(fw), row_spec(pw), row_spec(aw), row_spec(kvw),
                   pl.BlockSpec((1, kvw, tm), lambda s: (prev(s) // nt, 0, prev(s) % nt))),
        out_shape=out_shapes,
        scratch_shapes=[pltpu.VMEM((tm, in_w), F32), pltpu.VMEM((tm, in_w), F32)],
        compiler_params=_cparams(("arbitrary",)),
        name="inproj",
    )(x, mod3, norm_g, w_in_bf, qg, kg, cos_t, sin_t)


ATTN_CHUNK = 256
ATTN_SLAB = 64


def _attn_kernel(*refs, n_src, group, n_chunks, chunk):
    q_ref = refs[0]
    k_srcs = refs[1:1 + n_src]
    vt_srcs = refs[1 + n_src:1 + 2 * n_src]
    o_ref, s0_ref, s1_ref = refs[1 + 2 * n_src:4 + 2 * n_src]
    if n_src == 1:
        k_ref, vt_ref = k_srcs[0].at[0], vt_srcs[0].at[0]
    else:
        k_ref, vt_ref = refs[4 + 2 * n_src:]
        row = 0
        for ks, vts in zip(k_srcs, vt_srcs):
            rows = ks.shape[1]
            k_ref[row:row + rows, :] = ks[0]
            vt_ref[:, row:row + rows] = vts[0]
            row += rows
    dn = (((1,), (1,)), ((), ()))

    def scores(c, s_ref):
        r0 = pl.multiple_of(c * chunk, chunk)
        for g in range(group):
            q = q_ref[0, pl.ds(r0, chunk), g * HEAD_DIM:(g + 1) * HEAD_DIM]
            s_ref[g] = lax.dot_general(k_ref[...], q, dn, preferred_element_type=F32)

    def finish(c, s_ref):
        r0 = pl.multiple_of(c * chunk, chunk)
        for g in range(group):
            s = s_ref[g]
            nk = s.shape[0]
            slab = ATTN_SLAB if nk % ATTN_SLAB == 0 else nk
            m = s.reshape(nk // slab, slab, chunk).max(axis=0).max(axis=0, keepdims=True)
            p = jnp.exp2(s - m)
            l = p.reshape(nk // slab, slab, chunk).sum(axis=0).sum(axis=0, keepdims=True)
            ot = jnp.dot(vt_ref[...], p.astype(BF16), preferred_element_type=F32)
            o_ref[0, pl.ds(r0, chunk), g * HEAD_DIM:(g + 1) * HEAD_DIM] = (ot / l).T.astype(BF16)

    scores(0, s0_ref)
    if n_chunks == 1:
        finish(0, s0_ref)
        return

    def body(i, carry):
        c = 2 * i
        scores(c + 1, s1_ref)
        finish(c, s0_ref)
        scores(c + 2, s0_ref)
        finish(c + 1, s1_ref)
        return carry

    lax.fori_loop(0, n_chunks // 2 - 1, body, 0)
    scores(n_chunks - 1, s1_ref)
    finish(n_chunks - 2, s0_ref)
    finish(n_chunks - 1, s1_ref)


def _attention(q, ks, vts):
    b, n, aw = q.shape
    nk = sum(k.shape[1] for k in ks)
    n_src = len(ks)
    group = aw // HEAD_DIM // N_KV_HEADS
    gw = group * HEAD_DIM
    chunk = min(ATTN_CHUNK, n)
    n_chunks = n // chunk
    assert n % chunk == 0 and (n_chunks == 1 or n_chunks % 2 == 0)
    scratch = [pltpu.VMEM((group, nk, chunk), F32), pltpu.VMEM((group, nk, chunk), F32)]
    if n_src > 1:
        scratch += [pltpu.VMEM((nk, HEAD_DIM), BF16), pltpu.VMEM((HEAD_DIM, nk), BF16)]
    return pl.pallas_call(
        functools.partial(_attn_kernel, n_src=n_src, group=group, n_chunks=n_chunks, chunk=chunk),
        grid=(b, N_KV_HEADS),
        in_specs=[pl.BlockSpec((1, n, gw), lambda bi, hi: (bi, 0, hi))]
        + [pl.BlockSpec((1, k.shape[1], HEAD_DIM), lambda bi, hi: (bi, 0, hi)) for k in ks]
        + [pl.BlockSpec((1, HEAD_DIM, v.shape[2]), lambda bi, hi: (bi, hi, 0)) for v in vts],
        out_specs=pl.BlockSpec((1, n, gw), lambda bi, hi: (bi, 0, hi)),
        out_shape=jax.ShapeDtypeStruct((b, n, aw), BF16),
        scratch_shapes=scratch,
        compiler_params=_cparams(("parallel", "arbitrary")),
        name="attention",
    )(q, *ks, *vts)


def _fourier_kernel(f_ref, cc_ref, sc_ref, cn_ref, sn_ref, w_ref, o_ref, xc_ref, xs_ref, *, scale):
    bi = pl.program_id(1)

    @pl.when(pl.program_id(0) == 0)
    def _():
        f = f_ref[0]
        xc_ref[bi] = jnp.dot(f, cc_ref[...], preferred_element_type=F32).astype(BF16)
        xs_ref[bi] = jnp.dot(f, sc_ref[...], preferred_element_type=F32).astype(BF16)

    fr = (jnp.dot(cn_ref[...], xc_ref[bi], preferred_element_type=F32)
          - jnp.dot(sn_ref[...], xs_ref[bi], preferred_element_type=F32)) * scale
    o_ref[0] = jnp.dot(fr.astype(BF16), w_ref[0], preferred_element_type=F32).astype(BF16)


def _dft_kernel(clo_ref, slo_ref, chi_ref, shi_ref, cos_ref, sin_ref):
    c_lo, s_lo = clo_ref[...], slo_ref[...]
    for h in range(chi_ref.shape[1]):
        c_hi, s_hi = chi_ref[:, h:h + 1], shi_ref[:, h:h + 1]
        cos_ref[:, h * LANES:(h + 1) * LANES] = (c_hi * c_lo - s_hi * s_lo).astype(BF16)
        sin_ref[:, h * LANES:(h + 1) * LANES] = (s_hi * c_lo + c_hi * s_lo).astype(BF16)


def _dft_tables(n):
    k = jnp.arange(n, dtype=jnp.int32)

    def cos_sin(t):
        ang = ((k[:, None] * t[None, :]) % n).astype(F32) * (2.0 * jnp.pi / n)
        return jnp.cos(ang), jnp.sin(ang)

    if n <= 2 * LANES or n % LANES:
        return tuple(t.astype(BF16) for t in cos_sin(k))
    c_hi, s_hi = cos_sin(jnp.arange(n // LANES, dtype=jnp.int32) * LANES)
    c_lo, s_lo = cos_sin(jnp.arange(LANES, dtype=jnp.int32))
    tm = _row_tile(n, 256)

    def spec(w):
        return pl.BlockSpec((tm, w), lambda i: (i, 0))

    return pl.pallas_call(
        _dft_kernel,
        grid=(n // tm,),
        in_specs=[spec(LANES), spec(LANES), spec(n // LANES), spec(n // LANES)],
        out_specs=(spec(n), spec(n)),
        out_shape=(jax.ShapeDtypeStruct((n, n), BF16), jax.ShapeDtypeStruct((n, n), BF16)),
        compiler_params=_cparams(("parallel",)),
        name="dft_tables",
    )(c_lo, s_lo, c_hi, s_hi)


def _fourier(f, layer, w_fourier_bf, cn, sn, cc_bd, sc_bd):
    b, n, fw = f.shape
    tm = _row_tile(n, 512)
    scale = float((n * (fw // N_FOURIER_HEADS)) ** -0.5)
    return pl.pallas_call(
        functools.partial(_fourier_kernel, scale=scale),
        grid=(n // tm, b),
        in_specs=[
            pl.BlockSpec((1, n, fw), lambda ti, bi: (jnp.where(ti == 0, bi, b - 1), 0, 0)),
            pl.BlockSpec((fw, fw), lambda ti, bi: (0, 0)),
            pl.BlockSpec((fw, fw), lambda ti, bi: (0, 0)),
            pl.BlockSpec((tm, n), lambda ti, bi: (ti, 0)),
            pl.BlockSpec((tm, n), lambda ti, bi: (ti, 0)),
            pl.BlockSpec((1, fw, fw), lambda ti, bi: (layer, 0, 0)),
        ],
        out_specs=pl.BlockSpec((1, tm, fw), lambda ti, bi: (bi, ti, 0)),
        out_shape=jax.ShapeDtypeStruct((b, n, fw), BF16),
        scratch_shapes=[pltpu.VMEM((b, n, fw), BF16), pltpu.VMEM((b, n, fw), BF16)],
        compiler_params=_cparams(("arbitrary", "arbitrary")),
        name="fourier",
    )(f, cc_bd, sc_bd, cn, sn, w_fourier_bf)


POOL_HALO = 16
POOL_EDGE = 8


def _pool_kernel(p_ref, cnt_ref, w_ref, s_ref, o_ref, x_ref, s2_ref, s4_ref, s8_ref, *, n, pw):
    gdim = pw // len(POOL_WINDOWS)
    lo, hi = POOL_EDGE, n + 2 * POOL_HALO - POOL_EDGE
    for ref in (x_ref, s2_ref, s4_ref, s8_ref):
        ref[0:POOL_HALO, :] = jnp.zeros((POOL_HALO, pw), F32)
        ref[n + POOL_HALO:n + 2 * POOL_HALO, :] = jnp.zeros((POOL_HALO, pw), F32)
    x_ref[POOL_HALO:POOL_HALO + n, :] = p_ref[0]

    s2_ref[lo:hi, :] = x_ref[lo - 1:hi - 1, :] + x_ref[lo:hi, :]
    s4_ref[lo:hi, :] = s2_ref[lo - 1:hi - 1, :] + s2_ref[lo + 1:hi + 1, :]
    s8_ref[lo:hi, :] = s4_ref[lo - 2:hi - 2, :] + s4_ref[lo + 2:hi + 2, :]
    a, b = POOL_HALO, POOL_HALO + n
    s16 = s8_ref[a - 4:b - 4, :] + s8_ref[a + 4:b + 4, :]
    lane = lax.broadcasted_iota(jnp.int32, (1, pw), 1)
    acc = jnp.where(lane < gdim, s2_ref[a:b, :],
                    jnp.where(lane < 2 * gdim, s4_ref[a:b, :], jnp.where(lane < 3 * gdim, s8_ref[a:b, :], s16)))
    dlt = acc / cnt_ref[...] - x_ref[a:b, :]
    y = jnp.dot(dlt.astype(BF16), w_ref[0], preferred_element_type=F32) * s_ref[0]
    o_ref[0] = y.astype(BF16)


def _pool_counts(n, pw):
    gdim = pw // len(POOL_WINDOWS)
    half = jnp.repeat(jnp.asarray([w // 2 for w in POOL_WINDOWS], jnp.int32), gdim)[None, :]
    t = jnp.arange(n, dtype=jnp.int32)[:, None]
    return (jnp.minimum(t + half, n) - jnp.maximum(t - half, 0)).astype(F32)


def _pool(p, cnt, layer, w_pool_bd, pool_scale):
    b, n, pw = p.shape
    assert POOL_WINDOWS == (2, 4, 8, 16)
    buf = pltpu.VMEM((n + 2 * POOL_HALO, pw), F32)
    return pl.pallas_call(
        functools.partial(_pool_kernel, n=n, pw=pw),
        grid=(b,),
        in_specs=[
            pl.BlockSpec((1, n, pw), lambda bi: (bi, 0, 0)),
            pl.BlockSpec((n, pw), lambda bi: (0, 0)),
            pl.BlockSpec((1, pw, pw), lambda bi: (layer, 0, 0)),
            pl.BlockSpec((1, 1, pw), lambda bi: (layer, 0, 0)),
        ],
        out_specs=pl.BlockSpec((1, n, pw), lambda bi: (bi, 0, 0)),
        out_shape=jax.ShapeDtypeStruct((b, n, pw), BF16),
        scratch_shapes=[buf, buf, buf, buf],
        compiler_params=_cparams(("parallel",)),
        name="pool",
    )(p, cnt, w_pool_bd, pool_scale)


def _outproj_kernel(fo_ref, po_ref, ao_ref, x_ref, modc_ref, modp_ref, g_ref, w_ref, wr_ref,
                    xn_ref, h_ref, aff_ref, xn0_ref, xn1_ref, *, d, fw, pw):
    step = pl.program_id(0)

    @pl.when(step == 0)
    def _():
        xn1_ref[...] = jnp.zeros_like(xn1_ref)

    def project(dst_ref):
        w = w_ref[0]
        ox = jnp.dot(fo_ref[0], w[0:fw], preferred_element_type=F32)
        ox = ox + jnp.dot(po_ref[0], w[fw:fw + pw], preferred_element_type=F32)
        ox = ox + jnp.dot(ao_ref[0], w[fw + pw:], preferred_element_type=F32)
        dst_ref[...] = x_ref[0] + modc_ref[0][:, 2 * d:3 * d] * ox

    def emit(src_ref):
        mod = modp_ref[0]
        xn = src_ref[...]
        xn_ref[0] = xn
        h = _modulate(xn, g_ref[0], mod[:, 3 * d:4 * d], mod[:, 4 * d:5 * d])
        logits = jnp.dot(h.astype(BF16), wr_ref[0], preferred_element_type=F32)
        lane = lax.broadcasted_iota(jnp.int32, logits.shape, 1)
        logits = jnp.where(lane < N_EXPERTS, logits, -jnp.inf)
        e = jnp.exp(logits - logits.max(axis=-1, keepdims=True))
        aff = e / e.sum(axis=-1, keepdims=True)
        aff_ref[0] = aff.T[0:N_EXPERTS, :]
        a_hi = aff.astype(BF16).astype(F32)
        r1 = aff - a_hi
        a_mid = r1.astype(BF16).astype(F32)
        a_lo = (r1 - a_mid).astype(BF16).astype(F32)
        tail = a_hi + pltpu.roll(a_mid, N_EXPERTS, 1) + pltpu.roll(a_lo, 2 * N_EXPERTS, 1)
        h_ref[0, :, 0:d] = h.astype(BF16)
        h_ref[0, :, d:d + LANES] = tail.astype(BF16)

    @pl.when(step % 2 == 0)
    def _():
        project(xn0_ref)
        emit(xn1_ref)

    @pl.when(step % 2 == 1)
    def _():
        project(xn1_ref)
        emit(xn0_ref)


def _outproj(fo, po, ao, x, mod3, mod_row, layer, norm_g, w_out_bf, w_router_pad):
    b, n, d = x.shape
    fw = fo.shape[-1]
    pw = po.shape[-1]
    aw = ao.shape[-1]
    n6 = mod3.shape[-1]
    tm = _row_tile(n, 512)
    nt = n // tm
    total = b * nt

    def cur(s):
        return jnp.minimum(s, total - 1)

    def prev(s):
        return jnp.maximum(s - 1, 0)

    def in_spec(w):
        return pl.BlockSpec((1, tm, w), lambda s: (cur(s) // nt, cur(s) % nt, 0))

    return pl.pallas_call(
        functools.partial(_outproj_kernel, d=d, fw=fw, pw=pw),
        grid=(total + 1,),
        in_specs=[
            in_spec(fw), in_spec(pw), in_spec(aw), in_spec(d),
            pl.BlockSpec((1, 1, n6), lambda s: (mod_row(cur(s) // nt), 0, 0)),
            pl.BlockSpec((1, 1, n6), lambda s: (mod_row(prev(s) // nt), 0, 0)),
            pl.BlockSpec((1, 1, d), lambda s: (layer, 0, 0)),
            pl.BlockSpec((1, d, d), lambda s: (layer, 0, 0)),
            pl.BlockSpec((1, d, LANES), lambda s: (layer, 0, 0)),
        ],
        out_specs=(pl.BlockSpec((1, tm, d), lambda s: (prev(s) // nt, prev(s) % nt, 0)),
                   pl.BlockSpec((1, tm, d + LANES), lambda s: (prev(s) // nt, prev(s) % nt, 0)),
                   pl.BlockSpec((1, N_EXPERTS, tm), lambda s: (prev(s) // nt, 0, prev(s) % nt))),
        out_shape=(jax.ShapeDtypeStruct((b, n, d), F32),
                   jax.ShapeDtypeStruct((b, n, d + LANES), BF16),
                   jax.ShapeDtypeStruct((b, N_EXPERTS, n), F32)),
        scratch_shapes=[pltpu.VMEM((tm, d), F32), pltpu.VMEM((tm, d), F32)],
        compiler_params=_cparams(("arbitrary",)),
        name="outproj",
    )(fo, po, ao, x, mod3, mod3, norm_g, w_out_bf, w_router_pad)


ROUTE_BLOCK = 256
COMBINE_WINDOW = 64
NOT_IN_WINDOW = -(1 << 20)
FLAG_LANE0 = LANES // 2


def _combine_geometry(cap):
    win = min(COMBINE_WINDOW, cap)
    return win, N_EXPERTS


def _route_kernel(aff_ref, key_ref, relw_ref, soff_ref, tri_ref, *, n, cap, blk, win, grp):
    nb = n // blk
    e_n = N_EXPERTS

    @pl.when(pl.program_id(0) == 0)
    def _():
        r = lax.broadcasted_iota(jnp.int32, (blk, blk), 0)
        c = lax.broadcasted_iota(jnp.int32, (blk, blk), 1)
        tri_ref[...] = jnp.where(r < c, 1.0, 0.0).astype(BF16)

    aff = aff_ref[0]

    def count_ge(bits):
        return jnp.sum(jnp.where(aff >= pltpu.bitcast(bits, F32), 1.0, 0.0), axis=-1, keepdims=True)

    def search(i, thr_bits):
        lo = lax.shift_left(jnp.int32(1), 29 - 2 * i)
        hi = lo + lo
        take_hi = count_ge(thr_bits | hi) >= cap
        take_both = count_ge(thr_bits | hi | lo) >= cap
        take_lo = count_ge(thr_bits | lo) >= cap
        with_hi = jnp.where(take_both, thr_bits | hi | lo, thr_bits | hi)
        without_hi = jnp.where(take_lo, thr_bits | lo, thr_bits)
        return jnp.where(take_hi, with_hi, without_hi)

    thr_bits = lax.fori_loop(0, 15, search, jnp.zeros((e_n, 1), jnp.int32))
    thr_bits = jnp.where(count_ge(thr_bits | 1) >= cap, thr_bits | 1, thr_bits)
    thr = pltpu.bitcast(thr_bits, F32)
    gt = jnp.where(aff > thr, 1.0, 0.0)
    eq = jnp.where(aff == thr, 1.0, 0.0)
    need = cap - jnp.sum(gt, axis=-1, keepdims=True)
    tri = tri_ref[...]

    sel = []
    off = jnp.zeros((e_n, 1), F32)
    for j in range(nb):
        eqj = eq[:, j * blk:(j + 1) * blk]
        rank = jnp.dot(eqj.astype(BF16), tri, preferred_element_type=F32) + off
        off = off + jnp.sum(eqj, axis=-1, keepdims=True)
        sel.append(jnp.maximum(gt[:, j * blk:(j + 1) * blk], jnp.where(rank < need, eqj, 0.0)))

    lane = lax.broadcasted_iota(jnp.int32, (e_n, LANES), 1)
    q_off = ((lax.broadcasted_iota(jnp.int32, (e_n, 1), 0) % grp) * win).astype(F32)
    soff = jnp.zeros((e_n, LANES), F32)
    off = jnp.zeros((e_n, 1), F32)
    for j in range(nb):
        selj = sel[j]
        pos = jnp.dot(selj.astype(BF16), tri, preferred_element_type=F32) + off
        soff = jnp.where(lane == j, off, soff)
        start = jnp.minimum(jnp.floor(off * (1.0 / 16.0)) * 16.0, float(cap - win))
        rel = pos - start
        chosen = selj > 0.0
        key_ref[0, :, j * blk:(j + 1) * blk] = jnp.where(chosen, pos, -1.0).astype(jnp.int32)
        relw_ref[0, :, j * blk:(j + 1) * blk] = jnp.where(
            chosen & (rel < win), rel + q_off, float(NOT_IN_WINDOW)).astype(jnp.int32)
        count = jnp.sum(selj, axis=-1, keepdims=True)
        spills = jnp.max(jnp.where(off - start + count > win, 1.0, 0.0), axis=0, keepdims=True)
        soff = jnp.where(lane == FLAG_LANE0 + j, spills, soff)
        off = off + count
    soff_ref[0] = jnp.where(lane == nb, off, soff).astype(jnp.int32)


def _route(aff_t, cap):
    b, e, n = aff_t.shape
    blk = min(ROUTE_BLOCK, n)
    assert n % blk == 0 and n // blk < FLAG_LANE0
    win, grp = _combine_geometry(cap)

    def spec(w):
        return pl.BlockSpec((1, e, w), lambda bi: (bi, 0, 0))

    return pl.pallas_call(
        functools.partial(_route_kernel, n=n, cap=cap, blk=blk, win=win, grp=grp),
        grid=(b,),
        in_specs=[spec(n)],
        out_specs=(spec(n), spec(n), spec(LANES)),
        out_shape=(jax.ShapeDtypeStruct((b, e, n), jnp.int32),
                   jax.ShapeDtypeStruct((b, e, n), jnp.int32),
                   jax.ShapeDtypeStruct((b, e, LANES), jnp.int32)),
        scratch_shapes=[pltpu.VMEM((blk, blk), BF16)],
        compiler_params=_cparams(("arbitrary",)),
        name="moe_route",
    )(aff_t)


def _window_start(soff_ref, e, j, stride, cap, win):
    s = soff_ref[0, 0, e * stride + j]
    return s, jnp.minimum((s >> 4) << 4, cap - win)


def _gather_kernel(soff_ref, relw_ref, key_ref, h_ref, o_ref, *, cap, n_exp, nb, win):
    j = pl.program_id(1)
    stride = nb + 1
    blk = relw_ref.shape[-1]

    @pl.when(j == 0)
    def _():
        o_ref[...] = jnp.zeros_like(o_ref)

    row = lax.broadcasted_iota(jnp.int32, (n_exp * win, blk), 0)
    slabs = [jnp.broadcast_to(relw_ref[0, e:e + 1, :], (win, blk)) for e in range(n_exp)]
    onehot = jnp.where(jnp.concatenate(slabs, axis=0) == row, 1.0, 0.0).astype(BF16)
    rows = jnp.dot(onehot, h_ref[0], preferred_element_type=F32).astype(BF16)
    for e in range(n_exp):
        _, start = _window_start(soff_ref, e, j, stride, cap, win)
        dst = o_ref.at[0, e, pl.ds(pl.multiple_of(start, 16), win), :]
        dst[...] = dst[...] + rows[e * win:(e + 1) * win]

    row_w = lax.broadcasted_iota(jnp.int32, (win, blk), 0)

    def further_windows(e, carry):
        s, start0 = _window_start(soff_ref, e, j, stride, cap, win)
        count = soff_ref[0, 0, e * stride + j + 1] - s
        n_win = (s - start0 + count + win - 1) // win

        def body(k, c):
            lo = start0 + k * win
            st = pl.multiple_of(jnp.minimum(lo, cap - win), 16)
            krow = key_ref[0, pl.ds(e, 1), :]
            hot = jnp.where((krow - st == row_w) & (krow >= lo), 1.0, 0.0).astype(BF16)
            dst = o_ref.at[0, e, pl.ds(st, win), :]
            dst[...] = dst[...] + jnp.dot(hot, h_ref[0], preferred_element_type=F32).astype(BF16)
            return c

        lax.fori_loop(1, n_win, body, 0)
        return carry

    @pl.when(soff_ref[0, 0, n_exp * stride + j] > 0)
    def _():
        lax.fori_loop(0, n_exp, further_windows, 0)


def _route_scalars(soff, nb):
    b, e, _ = soff.shape
    return jnp.concatenate([soff[:, :, :nb + 1].reshape(b, 1, e * (nb + 1)),
                            soff[:, :1, FLAG_LANE0:FLAG_LANE0 + nb]], axis=-1)


def _gather(h_aug, soff, relw, key, cap):
    b, n, dw = h_aug.shape
    e = relw.shape[1]
    blk = min(ROUTE_BLOCK, n)
    nb = n // blk
    win, _ = _combine_geometry(cap)
    return pl.pallas_call(
        functools.partial(_gather_kernel, cap=cap, n_exp=e, nb=nb, win=win),
        grid=(b, nb),
        in_specs=[
            pl.BlockSpec((1, 1, e * (nb + 1) + nb), lambda bi, ji: (bi, 0, 0), memory_space=pltpu.SMEM),
            pl.BlockSpec((1, e, blk), lambda bi, ji: (bi, 0, ji)),
            pl.BlockSpec((1, e, blk), lambda bi, ji: (bi, 0, ji)),
            pl.BlockSpec((1, blk, dw), lambda bi, ji: (bi, ji, 0)),
        ],
        out_specs=pl.BlockSpec((1, e, cap, dw), lambda bi, ji: (bi, 0, 0, 0)),
        out_shape=jax.ShapeDtypeStruct((b, e, cap, dw), BF16),
        compiler_params=_cparams(("parallel", "arbitrary")),
        name="moe_gather",
    )(_route_scalars(soff, nb), relw, key, h_aug)


def _ffn_kernel(*refs, bt, n_sets, d):
    ins = refs[:n_sets]
    wg_ref, wu_ref, wd_ref = refs[n_sets:n_sets + 3]
    outs = refs[n_sets + 3:2 * n_sets + 3]
    wg_bf, wu_bf, wd_bf = refs[2 * n_sets + 3:]
    ei = pl.program_id(0)

    @pl.when(pl.program_id(1) == 0)
    def _():
        wg_bf[...] = wg_ref[0, 0].astype(BF16)
        wu_bf[...] = wu_ref[0, 0].astype(BF16)
        wd_bf[...] = wd_ref[0, 0].astype(BF16)

    def expert(rows):
        xs = rows[:, 0:d]
        tail = rows[:, d:d + LANES].astype(F32)
        lane = lax.broadcasted_iota(jnp.int32, tail.shape, 1)
        mine = (lane == ei) | (lane == ei + N_EXPERTS) | (lane == ei + 2 * N_EXPERTS)
        gate = jnp.sum(jnp.where(mine, tail, 0.0), axis=-1, keepdims=True)
        a = jnp.dot(xs, wg_bf[...], preferred_element_type=F32)
        u = jnp.dot(xs, wu_bf[...], preferred_element_type=F32)
        hid = (a * jax.nn.sigmoid(a) * u).astype(BF16)
        return (jnp.dot(hid, wd_bf[...], preferred_element_type=F32) * gate).astype(BF16)

    for xs_ref, y_ref in zip(ins, outs):
        cap, dw = xs_ref.shape[2], xs_ref.shape[3]
        if cap >= 256:
            for bi in range(bt):
                y_ref[bi, 0] = expert(xs_ref[bi, 0])
        else:
            y_ref[:, 0] = expert(xs_ref[:, 0].reshape(bt * cap, dw)).reshape(bt, cap, d)


def _ffn(sets, layer, w_gate, w_up, w_down):
    b, e, _, dw = sets[0].shape
    d = w_gate.shape[-2]
    ff = w_gate.shape[-1]
    bt = min(b, 8)
    assert b % bt == 0 and dw == d + LANES
    in_specs = [pl.BlockSpec((bt, 1, xs.shape[2], dw), lambda ei, ji: (ji, ei, 0, 0)) for xs in sets]
    out_specs = [pl.BlockSpec((bt, 1, xs.shape[2], d), lambda ei, ji: (ji, ei, 0, 0)) for xs in sets]
    out_shape = [jax.ShapeDtypeStruct(xs.shape[:3] + (d,), BF16) for xs in sets]
    in_specs += [
        pl.BlockSpec((1, 1, d, ff), lambda ei, ji: (layer, ei, 0, 0)),
        pl.BlockSpec((1, 1, d, ff), lambda ei, ji: (layer, ei, 0, 0)),
        pl.BlockSpec((1, 1, ff, d), lambda ei, ji: (layer, ei, 0, 0)),
    ]
    return pl.pallas_call(
        functools.partial(_ffn_kernel, bt=bt, n_sets=len(sets), d=d),
        grid=(e, b // bt),
        in_specs=in_specs,
        out_specs=tuple(out_specs),
        out_shape=tuple(out_shape),
        scratch_shapes=[pltpu.VMEM((d, ff), BF16), pltpu.VMEM((d, ff), BF16), pltpu.VMEM((ff, d), BF16)],
        compiler_params=_cparams(("parallel", "arbitrary")),
        name="moe_ffn",
    )(*sets, w_gate, w_up, w_down)


def _combine_kernel(soff_ref, relw_ref, key_ref, y_ref, xn_ref, mod_ref, o_ref, rhs_ref,
                    *, cap, d, n_exp, nb, win, grp):
    j = pl.program_id(1)
    stride = nb + 1
    blk = relw_ref.shape[-1]
    dn_t = (((0,), (0,)), ((), ()))

    def window_start(e):
        return _window_start(soff_ref, e, j, stride, cap, win)

    row = lax.broadcasted_iota(jnp.int32, (grp * win, blk), 0)
    acc = jnp.zeros((blk, d), F32)
    for g0 in range(0, n_exp, grp):
        slabs = []
        for q in range(grp):
            e = g0 + q
            _, start = window_start(e)
            rhs_ref[q * win:(q + 1) * win, :] = y_ref[0, e, pl.ds(pl.multiple_of(start, 16), win), :]
            slabs.append(jnp.broadcast_to(relw_ref[0, e:e + 1, :], (win, blk)))
        onehot = jnp.where(jnp.concatenate(slabs, axis=0) == row, 1.0, 0.0).astype(BF16)
        acc = acc + lax.dot_general(onehot, rhs_ref[...], dn_t, preferred_element_type=F32)
    o_ref[0] = acc

    row_w = lax.broadcasted_iota(jnp.int32, (win, blk), 0)

    def further_windows(e, carry):
        s, start0 = window_start(e)
        count = soff_ref[0, 0, e * stride + j + 1] - s
        n_win = (s - start0 + count + win - 1) // win

        def body(k, c):
            lo = start0 + k * win
            st = pl.multiple_of(jnp.minimum(lo, cap - win), 16)
            krow = key_ref[0, pl.ds(e, 1), :]
            onehot = jnp.where((krow - st == row_w) & (krow >= lo), 1.0, 0.0).astype(BF16)
            o_ref[0] = o_ref[0] + lax.dot_general(onehot, y_ref[0, e, pl.ds(st, win), :], dn_t,
                                                  preferred_element_type=F32)
            return c

        lax.fori_loop(1, n_win, body, 0)
        return carry

    @pl.when(soff_ref[0, 0, n_exp * stride + j] > 0)
    def _():
        lax.fori_loop(0, n_exp, further_windows, 0)

    o_ref[0] = xn_ref[0] + mod_ref[0][:, 5 * d:6 * d] * o_ref[0]


def _combine(y, soff, relw, key, xn, mod3, mod_row):
    b, e, cap, d = y.shape
    n = xn.shape[1]
    n6 = mod3.shape[-1]
    blk = min(ROUTE_BLOCK, n)
    nb = n // blk
    win, grp = _combine_geometry(cap)
    return pl.pallas_call(
        functools.partial(_combine_kernel, cap=cap, d=d, n_exp=e, nb=nb, win=win, grp=grp),
        grid=(b, nb),
        in_specs=[
            pl.BlockSpec((1, 1, e * (nb + 1) + nb), lambda bi, ji: (bi, 0, 0), memory_space=pltpu.SMEM),
            pl.BlockSpec((1, e, blk), lambda bi, ji: (bi, 0, ji)),
            pl.BlockSpec((1, e, blk), lambda bi, ji: (bi, 0, ji)),
            pl.BlockSpec((1, e, cap, d), lambda bi, ji: (bi, 0, 0, 0)),
            pl.BlockSpec((1, blk, d), lambda bi, ji: (bi, ji, 0)),
            pl.BlockSpec((1, 1, n6), lambda bi, ji: (mod_row(bi), 0, 0)),
        ],
        out_specs=pl.BlockSpec((1, blk, d), lambda bi, ji: (bi, ji, 0)),
        out_shape=jax.ShapeDtypeStruct((b, n, d), F32),
        scratch_shapes=[pltpu.VMEM((grp * win, d), BF16)],
        compiler_params=_cparams(("parallel", "arbitrary")),
        name="moe_combine",
    )(_route_scalars(soff, nb), relw, key, y, xn, mod3)


def _moe_dispatch(h_aug, aff_t):
    n = h_aug.shape[1]
    cap = EC_CAPACITY_FACTOR * n // N_EXPERTS
    key, relw, soff = _route(aff_t, cap)
    return _gather(h_aug, soff, relw, key, cap), (soff, relw, key)


def _rope_tables(n):
    t = jnp.arange(n, dtype=jnp.int32)
    row = (t // GRID_W).astype(F32)
    col = (t % GRID_W).astype(F32)
    inv = ROPE_THETA ** (-jnp.arange(0, ROPE_HALF, 2, dtype=F32) / ROPE_HALF)
    ar = row[:, None] * inv[None, :]
    ac = col[:, None] * inv[None, :]
    ang = jnp.concatenate([ar, ar, ac, ac], axis=-1)
    lane = jnp.arange(HEAD_DIM)
    sign = jnp.where((lane % ROPE_HALF) < ROPE_HALF // 2, -1.0, 1.0).astype(F32)
    return jnp.cos(ang), jnp.sin(ang) * sign[None, :]


def _block_diag(blocks):
    g, a, c = blocks.shape
    eye = jnp.eye(g, dtype=blocks.dtype)
    return (eye[:, None, :, None] * blocks[:, :, None, :]).reshape(g * a, g * c)


def kernel(x, c, ctx, c_ctx, ada_w, ada_b, norm1_g, norm2_g, w_in, w_fourier, w_pool, pool_scale,
           q_norm_g, k_norm_g, w_out, w_router, w_gate, w_up, w_down):
    b, n, d = x.shape
    lc = ctx.shape[1]
    depth = ada_w.shape[0]
    fw = d // 4

    rows = -(-(b + 1) // SUBLANES) * SUBLANES
    cond = jnp.zeros((rows, d), F32).at[:b].set(c).at[b].set(c_ctx)
    mod = _ada_mod(cond, ada_w, ada_b)

    def lat_row(bi):
        return bi

    def ctx_row(bi):
        return b

    norm1_g = norm1_g.reshape(depth, 1, d)
    norm2_g = norm2_g.reshape(depth, 1, d)
    q_norm_g = q_norm_g.reshape(depth, 1, HEAD_DIM)
    k_norm_g = k_norm_g.reshape(depth, 1, HEAD_DIM)
    pool_scale = pool_scale.reshape(depth, 1, d // 4)
    w_in_bf = w_in.astype(BF16)
    w_out_bf = w_out.astype(BF16)
    w_fourier_bf = w_fourier.astype(BF16)
    w_router_pad = jnp.zeros((depth, d, LANES), BF16).at[:, :, :N_EXPERTS].set(w_router.astype(BF16))
    w_pool_bd = jax.vmap(_block_diag)(w_pool).astype(BF16)

    cnt_t = _pool_counts(n, d // 4)
    cnt_c = _pool_counts(lc, d // 4)
    cos_t, sin_t = _rope_tables(n)
    cos_c, sin_c = _rope_tables(lc)
    cn, sn = _dft_tables(n)
    cnc, snc = _dft_tables(lc)
    hd = fw // N_FOURIER_HEADS
    cch, sch = _dft_tables(hd)
    cc_bd = _block_diag(jnp.broadcast_to(cch, (N_FOURIER_HEADS, hd, hd)))
    sc_bd = _block_diag(jnp.broadcast_to(sch, (N_FOURIER_HEADS, hd, hd)))

    for i in range(depth):
        last = i == depth - 1
        mod3 = mod[i].reshape(rows, 1, 6 * d)

        fx, px, qx, kx, vtx = _inproj(x, mod3, lat_row, i, norm1_g, w_in_bf, q_norm_g, k_norm_g,
                                      cos_t, sin_t, True)
        fc, pc, qc, kc, vtc = _inproj(ctx, mod3, ctx_row, i, norm1_g, w_in_bf, q_norm_g, k_norm_g,
                                      cos_c, sin_c, False)

        ax = _attention(qx, [kx, kc], [vtx, vtc])
        fox = _fourier(fx, i, w_fourier_bf, cn, sn, cc_bd, sc_bd)
        pox = _pool(px, cnt_t, i, w_pool_bd, pool_scale)
        xn, hx, affx = _outproj(fox, pox, ax, x, mod3, lat_row, i, norm2_g, w_out_bf, w_router_pad)
        xs, route_x = _moe_dispatch(hx, affx)

        if last:
            (yx,) = _ffn([xs], i, w_gate, w_up, w_down)
        else:
            ac = _attention(qc, [kc], [vtc])
            foc = _fourier(fc, i, w_fourier_bf, cnc, snc, cc_bd, sc_bd)
            poc = _pool(pc, cnt_c, i, w_pool_bd, pool_scale)
            cn_, hc, affc = _outproj(foc, poc, ac, ctx, mod3, ctx_row, i, norm2_g, w_out_bf, w_router_pad)
            xsc, route_c = _moe_dispatch(hc, affc)
            yx, yc = _ffn([xs, xsc], i, w_gate, w_up, w_down)
            ctx = _combine(yc, *route_c, cn_, mod3, ctx_row)
        x = _combine(yx, *route_x, xn, mod3, lat_row)
    return x
```

```python
import functools

import jax
import jax.numpy as jnp
from jax import lax
from jax.experimental import pallas as pl
from jax.experimental.pallas import tpu as pltpu

GRID_W = 64
EPS = 1e-6
N_FOURIER_HEADS = 4
POOL_WINDOWS = (2, 4, 8, 16)
HEAD_DIM = 128
N_KV_HEADS = 2
ROPE_HALF = HEAD_DIM // 2
ROPE_THETA = 10000.0
N_EXPERTS = 16
EC_CAPACITY_FACTOR = 2

LANES = 128
SUBLANES = 8
VMEM_LIMIT_BYTES = 56 * 1024 * 1024

LOG2_E = 1.4426950408889634

F32 = jnp.float32
BF16 = jnp.bfloat16


def _cparams(semantics):
    return pltpu.CompilerParams(dimension_semantics=semantics, vmem_limit_bytes=VMEM_LIMIT_BYTES)


def _row_tile(n, target):
    t = min(n, target)
    assert n % t == 0
    return t


def _ada_kernel(c_ref, w_ref, b_ref, o_ref):
    c = c_ref[...]
    s = c * jax.nn.sigmoid(c)
    r = s.shape[0]
    s_hi = s.astype(BF16)
    s_r1 = s - s_hi.astype(F32)
    s_mid = s_r1.astype(BF16)
    s_lo = (s_r1 - s_mid.astype(F32)).astype(BF16)
    lhs = jnp.concatenate([s_hi, s_mid, s_lo], axis=0)
    w = w_ref[0]
    w_hi = w.astype(BF16)
    w_lo = (w - w_hi.astype(F32)).astype(BF16)
    p_hi = jnp.dot(lhs, w_hi, preferred_element_type=F32)
    p_lo = jnp.dot(lhs, w_lo, preferred_element_type=F32)
    o_ref[0] = (p_lo[r:2 * r] + p_lo[0:r]) + (p_hi[2 * r:3 * r] + p_hi[r:2 * r]) + p_hi[0:r] + b_ref[0]


def _ada_mod(cond, ada_w, ada_b):
    depth, d, n6 = ada_w.shape
    r = cond.shape[0]
    tn = _row_tile(n6, 1536)
    return pl.pallas_call(
        _ada_kernel,
        grid=(depth, n6 // tn),
        in_specs=[
            pl.BlockSpec((r, d), lambda i, j: (0, 0)),
            pl.BlockSpec((1, d, tn), lambda i, j: (i, 0, j)),
            pl.BlockSpec((1, 1, tn), lambda i, j: (i, 0, j)),
        ],
        out_specs=pl.BlockSpec((1, r, tn), lambda i, j: (i, 0, j)),
        out_shape=jax.ShapeDtypeStruct((depth, r, n6), F32),
        compiler_params=_cparams(("arbitrary", "arbitrary")),
        name="ada_mod",
    )(cond, ada_w, ada_b.reshape(depth, 1, n6))


def _modulate(xf, g, shift, scale):
    ms = jnp.mean(xf * xf, axis=-1, keepdims=True)
    y = xf * lax.rsqrt(ms + EPS) * g
    return y * (1.0 + scale) + shift


def _head_rms(xh, g):
    ms = jnp.mean(xh * xh, axis=-1, keepdims=True)
    return xh * lax.rsqrt(ms + EPS) * g


def _rope(xh, cos, sin_signed, lo_half):
    partner = jnp.where(lo_half, pltpu.roll(xh, HEAD_DIM - ROPE_HALF // 2, 1), pltpu.roll(xh, ROPE_HALF // 2, 1))
    return xh * cos + partner * sin_signed


def _inproj_kernel(x_ref, mod_ref, g_ref, w_ref, qg_ref, kg_ref, cos_ref, sin_ref,
                   f_ref, p_ref, q_ref, k_ref, vt_ref, pr0_ref, pr1_ref, *, d, fw, pw, aw, kvw, use_rope):
    step = pl.program_id(0)

    @pl.when(step == 0)
    def _():
        pr1_ref[...] = jnp.zeros_like(pr1_ref)

    def project(dst_ref):
        mod = mod_ref[0]
        h = _modulate(x_ref[0], g_ref[0], mod[:, 0:d], mod[:, d:2 * d]).astype(BF16)
        dst_ref[...] = jnp.dot(h, w_ref[0], preferred_element_type=F32)

    def emit(src_ref):
        f_ref[0] = src_ref[:, 0:fw].astype(BF16)
        p_ref[0] = src_ref[:, fw:fw + pw]
        q_off = fw + pw
        k_off = q_off + aw
        v_off = k_off + kvw
        if use_rope:
            cos = cos_ref[...]
            sin = sin_ref[...]
            lane = lax.broadcasted_iota(jnp.int32, cos.shape, 1)
            lo_half = (lane % ROPE_HALF) < (ROPE_HALF // 2)
        q_scale = HEAD_DIM ** -0.5 * LOG2_E
        for j in range(aw // HEAD_DIM):
            qh = _head_rms(src_ref[:, q_off + j * HEAD_DIM:q_off + (j + 1) * HEAD_DIM], qg_ref[0])
            if use_rope:
                qh = _rope(qh, cos, sin, lo_half)
            q_ref[0, :, j * HEAD_DIM:(j + 1) * HEAD_DIM] = (qh * q_scale).astype(BF16)
        for j in range(kvw // HEAD_DIM):
            kh = _head_rms(src_ref[:, k_off + j * HEAD_DIM:k_off + (j + 1) * HEAD_DIM], kg_ref[0])
            if use_rope:
                kh = _rope(kh, cos, sin, lo_half)
            k_ref[0, :, j * HEAD_DIM:(j + 1) * HEAD_DIM] = kh.astype(BF16)
        vt_ref[0] = src_ref[:, v_off:v_off + kvw].T.astype(BF16)

    @pl.when(step % 2 == 0)
    def _():
        project(pr0_ref)
        emit(pr1_ref)

    @pl.when(step % 2 == 1)
    def _():
        project(pr1_ref)
        emit(pr0_ref)


def _inproj(x, mod3, mod_row, layer, norm_g, w_in_bf, qg, kg, cos_t, sin_t, use_rope):
    b, n, d = x.shape
    in_w = w_in_bf.shape[-1]
    fw = d // 4
    pw = d // 4
    aw = d // 2
    kvw = N_KV_HEADS * HEAD_DIM
    tm = _row_tile(n, 1024)
    nt = n // tm
    total = b * nt
    n6 = mod3.shape[-1]
    kern = functools.partial(_inproj_kernel, d=d, fw=fw, pw=pw, aw=aw, kvw=kvw, use_rope=use_rope)
    out_shapes = (
        jax.ShapeDtypeStruct((b, n, fw), BF16),
        jax.ShapeDtypeStruct((b, n, pw), F32),
        jax.ShapeDtypeStruct((b, n, aw), BF16),
        jax.ShapeDtypeStruct((b, n, kvw), BF16),
        jax.ShapeDtypeStruct((b, kvw, n), BF16),
    )

    def cur(s):
        return jnp.minimum(s, total - 1)

    def prev(s):
        return jnp.maximum(s - 1, 0)

    def row_spec(w):
        return pl.BlockSpec((1, tm, w), lambda s: (prev(s) // nt, prev(s) % nt, 0))

    return pl.pallas_call(
        kern,
        grid=(total + 1,),
        in_specs=[
            pl.BlockSpec((1, tm, d), lambda s: (cur(s) // nt, cur(s) % nt, 0)),
            pl.BlockSpec((1, 1, n6), lambda s: (mod_row(cur(s) // nt), 0, 0)),
            pl.BlockSpec((1, 1, d), lambda s: (layer, 0, 0)),
            pl.BlockSpec((1, d, in_w), lambda s: (layer, 0, 0)),
            pl.BlockSpec((1, 1, HEAD_DIM), lambda s: (layer, 0, 0)),
            pl.BlockSpec((1, 1, HEAD_DIM), lambda s: (layer, 0, 0)),
            pl.BlockSpec((tm, HEAD_DIM), lambda s: (prev(s) % nt, 0)),
            pl.BlockSpec((tm, HEAD_DIM), lambda s: (prev(s) % nt, 0)),
        ],
        out_specs=(row_spec(fw), row_spec(pw), row_spec(aw), row_spec(kvw),
                   pl.BlockSpec((1, kvw, tm), lambda s: (prev(s) // nt, 0, prev(s) % nt))),
        out_shape=out_shapes,
        scratch_shapes=[pltpu.VMEM((tm, in_w), F32), pltpu.VMEM((tm, in_w), F32)],
        compiler_params=_cparams(("arbitrary",)),
        name="inproj",
    )(x, mod3, norm_g, w_in_bf, qg, kg, cos_t, sin_t)


ATTN_CHUNK = 256
ATTN_SLAB = 64


def _attn_kernel(*refs, n_src, group, n_chunks, chunk):
    q_ref = refs[0]
    k_srcs = refs[1:1 + n_src]
    vt_srcs = refs[1 + n_src:1 + 2 * n_src]
    o_ref, s0_ref, s1_ref = refs[1 + 2 * n_src:4 + 2 * n_src]
    if n_src == 1:
        k_ref, vt_ref = k_srcs[0].at[0], vt_srcs[0].at[0]
    else:
        k_ref, vt_ref = refs[4 + 2 * n_src:]
        row = 0
        for ks, vts in zip(k_srcs, vt_srcs):
            rows = ks.shape[1]
            k_ref[row:row + rows, :] = ks[0]
            vt_ref[:, row:row + rows] = vts[0]
            row += rows
    dn = (((1,), (1,)), ((), ()))

    def scores(c, s_ref):
        r0 = pl.multiple_of(c * chunk, chunk)
        for g in range(group):
            q = q_ref[0, pl.ds(r0, chunk), g * HEAD_DIM:(g + 1) * HEAD_DIM]
            s_ref[g] = lax.dot_general(k_ref[...], q, dn, preferred_element_type=F32)

    def finish(c, s_ref):
        r0 = pl.multiple_of(c * chunk, chunk)
        for g in range(group):
            s = s_ref[g]
            nk = s.shape[0]
            slab = ATTN_SLAB if nk % ATTN_SLAB == 0 else nk
            m = s.reshape(nk // slab, slab, chunk).max(axis=0).max(axis=0, keepdims=True)
            p = jnp.exp2(s - m)
            l = p.reshape(nk // slab, slab, chunk).sum(axis=0).sum(axis=0, keepdims=True)
            ot = jnp.dot(vt_ref[...], p.astype(BF16), preferred_element_type=F32)
            o_ref[0, pl.ds(r0, chunk), g * HEAD_DIM:(g + 1) * HEAD_DIM] = (ot / l).T.astype(BF16)

    scores(0, s0_ref)
    if n_chunks == 1:
        finish(0, s0_ref)
        return

    def body(i, carry):
        c = 2 * i
        scores(c + 1, s1_ref)
        finish(c, s0_ref)
        scores(c + 2, s0_ref)
        finish(c + 1, s1_ref)
        return carry

    lax.fori_loop(0, n_chunks // 2 - 1, body, 0)
    scores(n_chunks - 1, s1_ref)
    finish(n_chunks - 2, s0_ref)
    finish(n_chunks - 1, s1_ref)


def _attention(q, ks, vts):
    b, n, aw = q.shape
    nk = sum(k.shape[1] for k in ks)
    n_src = len(ks)
    group = aw // HEAD_DIM // N_KV_HEADS
    gw = group * HEAD_DIM
    chunk = min(ATTN_CHUNK, n)
    n_chunks = n // chunk
    assert n % chunk == 0 and (n_chunks == 1 or n_chunks % 2 == 0)
    scratch = [pltpu.VMEM((group, nk, chunk), F32), pltpu.VMEM((group, nk, chunk), F32)]
    if n_src > 1:
        scratch += [pltpu.VMEM((nk, HEAD_DIM), BF16), pltpu.VMEM((HEAD_DIM, nk), BF16)]
    return pl.pallas_call(
        functools.partial(_attn_kernel, n_src=n_src, group=group, n_chunks=n_chunks, chunk=chunk),
        grid=(b, N_KV_HEADS),
        in_specs=[pl.BlockSpec((1, n, gw), lambda bi, hi: (bi, 0, hi))]
        + [pl.BlockSpec((1, k.shape[1], HEAD_DIM), lambda bi, hi: (bi, 0, hi)) for k in ks]
        + [pl.BlockSpec((1, HEAD_DIM, v.shape[2]), lambda bi, hi: (bi, hi, 0)) for v in vts],
        out_specs=pl.BlockSpec((1, n, gw), lambda bi, hi: (bi, 0, hi)),
        out_shape=jax.ShapeDtypeStruct((b, n, aw), BF16),
        scratch_shapes=scratch,
        compiler_params=_cparams(("parallel", "arbitrary")),
        name="attention",
    )(q, *ks, *vts)


def _fourier_kernel(f_ref, cc_ref, sc_ref, cn_ref, sn_ref, w_ref, o_ref, xc_ref, xs_ref, *, scale):
    bi = pl.program_id(1)

    @pl.when(pl.program_id(0) == 0)
    def _():
        f = f_ref[0]
        xc_ref[bi] = jnp.dot(f, cc_ref[...], preferred_element_type=F32).astype(BF16)
        xs_ref[bi] = jnp.dot(f, sc_ref[...], preferred_element_type=F32).astype(BF16)

    fr = (jnp.dot(cn_ref[...], xc_ref[bi], preferred_element_type=F32)
          - jnp.dot(sn_ref[...], xs_ref[bi], preferred_element_type=F32)) * scale
    o_ref[0] = jnp.dot(fr.astype(BF16), w_ref[0], preferred_element_type=F32).astype(BF16)


def _dft_kernel(clo_ref, slo_ref, chi_ref, shi_ref, cos_ref, sin_ref):
    c_lo, s_lo = clo_ref[...], slo_ref[...]
    for h in range(chi_ref.shape[1]):
        c_hi, s_hi = chi_ref[:, h:h + 1], shi_ref[:, h:h + 1]
        cos_ref[:, h * LANES:(h + 1) * LANES] = (c_hi * c_lo - s_hi * s_lo).astype(BF16)
        sin_ref[:, h * LANES:(h + 1) * LANES] = (s_hi * c_lo + c_hi * s_lo).astype(BF16)


def _dft_tables(n):
    k = jnp.arange(n, dtype=jnp.int32)

    def cos_sin(t):
        ang = ((k[:, None] * t[None, :]) % n).astype(F32) * (2.0 * jnp.pi / n)
        return jnp.cos(ang), jnp.sin(ang)

    if n <= 2 * LANES or n % LANES:
        return tuple(t.astype(BF16) for t in cos_sin(k))
    c_hi, s_hi = cos_sin(jnp.arange(n // LANES, dtype=jnp.int32) * LANES)
    c_lo, s_lo = cos_sin(jnp.arange(LANES, dtype=jnp.int32))
    tm = _row_tile(n, 256)

    def spec(w):
        return pl.BlockSpec((tm, w), lambda i: (i, 0))

    return pl.pallas_call(
        _dft_kernel,
        grid=(n // tm,),
        in_specs=[spec(LANES), spec(LANES), spec(n // LANES), spec(n // LANES)],
        out_specs=(spec(n), spec(n)),
        out_shape=(jax.ShapeDtypeStruct((n, n), BF16), jax.ShapeDtypeStruct((n, n), BF16)),
        compiler_params=_cparams(("parallel",)),
        name="dft_tables",
    )(c_lo, s_lo, c_hi, s_hi)


def _fourier(f, layer, w_fourier_bf, cn, sn, cc_bd, sc_bd):
    b, n, fw = f.shape
    tm = _row_tile(n, 512)
    scale = float((n * (fw // N_FOURIER_HEADS)) ** -0.5)
    return pl.pallas_call(
        functools.partial(_fourier_kernel, scale=scale),
        grid=(n // tm, b),
        in_specs=[
            pl.BlockSpec((1, n, fw), lambda ti, bi: (jnp.where(ti == 0, bi, b - 1), 0, 0)),
            pl.BlockSpec((fw, fw), lambda ti, bi: (0, 0)),
            pl.BlockSpec((fw, fw), lambda ti, bi: (0, 0)),
            pl.BlockSpec((tm, n), lambda ti, bi: (ti, 0)),
            pl.BlockSpec((tm, n), lambda ti, bi: (ti, 0)),
            pl.BlockSpec((1, fw, fw), lambda ti, bi: (layer, 0, 0)),
        ],
        out_specs=pl.BlockSpec((1, tm, fw), lambda ti, bi: (bi, ti, 0)),
        out_shape=jax.ShapeDtypeStruct((b, n, fw), BF16),
        scratch_shapes=[pltpu.VMEM((b, n, fw), BF16), pltpu.VMEM((b, n, fw), BF16)],
        compiler_params=_cparams(("arbitrary", "arbitrary")),
        name="fourier",
    )(f, cc_bd, sc_bd, cn, sn, w_fourier_bf)


POOL_HALO = 16
POOL_EDGE = 8


def _pool_kernel(p_ref, cnt_ref, w_ref, s_ref, o_ref, x_ref, s2_ref, s4_ref, s8_ref, *, n, pw):
    gdim = pw // len(POOL_WINDOWS)
    lo, hi = POOL_EDGE, n + 2 * POOL_HALO - POOL_EDGE
    for ref in (x_ref, s2_ref, s4_ref, s8_ref):
        ref[0:POOL_HALO, :] = jnp.zeros((POOL_HALO, pw), F32)
        ref[n + POOL_HALO:n + 2 * POOL_HALO, :] = jnp.zeros((POOL_HALO, pw), F32)
    x_ref[POOL_HALO:POOL_HALO + n, :] = p_ref[0]

    s2_ref[lo:hi, :] = x_ref[lo - 1:hi - 1, :] + x_ref[lo:hi, :]
    s4_ref[lo:hi, :] = s2_ref[lo - 1:hi - 1, :] + s2_ref[lo + 1:hi + 1, :]
    s8_ref[lo:hi, :] = s4_ref[lo - 2:hi - 2, :] + s4_ref[lo + 2:hi + 2, :]
    a, b = POOL_HALO, POOL_HALO + n
    s16 = s8_ref[a - 4:b - 4, :] + s8_ref[a + 4:b + 4, :]
    lane = lax.broadcasted_iota(jnp.int32, (1, pw), 1)
    acc = jnp.where(lane < gdim, s2_ref[a:b, :],
                    jnp.where(lane < 2 * gdim, s4_ref[a:b, :], jnp.where(lane < 3 * gdim, s8_ref[a:b, :], s16)))
    dlt = acc / cnt_ref[...] - x_ref[a:b, :]
    y = jnp.dot(dlt.astype(BF16), w_ref[0], preferred_element_type=F32) * s_ref[0]
    o_ref[0] = y.astype(BF16)


def _pool_counts(n, pw):
    gdim = pw // len(POOL_WINDOWS)
    half = jnp.repeat(jnp.asarray([w // 2 for w in POOL_WINDOWS], jnp.int32), gdim)[None, :]
    t = jnp.arange(n, dtype=jnp.int32)[:, None]
    return (jnp.minimum(t + half, n) - jnp.maximum(t - half, 0)).astype(F32)


def _pool(p, cnt, layer, w_pool_bd, pool_scale):
    b, n, pw = p.shape
    assert POOL_WINDOWS == (2, 4, 8, 16)
    buf = pltpu.VMEM((n + 2 * POOL_HALO, pw), F32)
    return pl.pallas_call(
        functools.partial(_pool_kernel, n=n, pw=pw),
        grid=(b,),
        in_specs=[
            pl.BlockSpec((1, n, pw), lambda bi: (bi, 0, 0)),
            pl.BlockSpec((n, pw), lambda bi: (0, 0)),
            pl.BlockSpec((1, pw, pw), lambda bi: (layer, 0, 0)),
            pl.BlockSpec((1, 1, pw), lambda bi: (layer, 0, 0)),
        ],
        out_specs=pl.BlockSpec((1, n, pw), lambda bi: (bi, 0, 0)),
        out_shape=jax.ShapeDtypeStruct((b, n, pw), BF16),
        scratch_shapes=[buf, buf, buf, buf],
        compiler_params=_cparams(("parallel",)),
        name="pool",
    )(p, cnt, w_pool_bd, pool_scale)


def _outproj_kernel(fo_ref, po_ref, ao_ref, x_ref, modc_ref, modp_ref, g_ref, w_ref, wr_ref,
                    xn_ref, h_ref, aff_ref, xn0_ref, xn1_ref, *, d, fw, pw):
    step = pl.program_id(0)

    @pl.when(step == 0)
    def _():
        xn1_ref[...] = jnp.zeros_like(xn1_ref)

    def project(dst_ref):
        w = w_ref[0]
        ox = jnp.dot(fo_ref[0], w[0:fw], preferred_element_type=F32)
        ox = ox + jnp.dot(po_ref[0], w[fw:fw + pw], preferred_element_type=F32)
        ox = ox + jnp.dot(ao_ref[0], w[fw + pw:], preferred_element_type=F32)
        dst_ref[...] = x_ref[0] + modc_ref[0][:, 2 * d:3 * d] * ox

    def emit(src_ref):
        mod = modp_ref[0]
        xn = src_ref[...]
        xn_ref[0] = xn
        h = _modulate(xn, g_ref[0], mod[:, 3 * d:4 * d], mod[:, 4 * d:5 * d])
        logits = jnp.dot(h.astype(BF16), wr_ref[0], preferred_element_type=F32)
        lane = lax.broadcasted_iota(jnp.int32, logits.shape, 1)
        logits = jnp.where(lane < N_EXPERTS, logits, -jnp.inf)
        e = jnp.exp(logits - logits.max(axis=-1, keepdims=True))
        aff = e / e.sum(axis=-1, keepdims=True)
        aff_ref[0] = aff.T[0:N_EXPERTS, :]
        a_hi = aff.astype(BF16).astype(F32)
        r1 = aff - a_hi
        a_mid = r1.astype(BF16).astype(F32)
        a_lo = (r1 - a_mid).astype(BF16).astype(F32)
        tail = a_hi + pltpu.roll(a_mid, N_EXPERTS, 1) + pltpu.roll(a_lo, 2 * N_EXPERTS, 1)
        h_ref[0, :, 0:d] = h.astype(BF16)
        h_ref[0, :, d:d + LANES] = tail.astype(BF16)

    @pl.when(step % 2 == 0)
    def _():
        project(xn0_ref)
        emit(xn1_ref)

    @pl.when(step % 2 == 1)
    def _():
        project(xn1_ref)
        emit(xn0_ref)


def _outproj(fo, po, ao, x, mod3, mod_row, layer, norm_g, w_out_bf, w_router_pad):
    b, n, d = x.shape
    fw = fo.shape[-1]
    pw = po.shape[-1]
    aw = ao.shape[-1]
    n6 = mod3.shape[-1]
    tm = _row_tile(n, 512)
    nt = n // tm
    total = b * nt

    def cur(s):
        return jnp.minimum(s, total - 1)

    def prev(s):
        return jnp.maximum(s - 1, 0)

    def in_spec(w):
        return pl.BlockSpec((1, tm, w), lambda s: (cur(s) // nt, cur(s) % nt, 0))

    return pl.pallas_call(
        functools.partial(_outproj_kernel, d=d, fw=fw, pw=pw),
        grid=(total + 1,),
        in_specs=[
            in_spec(fw), in_spec(pw), in_spec(aw), in_spec(d),
            pl.BlockSpec((1, 1, n6), lambda s: (mod_row(cur(s) // nt), 0, 0)),
            pl.BlockSpec((1, 1, n6), lambda s: (mod_row(prev(s) // nt), 0, 0)),
            pl.BlockSpec((1, 1, d), lambda s: (layer, 0, 0)),
            pl.BlockSpec((1, d, d), lambda s: (layer, 0, 0)),
            pl.BlockSpec((1, d, LANES), lambda s: (layer, 0, 0)),
        ],
        out_specs=(pl.BlockSpec((1, tm, d), lambda s: (prev(s) // nt, prev(s) % nt, 0)),
                   pl.BlockSpec((1, tm, d + LANES), lambda s: (prev(s) // nt, prev(s) % nt, 0)),
                   pl.BlockSpec((1, N_EXPERTS, tm), lambda s: (prev(s) // nt, 0, prev(s) % nt))),
        out_shape=(jax.ShapeDtypeStruct((b, n, d), F32),
                   jax.ShapeDtypeStruct((b, n, d + LANES), BF16),
                   jax.ShapeDtypeStruct((b, N_EXPERTS, n), F32)),
        scratch_shapes=[pltpu.VMEM((tm, d), F32), pltpu.VMEM((tm, d), F32)],
        compiler_params=_cparams(("arbitrary",)),
        name="outproj",
    )(fo, po, ao, x, mod3, mod3, norm_g, w_out_bf, w_router_pad)


ROUTE_BLOCK = 256
COMBINE_WINDOW = 64
NOT_IN_WINDOW = -(1 << 20)
FLAG_LANE0 = LANES // 2


def _combine_geometry(cap):
    win = min(COMBINE_WINDOW, cap)
    return win, N_EXPERTS


def _route_kernel(aff_ref, key_ref, relw_ref, soff_ref, tri_ref, *, n, cap, blk, win, grp):
    nb = n // blk
    e_n = N_EXPERTS

    @pl.when(pl.program_id(0) == 0)
    def _():
        r = lax.broadcasted_iota(jnp.int32, (blk, blk), 0)
        c = lax.broadcasted_iota(jnp.int32, (blk, blk), 1)
        tri_ref[...] = jnp.where(r < c, 1.0, 0.0).astype(BF16)

    aff = aff_ref[0]

    def count_ge(bits):
        return jnp.sum(jnp.where(aff >= pltpu.bitcast(bits, F32), 1.0, 0.0), axis=-1, keepdims=True)

    def search(i, thr_bits):
        lo = lax.shift_left(jnp.int32(1), 29 - 2 * i)
        hi = lo + lo
        take_hi = count_ge(thr_bits | hi) >= cap
        take_both = count_ge(thr_bits | hi | lo) >= cap
        take_lo = count_ge(thr_bits | lo) >= cap
        with_hi = jnp.where(take_both, thr_bits | hi | lo, thr_bits | hi)
        without_hi = jnp.where(take_lo, thr_bits | lo, thr_bits)
        return jnp.where(take_hi, with_hi, without_hi)

    thr_bits = lax.fori_loop(0, 15, search, jnp.zeros((e_n, 1), jnp.int32))
    thr_bits = jnp.where(count_ge(thr_bits | 1) >= cap, thr_bits | 1, thr_bits)
    thr = pltpu.bitcast(thr_bits, F32)
    gt = jnp.where(aff > thr, 1.0, 0.0)
    eq = jnp.where(aff == thr, 1.0, 0.0)
    need = cap - jnp.sum(gt, axis=-1, keepdims=True)
    tri = tri_ref[...]

    sel = []
    off = jnp.zeros((e_n, 1), F32)
    for j in range(nb):
        eqj = eq[:, j * blk:(j + 1) * blk]
        rank = jnp.dot(eqj.astype(BF16), tri, preferred_element_type=F32) + off
        off = off + jnp.sum(eqj, axis=-1, keepdims=True)
        sel.append(jnp.maximum(gt[:, j * blk:(j + 1) * blk], jnp.where(rank < need, eqj, 0.0)))

    lane = lax.broadcasted_iota(jnp.int32, (e_n, LANES), 1)
    q_off = ((lax.broadcasted_iota(jnp.int32, (e_n, 1), 0) % grp) * win).astype(F32)
    soff = jnp.zeros((e_n, LANES), F32)
    off = jnp.zeros((e_n, 1), F32)
    for j in range(nb):
        selj = sel[j]
        pos = jnp.dot(selj.astype(BF16), tri, preferred_element_type=F32) + off
        soff = jnp.where(lane == j, off, soff)
        start = jnp.minimum(jnp.floor(off * (1.0 / 16.0)) * 16.0, float(cap - win))
        rel = pos - start
        chosen = selj > 0.0
        key_ref[0, :, j * blk:(j + 1) * blk] = jnp.where(chosen, pos, -1.0).astype(jnp.int32)
        relw_ref[0, :, j * blk:(j + 1) * blk] = jnp.where(
            chosen & (rel < win), rel + q_off, float(NOT_IN_WINDOW)).astype(jnp.int32)
        count = jnp.sum(selj, axis=-1, keepdims=True)
        spills = jnp.max(jnp.where(off - start + count > win, 1.0, 0.0), axis=0, keepdims=True)
        soff = jnp.where(lane == FLAG_LANE0 + j, spills, soff)
        off = off + count
    soff_ref[0] = jnp.where(lane == nb, off, soff).astype(jnp.int32)


def _route(aff_t, cap):
    b, e, n = aff_t.shape
    blk = min(ROUTE_BLOCK, n)
    assert n % blk == 0 and n // blk < FLAG_LANE0
    win, grp = _combine_geometry(cap)

    def spec(w):
        return pl.BlockSpec((1, e, w), lambda bi: (bi, 0, 0))

    return pl.pallas_call(
        functools.partial(_route_kernel, n=n, cap=cap, blk=blk, win=win, grp=grp),
        grid=(b,),
        in_specs=[spec(n)],
        out_specs=(spec(n), spec(n), spec(LANES)),
        out_shape=(jax.ShapeDtypeStruct((b, e, n), jnp.int32),
                   jax.ShapeDtypeStruct((b, e, n), jnp.int32),
                   jax.ShapeDtypeStruct((b, e, LANES), jnp.int32)),
        scratch_shapes=[pltpu.VMEM((blk, blk), BF16)],
        compiler_params=_cparams(("arbitrary",)),
        name="moe_route",
    )(aff_t)


def _window_start(soff_ref, e, j, stride, cap, win):
    s = soff_ref[0, 0, e * stride + j]
    return s, jnp.minimum((s >> 4) << 4, cap - win)


def _gather_kernel(soff_ref, relw_ref, key_ref, h_ref, o_ref, *, cap, n_exp, nb, win):
    j = pl.program_id(1)
    stride = nb + 1
    blk = relw_ref.shape[-1]

    @pl.when(j == 0)
    def _():
        o_ref[...] = jnp.zeros_like(o_ref)

    row = lax.broadcasted_iota(jnp.int32, (n_exp * win, blk), 0)
    slabs = [jnp.broadcast_to(relw_ref[0, e:e + 1, :], (win, blk)) for e in range(n_exp)]
    onehot = jnp.where(jnp.concatenate(slabs, axis=0) == row, 1.0, 0.0).astype(BF16)
    rows = jnp.dot(onehot, h_ref[0], preferred_element_type=F32).astype(BF16)
    for e in range(n_exp):
        _, start = _window_start(soff_ref, e, j, stride, cap, win)
        dst = o_ref.at[0, e, pl.ds(pl.multiple_of(start, 16), win), :]
        dst[...] = dst[...] + rows[e * win:(e + 1) * win]

    row_w = lax.broadcasted_iota(jnp.int32, (win, blk), 0)

    def further_windows(e, carry):
        s, start0 = _window_start(soff_ref, e, j, stride, cap, win)
        count = soff_ref[0, 0, e * stride + j + 1] - s
        n_win = (s - start0 + count + win - 1) // win

        def body(k, c):
            lo = start0 + k * win
            st = pl.multiple_of(jnp.minimum(lo, cap - win), 16)
            krow = key_ref[0, pl.ds(e, 1), :]
            hot = jnp.where((krow - st == row_w) & (krow >= lo), 1.0, 0.0).astype(BF16)
            dst = o_ref.at[0, e, pl.ds(st, win), :]
            dst[...] = dst[...] + jnp.dot(hot, h_ref[0], preferred_element_type=F32).astype(BF16)
            return c

        lax.fori_loop(1, n_win, body, 0)
        return carry

    @pl.when(soff_ref[0, 0, n_exp * stride + j] > 0)
    def _():
        lax.fori_loop(0, n_exp, further_windows, 0)


def _route_scalars(soff, nb):
    b, e, _ = soff.shape
    return jnp.concatenate([soff[:, :, :nb + 1].reshape(b, 1, e * (nb + 1)),
                            soff[:, :1, FLAG_LANE0:FLAG_LANE0 + nb]], axis=-1)


def _gather(h_aug, soff, relw, key, cap):
    b, n, dw = h_aug.shape
    e = relw.shape[1]
    blk = min(ROUTE_BLOCK, n)
    nb = n // blk
    win, _ = _combine_geometry(cap)
    return pl.pallas_call(
        functools.partial(_gather_kernel, cap=cap, n_exp=e, nb=nb, win=win),
        grid=(b, nb),
        in_specs=[
            pl.BlockSpec((1, 1, e * (nb + 1) + nb), lambda bi, ji: (bi, 0, 0), memory_space=pltpu.SMEM),
            pl.BlockSpec((1, e, blk), lambda bi, ji: (bi, 0, ji)),
            pl.BlockSpec((1, e, blk), lambda bi, ji: (bi, 0, ji)),
            pl.BlockSpec((1, blk, dw), lambda bi, ji: (bi, ji, 0)),
        ],
        out_specs=pl.BlockSpec((1, e, cap, dw), lambda bi, ji: (bi, 0, 0, 0)),
        out_shape=jax.ShapeDtypeStruct((b, e, cap, dw), BF16),
        compiler_params=_cparams(("parallel", "arbitrary")),
        name="moe_gather",
    )(_route_scalars(soff, nb), relw, key, h_aug)


def _ffn_kernel(*refs, bt, n_sets, d):
    ins = refs[:n_sets]
    wg_ref, wu_ref, wd_ref = refs[n_sets:n_sets + 3]
    outs = refs[n_sets + 3:2 * n_sets + 3]
    wg_bf, wu_bf, wd_bf = refs[2 * n_sets + 3:]
    ei = pl.program_id(0)

    @pl.when(pl.program_id(1) == 0)
    def _():
        wg_bf[...] = wg_ref[0, 0].astype(BF16)
        wu_bf[...] = wu_ref[0, 0].astype(BF16)
        wd_bf[...] = wd_ref[0, 0].astype(BF16)

    def expert(rows):
        xs = rows[:, 0:d]
        tail = rows[:, d:d + LANES].astype(F32)
        lane = lax.broadcasted_iota(jnp.int32, tail.shape, 1)
        mine = (lane == ei) | (lane == ei + N_EXPERTS) | (lane == ei + 2 * N_EXPERTS)
        gate = jnp.sum(jnp.where(mine, tail, 0.0), axis=-1, keepdims=True)
        a = jnp.dot(xs, wg_bf[...], preferred_element_type=F32)
        u = jnp.dot(xs, wu_bf[...], preferred_element_type=F32)
        hid = (a * jax.nn.sigmoid(a) * u).astype(BF16)
        return (jnp.dot(hid, wd_bf[...], preferred_element_type=F32) * gate).astype(BF16)

    for xs_ref, y_ref in zip(ins, outs):
        cap, dw = xs_ref.shape[2], xs_ref.shape[3]
        if cap >= 256:
            for bi in range(bt):
                y_ref[bi, 0] = expert(xs_ref[bi, 0])
        else:
            y_ref[:, 0] = expert(xs_ref[:, 0].reshape(bt * cap, dw)).reshape(bt, cap, d)


def _ffn(sets, layer, w_gate, w_up, w_down):
    b, e, _, dw = sets[0].shape
    d = w_gate.shape[-2]
    ff = w_gate.shape[-1]
    bt = min(b, 8)
    assert b % bt == 0 and dw == d + LANES
    in_specs = [pl.BlockSpec((bt, 1, xs.shape[2], dw), lambda ei, ji: (ji, ei, 0, 0)) for xs in sets]
    out_specs = [pl.BlockSpec((bt, 1, xs.shape[2], d), lambda ei, ji: (ji, ei, 0, 0)) for xs in sets]
    out_shape = [jax.ShapeDtypeStruct(xs.shape[:3] + (d,), BF16) for xs in sets]
    in_specs += [
        pl.BlockSpec((1, 1, d, ff), lambda ei, ji: (layer, ei, 0, 0)),
        pl.BlockSpec((1, 1, d, ff), lambda ei, ji: (layer, ei, 0, 0)),
        pl.BlockSpec((1, 1, ff, d), lambda ei, ji: (layer, ei, 0, 0)),
    ]
    return pl.pallas_call(
        functools.partial(_ffn_kernel, bt=bt, n_sets=len(sets), d=d),
        grid=(e, b // bt),
        in_specs=in_specs,
        out_specs=tuple(out_specs),
        out_shape=tuple(out_shape),
        scratch_shapes=[pltpu.VMEM((d, ff), BF16), pltpu.VMEM((d, ff), BF16), pltpu.VMEM((ff, d), BF16)],
        compiler_params=_cparams(("parallel", "arbitrary")),
        name="moe_ffn",
    )(*sets, w_gate, w_up, w_down)


def _combine_kernel(soff_ref, relw_ref, key_ref, y_ref, xn_ref, mod_ref, o_ref, rhs_ref,
                    *, cap, d, n_exp, nb, win, grp):
    j = pl.program_id(1)
    stride = nb + 1
    blk = relw_ref.shape[-1]
    dn_t = (((0,), (0,)), ((), ()))

    def window_start(e):
        return _window_start(soff_ref, e, j, stride, cap, win)

    row = lax.broadcasted_iota(jnp.int32, (grp * win, blk), 0)
    acc = jnp.zeros((blk, d), F32)
    for g0 in range(0, n_exp, grp):
        slabs = []
        for q in range(grp):
            e = g0 + q
            _, start = window_start(e)
            rhs_ref[q * win:(q + 1) * win, :] = y_ref[0, e, pl.ds(pl.multiple_of(start, 16), win), :]
            slabs.append(jnp.broadcast_to(relw_ref[0, e:e + 1, :], (win, blk)))
        onehot = jnp.where(jnp.concatenate(slabs, axis=0) == row, 1.0, 0.0).astype(BF16)
        acc = acc + lax.dot_general(onehot, rhs_ref[...], dn_t, preferred_element_type=F32)
    o_ref[0] = acc

    row_w = lax.broadcasted_iota(jnp.int32, (win, blk), 0)

    def further_windows(e, carry):
        s, start0 = window_start(e)
        count = soff_ref[0, 0, e * stride + j + 1] - s
        n_win = (s - start0 + count + win - 1) // win

        def body(k, c):
            lo = start0 + k * win
            st = pl.multiple_of(jnp.minimum(lo, cap - win), 16)
            krow = key_ref[0, pl.ds(e, 1), :]
            onehot = jnp.where((krow - st == row_w) & (krow >= lo), 1.0, 0.0).astype(BF16)
            o_ref[0] = o_ref[0] + lax.dot_general(onehot, y_ref[0, e, pl.ds(st, win), :], dn_t,
                                                  preferred_element_type=F32)
            return c

        lax.fori_loop(1, n_win, body, 0)
        return carry

    @pl.when(soff_ref[0, 0, n_exp * stride + j] > 0)
    def _():
        lax.fori_loop(0, n_exp, further_windows, 0)

    o_ref[0] = xn_ref[0] + mod_ref[0][:, 5 * d:6 * d] * o_ref[0]


def _combine(y, soff, relw, key, xn, mod3, mod_row):
    b, e, cap, d = y.shape
    n = xn.shape[1]
    n6 = mod3.shape[-1]
    blk = min(ROUTE_BLOCK, n)
    nb = n // blk
    win, grp = _combine_geometry(cap)
    return pl.pallas_call(
        functools.partial(_combine_kernel, cap=cap, d=d, n_exp=e, nb=nb, win=win, grp=grp),
        grid=(b, nb),
        in_specs=[
            pl.BlockSpec((1, 1, e * (nb + 1) + nb), lambda bi, ji: (bi, 0, 0), memory_space=pltpu.SMEM),
            pl.BlockSpec((1, e, blk), lambda bi, ji: (bi, 0, ji)),
            pl.BlockSpec((1, e, blk), lambda bi, ji: (bi, 0, ji)),
            pl.BlockSpec((1, e, cap, d), lambda bi, ji: (bi, 0, 0, 0)),
            pl.BlockSpec((1, blk, d), lambda bi, ji: (bi, ji, 0)),
            pl.BlockSpec((1, 1, n6), lambda bi, ji: (mod_row(bi), 0, 0)),
        ],
        out_specs=pl.BlockSpec((1, blk, d), lambda bi, ji: (bi, ji, 0)),
        out_shape=jax.ShapeDtypeStruct((b, n, d), F32),
        scratch_shapes=[pltpu.VMEM((grp * win, d), BF16)],
        compiler_params=_cparams(("parallel", "arbitrary")),
        name="moe_combine",
    )(_route_scalars(soff, nb), relw, key, y, xn, mod3)


def _moe_dispatch(h_aug, aff_t):
    n = h_aug.shape[1]
    cap = EC_CAPACITY_FACTOR * n // N_EXPERTS
    key, relw, soff = _route(aff_t, cap)
    return _gather(h_aug, soff, relw, key, cap), (soff, relw, key)


def _rope_tables(n):
    t = jnp.arange(n, dtype=jnp.int32)
    row = (t // GRID_W).astype(F32)
    col = (t % GRID_W).astype(F32)
    inv = ROPE_THETA ** (-jnp.arange(0, ROPE_HALF, 2, dtype=F32) / ROPE_HALF)
    ar = row[:, None] * inv[None, :]
    ac = col[:, None] * inv[None, :]
    ang = jnp.concatenate([ar, ar, ac, ac], axis=-1)
    lane = jnp.arange(HEAD_DIM)
    sign = jnp.where((lane % ROPE_HALF) < ROPE_HALF // 2, -1.0, 1.0).astype(F32)
    return jnp.cos(ang), jnp.sin(ang) * sign[None, :]


def _block_diag(blocks):
    g, a, c = blocks.shape
    eye = jnp.eye(g, dtype=blocks.dtype)
    return (eye[:, None, :, None] * blocks[:, :, None, :]).reshape(g * a, g * c)


def kernel(x, c, ctx, c_ctx, ada_w, ada_b, norm1_g, norm2_g, w_in, w_fourier, w_pool, pool_scale,
           q_norm_g, k_norm_g, w_out, w_router, w_gate, w_up, w_down):
    b, n, d = x.shape
    lc = ctx.shape[1]
    depth = ada_w.shape[0]
    fw = d // 4

    rows = -(-(b + 1) // SUBLANES) * SUBLANES
    cond = jnp.zeros((rows, d), F32).at[:b].set(c).at[b].set(c_ctx)
    mod = _ada_mod(cond, ada_w, ada_b)

    def lat_row(bi):
        return bi

    def ctx_row(bi):
        return b

    norm1_g = norm1_g.reshape(depth, 1, d)
    norm2_g = norm2_g.reshape(depth, 1, d)
    q_norm_g = q_norm_g.reshape(depth, 1, HEAD_DIM)
    k_norm_g = k_norm_g.reshape(depth, 1, HEAD_DIM)
    pool_scale = pool_scale.reshape(depth, 1, d // 4)
    w_in_bf = w_in.astype(BF16)
    w_out_bf = w_out.astype(BF16)
    w_fourier_bf = w_fourier.astype(BF16)
    w_router_pad = jnp.zeros((depth, d, LANES), BF16).at[:, :, :N_EXPERTS].set(w_router.astype(BF16))
    w_pool_bd = jax.vmap(_block_diag)(w_pool).astype(BF16)

    cnt_t = _pool_counts(n, d // 4)
    cnt_c = _pool_counts(lc, d // 4)
    cos_t, sin_t = _rope_tables(n)
    cos_c, sin_c = _rope_tables(lc)
    cn, sn = _dft_tables(n)
    cnc, snc = _dft_tables(lc)
    hd = fw // N_FOURIER_HEADS
    cch, sch = _dft_tables(hd)
    cc_bd = _block_diag(jnp.broadcast_to(cch, (N_FOURIER_HEADS, hd, hd)))
    sc_bd = _block_diag(jnp.broadcast_to(sch, (N_FOURIER_HEADS, hd, hd)))

    for i in range(depth):
        last = i == depth - 1
        mod3 = mod[i].reshape(rows, 1, 6 * d)

        fx, px, qx, kx, vtx = _inproj(x, mod3, lat_row, i, norm1_g, w_in_bf, q_norm_g, k_norm_g,
                                      cos_t, sin_t, True)
        fc, pc, qc, kc, vtc = _inproj(ctx, mod3, ctx_row, i, norm1_g, w_in_bf, q_norm_g, k_norm_g,
                                      cos_c, sin_c, False)

        ax = _attention(qx, [kx, kc], [vtx, vtc])
        fox = _fourier(fx, i, w_fourier_bf, cn, sn, cc_bd, sc_bd)
        pox = _pool(px, cnt_t, i, w_pool_bd, pool_scale)
        xn, hx, affx = _outproj(fox, pox, ax, x, mod3, lat_row, i, norm2_g, w_out_bf, w_router_pad)
        xs, route_x = _moe_dispatch(hx, affx)

        if last:
            (yx,) = _ffn([xs], i, w_gate, w_up, w_down)
        else:
            ac = _attention(qc, [kc], [vtc])
            foc = _fourier(fc, i, w_fourier_bf, cnc, snc, cc_bd, sc_bd)
            poc = _pool(pc, cnt_c, i, w_pool_bd, pool_scale)
            cn_, hc, affc = _outproj(foc, poc, ac, ctx, mod3, ctx_row, i, norm2_g, w_out_bf, w_router_pad)
            xsc, route_c = _moe_dispatch(hc, affc)
            yx, yc = _ffn([xs, xsc], i, w_gate, w_up, w_down)
            ctx = _combine(yc, *route_c, cn_, mod3, ctx_row)
        x = _combine(yx, *route_x, xn, mod3, lat_row)
    return x
```

```python
import functools

import jax
import jax.numpy as jnp
from jax import lax
from jax.experimental import pallas as pl
from jax.experimental.pallas import tpu as pltpu

GRID_W = 64
EPS = 1e-6
N_FOURIER_HEADS = 4
POOL_WINDOWS = (2, 4, 8, 16)
HEAD_DIM = 128
N_KV_HEADS = 2
ROPE_HALF = HEAD_DIM // 2
ROPE_THETA = 10000.0
N_EXPERTS = 16
EC_CAPACITY_FACTOR = 2

LANES = 128
SUBLANES = 8
VMEM_LIMIT_BYTES = 56 * 1024 * 1024

LOG2_E = 1.4426950408889634

F32 = jnp.float32
BF16 = jnp.bfloat16


def _cparams(semantics):
    return pltpu.CompilerParams(dimension_semantics=semantics, vmem_limit_bytes=VMEM_LIMIT_BYTES)


def _row_tile(n, target):
    t = min(n, target)
    assert n % t == 0
    return t


def _ada_kernel(c_ref, w_ref, b_ref, o_ref):
    c = c_ref[...]
    s = c * jax.nn.sigmoid(c)
    r = s.shape[0]
    s_hi = s.astype(BF16)
    s_r1 = s - s_hi.astype(F32)
    s_mid = s_r1.astype(BF16)
    s_lo = (s_r1 - s_mid.astype(F32)).astype(BF16)
    lhs = jnp.concatenate([s_hi, s_mid, s_lo], axis=0)
    w = w_ref[0]
    w_hi = w.astype(BF16)
    w_lo = (w - w_hi.astype(F32)).astype(BF16)
    p_hi = jnp.dot(lhs, w_hi, preferred_element_type=F32)
    p_lo = jnp.dot(lhs, w_lo, preferred_element_type=F32)
    o_ref[0] = (p_lo[r:2 * r] + p_lo[0:r]) + (p_hi[2 * r:3 * r] + p_hi[r:2 * r]) + p_hi[0:r] + b_ref[0]


def _ada_mod(cond, ada_w, ada_b):
    depth, d, n6 = ada_w.shape
    r = cond.shape[0]
    tn = _row_tile(n6, 1536)
    return pl.pallas_call(
        _ada_kernel,
        grid=(depth, n6 // tn),
        in_specs=[
            pl.BlockSpec((r, d), lambda i, j: (0, 0)),
            pl.BlockSpec((1, d, tn), lambda i, j: (i, 0, j)),
            pl.BlockSpec((1, 1, tn), lambda i, j: (i, 0, j)),
        ],
        out_specs=pl.BlockSpec((1, r, tn), lambda i, j: (i, 0, j)),
        out_shape=jax.ShapeDtypeStruct((depth, r, n6), F32),
        compiler_params=_cparams(("arbitrary", "arbitrary")),
        name="ada_mod",
    )(cond, ada_w, ada_b.reshape(depth, 1, n6))


def _modulate(xf, g, shift, scale):
    ms = jnp.mean(xf * xf, axis=-1, keepdims=True)
    y = xf * lax.rsqrt(ms + EPS) * g
    return y * (1.0 + scale) + shift


def _head_rms(xh, g):
    ms = jnp.mean(xh * xh, axis=-1, keepdims=True)
    return xh * lax.rsqrt(ms + EPS) * g


def _rope(xh, cos, sin_signed, lo_half):
    partner = jnp.where(lo_half, pltpu.roll(xh, HEAD_DIM - ROPE_HALF // 2, 1), pltpu.roll(xh, ROPE_HALF // 2, 1))
    return xh * cos + partner * sin_signed


def _inproj_kernel(x_ref, mod_ref, g_ref, w_ref, qg_ref, kg_ref, cos_ref, sin_ref,
                   f_ref, p_ref, q_ref, k_ref, vt_ref, pr0_ref, pr1_ref, *, d, fw, pw, aw, kvw, use_rope):
    step = pl.program_id(0)

    @pl.when(step == 0)
    def _():
        pr1_ref[...] = jnp.zeros_like(pr1_ref)

    def project(dst_ref):
        mod = mod_ref[0]
        h = _modulate(x_ref[0], g_ref[0], mod[:, 0:d], mod[:, d:2 * d]).astype(BF16)
        dst_ref[...] = jnp.dot(h, w_ref[0], preferred_element_type=F32)

    def emit(src_ref):
        f_ref[0] = src_ref[:, 0:fw].astype(BF16)
        p_ref[0] = src_ref[:, fw:fw + pw]
        q_off = fw + pw
        k_off = q_off + aw
        v_off = k_off + kvw
        if use_rope:
            cos = cos_ref[...]
            sin = sin_ref[...]
            lane = lax.broadcasted_iota(jnp.int32, cos.shape, 1)
            lo_half = (lane % ROPE_HALF) < (ROPE_HALF // 2)
        q_scale = HEAD_DIM ** -0.5 * LOG2_E
        for j in range(aw // HEAD_DIM):
            qh = _head_rms(src_ref[:, q_off + j * HEAD_DIM:q_off + (j + 1) * HEAD_DIM], qg_ref[0])
            if use_rope:
                qh = _rope(qh, cos, sin, lo_half)
            q_ref[0, :, j * HEAD_DIM:(j + 1) * HEAD_DIM] = (qh * q_scale).astype(BF16)
        for j in range(kvw // HEAD_DIM):
            kh = _head_rms(src_ref[:, k_off + j * HEAD_DIM:k_off + (j + 1) * HEAD_DIM], kg_ref[0])
            if use_rope:
                kh = _rope(kh, cos, sin, lo_half)
            k_ref[0, :, j * HEAD_DIM:(j + 1) * HEAD_DIM] = kh.astype(BF16)
        vt_ref[0] = src_ref[:, v_off:v_off + kvw].T.astype(BF16)

    @pl.when(step % 2 == 0)
    def _():
        project(pr0_ref)
        emit(pr1_ref)

    @pl.when(step % 2 == 1)
    def _():
        project(pr1_ref)
        emit(pr0_ref)


def _inproj(x, mod3, mod_row, layer, norm_g, w_in_bf, qg, kg, cos_t, sin_t, use_rope):
    b, n, d = x.shape
    in_w = w_in_bf.shape[-1]
    fw = d // 4
    pw = d // 4
    aw = d // 2
    kvw = N_KV_HEADS * HEAD_DIM
    tm = _row_tile(n, 1024)
    nt = n // tm
    total = b * nt
    n6 = mod3.shape[-1]
    kern = functools.partial(_inproj_kernel, d=d, fw=fw, pw=pw, aw=aw, kvw=kvw, use_rope=use_rope)
    out_shapes = (
        jax.ShapeDtypeStruct((b, n, fw), BF16),
        jax.ShapeDtypeStruct((b, n, pw), F32),
        jax.ShapeDtypeStruct((b, n, aw), BF16),
        jax.ShapeDtypeStruct((b, n, kvw), BF16),
        jax.ShapeDtypeStruct((b, kvw, n), BF16),
    )

    def cur(s):
        return jnp.minimum(s, total - 1)

    def prev(s):
        return jnp.maximum(s - 1, 0)

    def row_spec(w):
        return pl.BlockSpec((1, tm, w), lambda s: (prev(s) // nt, prev(s) % nt, 0))

    return pl.pallas_call(
        kern,
        grid=(total + 1,),
        in_specs=[
            pl.BlockSpec((1, tm, d), lambda s: (cur(s) // nt, cur(s) % nt, 0)),
            pl.BlockSpec((1, 1, n6), lambda s: (mod_row(cur(s) // nt), 0, 0)),
            pl.BlockSpec((1, 1, d), lambda s: (layer, 0, 0)),
            pl.BlockSpec((1, d, in_w), lambda s: (layer, 0, 0)),
            pl.BlockSpec((1, 1, HEAD_DIM), lambda s: (layer, 0, 0)),
            pl.BlockSpec((1, 1, HEAD_DIM), lambda s: (layer, 0, 0)),
            pl.BlockSpec((tm, HEAD_DIM), lambda s: (prev(s) % nt, 0)),
            pl.BlockSpec((tm, HEAD_DIM), lambda s: (prev(s) % nt, 0)),
        ],
        out_specs=(row_spec(fw), row_spec(pw), row_spec(aw), row_spec(kvw),
                   pl.BlockSpec((1, kvw, tm), lambda s: (prev(s) // nt, 0, prev(s) % nt))),
        out_shape=out_shapes,
        scratch_shapes=[pltpu.VMEM((tm, in_w), F32), pltpu.VMEM((tm, in_w), F32)],
        compiler_params=_cparams(("arbitrary",)),
        name="inproj",
    )(x, mod3, norm_g, w_in_bf, qg, kg, cos_t, sin_t)


ATTN_CHUNK = 256
ATTN_SLAB = 64


def _attn_kernel(*refs, n_src, group, n_chunks, chunk):
    q_ref = refs[0]
    k_srcs = refs[1:1 + n_src]
    vt_srcs = refs[1 + n_src:1 + 2 * n_src]
    o_ref, s0_ref, s1_ref = refs[1 + 2 * n_src:4 + 2 * n_src]
    if n_src == 1:
        k_ref, vt_ref = k_srcs[0].at[0], vt_srcs[0].at[0]
    else:
        k_ref, vt_ref = refs[4 + 2 * n_src:]
        row = 0
        for ks, vts in zip(k_srcs, vt_srcs):
            rows = ks.shape[1]
            k_ref[row:row + rows, :] = ks[0]
            vt_ref[:, row:row + rows] = vts[0]
            row += rows
    dn = (((1,), (1,)), ((), ()))

    def scores(c, s_ref):
        r0 = pl.multiple_of(c * chunk, chunk)
        for g in range(group):
            q = q_ref[0, pl.ds(r0, chunk), g * HEAD_DIM:(g + 1) * HEAD_DIM]
            s_ref[g] = lax.dot_general(k_ref[...], q, dn, preferred_element_type=F32)

    def finish(c, s_ref):
        r0 = pl.multiple_of(c * chunk, chunk)
        for g in range(group):
            s = s_ref[g]
            nk = s.shape[0]
            slab = ATTN_SLAB if nk % ATTN_SLAB == 0 else nk
            m = s.reshape(nk // slab, slab, chunk).max(axis=0).max(axis=0, keepdims=True)
            p = jnp.exp2(s - m)
            l = p.reshape(nk // slab, slab, chunk).sum(axis=0).sum(axis=0, keepdims=True)
            ot = jnp.dot(vt_ref[...], p.astype(BF16), preferred_element_type=F32)
            o_ref[0, pl.ds(r0, chunk), g * HEAD_DIM:(g + 1) * HEAD_DIM] = (ot / l).T.astype(BF16)

    scores(0, s0_ref)
    if n_chunks == 1:
        finish(0, s0_ref)
        return

    def body(i, carry):
        c = 2 * i
        scores(c + 1, s1_ref)
        finish(c, s0_ref)
        scores(c + 2, s0_ref)
        finish(c + 1, s1_ref)
        return carry

    lax.fori_loop(0, n_chunks // 2 - 1, body, 0)
    scores(n_chunks - 1, s1_ref)
    finish(n_chunks - 2, s0_ref)
    finish(n_chunks - 1, s1_ref)


def _attention(q, ks, vts):
    b, n, aw = q.shape
    nk = sum(k.shape[1] for k in ks)
    n_src = len(ks)
    group = aw // HEAD_DIM // N_KV_HEADS
    gw = group * HEAD_DIM
    chunk = min(ATTN_CHUNK, n)
    n_chunks = n // chunk
    assert n % chunk == 0 and (n_chunks == 1 or n_chunks % 2 == 0)
    scratch = [pltpu.VMEM((group, nk, chunk), F32), pltpu.VMEM((group, nk, chunk), F32)]
    if n_src > 1:
        scratch += [pltpu.VMEM((nk, HEAD_DIM), BF16), pltpu.VMEM((HEAD_DIM, nk), BF16)]
    return pl.pallas_call(
        functools.partial(_attn_kernel, n_src=n_src, group=group, n_chunks=n_chunks, chunk=chunk),
        grid=(b, N_KV_HEADS),
        in_specs=[pl.BlockSpec((1, n, gw), lambda bi, hi: (bi, 0, hi))]
        + [pl.BlockSpec((1, k.shape[1], HEAD_DIM), lambda bi, hi: (bi, 0, hi)) for k in ks]
        + [pl.BlockSpec((1, HEAD_DIM, v.shape[2]), lambda bi, hi: (bi, hi, 0)) for v in vts],
        out_specs=pl.BlockSpec((1, n, gw), lambda bi, hi: (bi, 0, hi)),
        out_shape=jax.ShapeDtypeStruct((b, n, aw), BF16),
        scratch_shapes=scratch,
        compiler_params=_cparams(("parallel", "arbitrary")),
        name="attention",
    )(q, *ks, *vts)


def _fourier_kernel(f_ref, cc_ref, sc_ref, cn_ref, sn_ref, w_ref, o_ref, xc_ref, xs_ref, *, scale):
    bi = pl.program_id(1)

    @pl.when(pl.program_id(0) == 0)
    def _():
        f = f_ref[0]
        xc_ref[bi] = jnp.dot(f, cc_ref[...], preferred_element_type=F32).astype(BF16)
        xs_ref[bi] = jnp.dot(f, sc_ref[...], preferred_element_type=F32).astype(BF16)

    fr = (jnp.dot(cn_ref[...], xc_ref[bi], preferred_element_type=F32)
          - jnp.dot(sn_ref[...], xs_ref[bi], preferred_element_type=F32)) * scale
    o_ref[0] = jnp.dot(fr.astype(BF16), w_ref[0], preferred_element_type=F32).astype(BF16)


def _dft_kernel(clo_ref, slo_ref, chi_ref, shi_ref, cos_ref, sin_ref):
    c_lo, s_lo = clo_ref[...], slo_ref[...]
    for h in range(chi_ref.shape[1]):
        c_hi, s_hi = chi_ref[:, h:h + 1], shi_ref[:, h:h + 1]
        cos_ref[:, h * LANES:(h + 1) * LANES] = (c_hi * c_lo - s_hi * s_lo).astype(BF16)
        sin_ref[:, h * LANES:(h + 1) * LANES] = (s_hi * c_lo + c_hi * s_lo).astype(BF16)


def _dft_tables(n):
    k = jnp.arange(n, dtype=jnp.int32)

    def cos_sin(t):
        ang = ((k[:, None] * t[None, :]) % n).astype(F32) * (2.0 * jnp.pi / n)
        return jnp.cos(ang), jnp.sin(ang)

    if n <= 2 * LANES or n % LANES:
        return tuple(t.astype(BF16) for t in cos_sin(k))
    c_hi, s_hi = cos_sin(jnp.arange(n // LANES, dtype=jnp.int32) * LANES)
    c_lo, s_lo = cos_sin(jnp.arange(LANES, dtype=jnp.int32))
    tm = _row_tile(n, 256)

    def spec(w):
        return pl.BlockSpec((tm, w), lambda i: (i, 0))

    return pl.pallas_call(
        _dft_kernel,
        grid=(n // tm,),
        in_specs=[spec(LANES), spec(LANES), spec(n // LANES), spec(n // LANES)],
        out_specs=(spec(n), spec(n)),
        out_shape=(jax.ShapeDtypeStruct((n, n), BF16), jax.ShapeDtypeStruct((n, n), BF16)),
        compiler_params=_cparams(("parallel",)),
        name="dft_tables",
    )(c_lo, s_lo, c_hi, s_hi)


def _fourier(f, layer, w_fourier_bf, cn, sn, cc_bd, sc_bd):
    b, n, fw = f.shape
    tm = _row_tile(n, 1024)
    scale = float((n * (fw // N_FOURIER_HEADS)) ** -0.5)
    return pl.pallas_call(
        functools.partial(_fourier_kernel, scale=scale),
        grid=(n // tm, b),
        in_specs=[
            pl.BlockSpec((1, n, fw), lambda ti, bi: (jnp.where(ti == 0, bi, b - 1), 0, 0)),
            pl.BlockSpec((fw, fw), lambda ti, bi: (0, 0)),
            pl.BlockSpec((fw, fw), lambda ti, bi: (0, 0)),
            pl.BlockSpec((tm, n), lambda ti, bi: (ti, 0)),
            pl.BlockSpec((tm, n), lambda ti, bi: (ti, 0)),
            pl.BlockSpec((1, fw, fw), lambda ti, bi: (layer, 0, 0)),
        ],
        out_specs=pl.BlockSpec((1, tm, fw), lambda ti, bi: (bi, ti, 0)),
        out_shape=jax.ShapeDtypeStruct((b, n, fw), BF16),
        scratch_shapes=[pltpu.VMEM((b, n, fw), BF16), pltpu.VMEM((b, n, fw), BF16)],
        compiler_params=_cparams(("arbitrary", "arbitrary")),
        name="fourier",
    )(f, cc_bd, sc_bd, cn, sn, w_fourier_bf)


POOL_HALO = 16
POOL_EDGE = 8


def _pool_kernel(p_ref, cnt_ref, w_ref, s_ref, o_ref, x_ref, s2_ref, s4_ref, s8_ref, *, n, pw):
    gdim = pw // len(POOL_WINDOWS)
    lo, hi = POOL_EDGE, n + 2 * POOL_HALO - POOL_EDGE
    for ref in (x_ref, s2_ref, s4_ref, s8_ref):
        ref[0:POOL_HALO, :] = jnp.zeros((POOL_HALO, pw), F32)
        ref[n + POOL_HALO:n + 2 * POOL_HALO, :] = jnp.zeros((POOL_HALO, pw), F32)
    x_ref[POOL_HALO:POOL_HALO + n, :] = p_ref[0]

    s2_ref[lo:hi, :] = x_ref[lo - 1:hi - 1, :] + x_ref[lo:hi, :]
    s4_ref[lo:hi, :] = s2_ref[lo - 1:hi - 1, :] + s2_ref[lo + 1:hi + 1, :]
    s8_ref[lo:hi, :] = s4_ref[lo - 2:hi - 2, :] + s4_ref[lo + 2:hi + 2, :]
    a, b = POOL_HALO, POOL_HALO + n
    s16 = s8_ref[a - 4:b - 4, :] + s8_ref[a + 4:b + 4, :]
    lane = lax.broadcasted_iota(jnp.int32, (1, pw), 1)
    acc = jnp.where(lane < gdim, s2_ref[a:b, :],
                    jnp.where(lane < 2 * gdim, s4_ref[a:b, :], jnp.where(lane < 3 * gdim, s8_ref[a:b, :], s16)))
    dlt = acc / cnt_ref[...] - x_ref[a:b, :]
    y = jnp.dot(dlt.astype(BF16), w_ref[0], preferred_element_type=F32) * s_ref[0]
    o_ref[0] = y.astype(BF16)


def _pool_counts(n, pw):
    gdim = pw // len(POOL_WINDOWS)
    half = jnp.repeat(jnp.asarray([w // 2 for w in POOL_WINDOWS], jnp.int32), gdim)[None, :]
    t = jnp.arange(n, dtype=jnp.int32)[:, None]
    return (jnp.minimum(t + half, n) - jnp.maximum(t - half, 0)).astype(F32)


def _pool(p, cnt, layer, w_pool_bd, pool_scale):
    b, n, pw = p.shape
    assert POOL_WINDOWS == (2, 4, 8, 16)
    buf = pltpu.VMEM((n + 2 * POOL_HALO, pw), F32)
    return pl.pallas_call(
        functools.partial(_pool_kernel, n=n, pw=pw),
        grid=(b,),
        in_specs=[
            pl.BlockSpec((1, n, pw), lambda bi: (bi, 0, 0)),
            pl.BlockSpec((n, pw), lambda bi: (0, 0)),
            pl.BlockSpec((1, pw, pw), lambda bi: (layer, 0, 0)),
            pl.BlockSpec((1, 1, pw), lambda bi: (layer, 0, 0)),
        ],
        out_specs=pl.BlockSpec((1, n, pw), lambda bi: (bi, 0, 0)),
        out_shape=jax.ShapeDtypeStruct((b, n, pw), BF16),
        scratch_shapes=[buf, buf, buf, buf],
        compiler_params=_cparams(("parallel",)),
        name="pool",
    )(p, cnt, w_pool_bd, pool_scale)


def _outproj_kernel(fo_ref, po_ref, ao_ref, x_ref, modc_ref, modp_ref, g_ref, w_ref, wr_ref,
                    xn_ref, h_ref, aff_ref, xn0_ref, xn1_ref, *, d, fw, pw):
    step = pl.program_id(0)

    @pl.when(step == 0)
    def _():
        xn1_ref[...] = jnp.zeros_like(xn1_ref)

    def project(dst_ref):
        w = w_ref[0]
        ox = jnp.dot(fo_ref[0], w[0:fw], preferred_element_type=F32)
        ox = ox + jnp.dot(po_ref[0], w[fw:fw + pw], preferred_element_type=F32)
        ox = ox + jnp.dot(ao_ref[0], w[fw + pw:], preferred_element_type=F32)
        dst_ref[...] = x_ref[0] + modc_ref[0][:, 2 * d:3 * d] * ox

    def emit(src_ref):
        mod = modp_ref[0]
        xn = src_ref[...]
        xn_ref[0] = xn
        h = _modulate(xn, g_ref[0], mod[:, 3 * d:4 * d], mod[:, 4 * d:5 * d])
        logits = jnp.dot(h.astype(BF16), wr_ref[0], preferred_element_type=F32)
        lane = lax.broadcasted_iota(jnp.int32, logits.shape, 1)
        logits = jnp.where(lane < N_EXPERTS, logits, -jnp.inf)
        e = jnp.exp(logits - logits.max(axis=-1, keepdims=True))
        aff = e / e.sum(axis=-1, keepdims=True)
        aff_ref[0] = aff.T[0:N_EXPERTS, :]
        a_hi = aff.astype(BF16).astype(F32)
        r1 = aff - a_hi
        a_mid = r1.astype(BF16).astype(F32)
        a_lo = (r1 - a_mid).astype(BF16).astype(F32)
        tail = a_hi + pltpu.roll(a_mid, N_EXPERTS, 1) + pltpu.roll(a_lo, 2 * N_EXPERTS, 1)
        h_ref[0, :, 0:d] = h.astype(BF16)
        h_ref[0, :, d:d + LANES] = tail.astype(BF16)

    @pl.when(step % 2 == 0)
    def _():
        project(xn0_ref)
        emit(xn1_ref)

    @pl.when(step % 2 == 1)
    def _():
        project(xn1_ref)
        emit(xn0_ref)


def _outproj(fo, po, ao, x, mod3, mod_row, layer, norm_g, w_out_bf, w_router_pad):
    b, n, d = x.shape
    fw = fo.shape[-1]
    pw = po.shape[-1]
    aw = ao.shape[-1]
    n6 = mod3.shape[-1]
    tm = _row_tile(n, 512)
    nt = n // tm
    total = b * nt

    def cur(s):
        return jnp.minimum(s, total - 1)

    def prev(s):
        return jnp.maximum(s - 1, 0)

    def in_spec(w):
        return pl.BlockSpec((1, tm, w), lambda s: (cur(s) // nt, cur(s) % nt, 0))

    return pl.pallas_call(
        functools.partial(_outproj_kernel, d=d, fw=fw, pw=pw),
        grid=(total + 1,),
        in_specs=[
            in_spec(fw), in_spec(pw), in_spec(aw), in_spec(d),
            pl.BlockSpec((1, 1, n6), lambda s: (mod_row(cur(s) // nt), 0, 0)),
            pl.BlockSpec((1, 1, n6), lambda s: (mod_row(prev(s) // nt), 0, 0)),
            pl.BlockSpec((1, 1, d), lambda s: (layer, 0, 0)),
            pl.BlockSpec((1, d, d), lambda s: (layer, 0, 0)),
            pl.BlockSpec((1, d, LANES), lambda s: (layer, 0, 0)),
        ],
        out_specs=(pl.BlockSpec((1, tm, d), lambda s: (prev(s) // nt, prev(s) % nt, 0)),
                   pl.BlockSpec((1, tm, d + LANES), lambda s: (prev(s) // nt, prev(s) % nt, 0)),
                   pl.BlockSpec((1, N_EXPERTS, tm), lambda s: (prev(s) // nt, 0, prev(s) % nt))),
        out_shape=(jax.ShapeDtypeStruct((b, n, d), F32),
                   jax.ShapeDtypeStruct((b, n, d + LANES), BF16),
                   jax.ShapeDtypeStruct((b, N_EXPERTS, n), F32)),
        scratch_shapes=[pltpu.VMEM((tm, d), F32), pltpu.VMEM((tm, d), F32)],
        compiler_params=_cparams(("arbitrary",)),
        name="outproj",
    )(fo, po, ao, x, mod3, mod3, norm_g, w_out_bf, w_router_pad)


ROUTE_BLOCK = 256
COMBINE_WINDOW = 64
NOT_IN_WINDOW = -(1 << 20)
FLAG_LANE0 = LANES // 2


def _combine_geometry(cap):
    win = min(COMBINE_WINDOW, cap)
    return win, N_EXPERTS


def _route_kernel(aff_ref, key_ref, relw_ref, soff_ref, tri_ref, *, n, cap, blk, win, grp):
    nb = n // blk
    e_n = N_EXPERTS

    @pl.when(pl.program_id(0) == 0)
    def _():
        r = lax.broadcasted_iota(jnp.int32, (blk, blk), 0)
        c = lax.broadcasted_iota(jnp.int32, (blk, blk), 1)
        tri_ref[...] = jnp.where(r < c, 1.0, 0.0).astype(BF16)

    aff = aff_ref[0]

    def count_ge(bits):
        return jnp.sum(jnp.where(aff >= pltpu.bitcast(bits, F32), 1.0, 0.0), axis=-1, keepdims=True)

    def search(i, thr_bits):
        lo = lax.shift_left(jnp.int32(1), 29 - 2 * i)
        hi = lo + lo
        take_hi = count_ge(thr_bits | hi) >= cap
        take_both = count_ge(thr_bits | hi | lo) >= cap
        take_lo = count_ge(thr_bits | lo) >= cap
        with_hi = jnp.where(take_both, thr_bits | hi | lo, thr_bits | hi)
        without_hi = jnp.where(take_lo, thr_bits | lo, thr_bits)
        return jnp.where(take_hi, with_hi, without_hi)

    thr_bits = lax.fori_loop(0, 15, search, jnp.zeros((e_n, 1), jnp.int32))
    thr_bits = jnp.where(count_ge(thr_bits | 1) >= cap, thr_bits | 1, thr_bits)
    thr = pltpu.bitcast(thr_bits, F32)
    gt = jnp.where(aff > thr, 1.0, 0.0)
    eq = jnp.where(aff == thr, 1.0, 0.0)
    need = cap - jnp.sum(gt, axis=-1, keepdims=True)
    tri = tri_ref[...]

    sel = []
    off = jnp.zeros((e_n, 1), F32)
    for j in range(nb):
        eqj = eq[:, j * blk:(j + 1) * blk]
        rank = jnp.dot(eqj.astype(BF16), tri, preferred_element_type=F32) + off
        off = off + jnp.sum(eqj, axis=-1, keepdims=True)
        sel.append(jnp.maximum(gt[:, j * blk:(j + 1) * blk], jnp.where(rank < need, eqj, 0.0)))

    lane = lax.broadcasted_iota(jnp.int32, (e_n, LANES), 1)
    q_off = ((lax.broadcasted_iota(jnp.int32, (e_n, 1), 0) % grp) * win).astype(F32)
    soff = jnp.zeros((e_n, LANES), F32)
    off = jnp.zeros((e_n, 1), F32)
    for j in range(nb):
        selj = sel[j]
        pos = jnp.dot(selj.astype(BF16), tri, preferred_element_type=F32) + off
        soff = jnp.where(lane == j, off, soff)
        start = jnp.minimum(jnp.floor(off * (1.0 / 16.0)) * 16.0, float(cap - win))
        rel = pos - start
        chosen = selj > 0.0
        key_ref[0, :, j * blk:(j + 1) * blk] = jnp.where(chosen, pos, -1.0).astype(jnp.int32)
        relw_ref[0, :, j * blk:(j + 1) * blk] = jnp.where(
            chosen & (rel < win), rel + q_off, float(NOT_IN_WINDOW)).astype(jnp.int32)
        count = jnp.sum(selj, axis=-1, keepdims=True)
        spills = jnp.max(jnp.where(off - start + count > win, 1.0, 0.0), axis=0, keepdims=True)
        soff = jnp.where(lane == FLAG_LANE0 + j, spills, soff)
        off = off + count
    soff_ref[0] = jnp.where(lane == nb, off, soff).astype(jnp.int32)


def _route(aff_t, cap):
    b, e, n = aff_t.shape
    blk = min(ROUTE_BLOCK, n)
    assert n % blk == 0 and n // blk < FLAG_LANE0
    win, grp = _combine_geometry(cap)

    def spec(w):
        return pl.BlockSpec((1, e, w), lambda bi: (bi, 0, 0))

    return pl.pallas_call(
        functools.partial(_route_kernel, n=n, cap=cap, blk=blk, win=win, grp=grp),
        grid=(b,),
        in_specs=[spec(n)],
        out_specs=(spec(n), spec(n), spec(LANES)),
        out_shape=(jax.ShapeDtypeStruct((b, e, n), jnp.int32),
                   jax.ShapeDtypeStruct((b, e, n), jnp.int32),
                   jax.ShapeDtypeStruct((b, e, LANES), jnp.int32)),
        scratch_shapes=[pltpu.VMEM((blk, blk), BF16)],
        compiler_params=_cparams(("arbitrary",)),
        name="moe_route",
    )(aff_t)


def _window_start(soff_ref, e, j, stride, cap, win):
    s = soff_ref[0, 0, e * stride + j]
    return s, jnp.minimum((s >> 4) << 4, cap - win)


def _gather_kernel(soff_ref, relw_ref, key_ref, h_ref, o_ref, *, cap, n_exp, nb, win):
    j = pl.program_id(1)
    stride = nb + 1
    blk = relw_ref.shape[-1]

    @pl.when(j == 0)
    def _():
        o_ref[...] = jnp.zeros_like(o_ref)

    row = lax.broadcasted_iota(jnp.int32, (n_exp * win, blk), 0)
    slabs = [jnp.broadcast_to(relw_ref[0, e:e + 1, :], (win, blk)) for e in range(n_exp)]
    onehot = jnp.where(jnp.concatenate(slabs, axis=0) == row, 1.0, 0.0).astype(BF16)
    rows = jnp.dot(onehot, h_ref[0], preferred_element_type=F32).astype(BF16)
    for e in range(n_exp):
        _, start = _window_start(soff_ref, e, j, stride, cap, win)
        dst = o_ref.at[0, e, pl.ds(pl.multiple_of(start, 16), win), :]
        dst[...] = dst[...] + rows[e * win:(e + 1) * win]

    row_w = lax.broadcasted_iota(jnp.int32, (win, blk), 0)

    def further_windows(e, carry):
        s, start0 = _window_start(soff_ref, e, j, stride, cap, win)
        count = soff_ref[0, 0, e * stride + j + 1] - s
        n_win = (s - start0 + count + win - 1) // win

        def body(k, c):
            lo = start0 + k * win
            st = pl.multiple_of(jnp.minimum(lo, cap - win), 16)
            krow = key_ref[0, pl.ds(e, 1), :]
            hot = jnp.where((krow - st == row_w) & (krow >= lo), 1.0, 0.0).astype(BF16)
            dst = o_ref.at[0, e, pl.ds(st, win), :]
            dst[...] = dst[...] + jnp.dot(hot, h_ref[0], preferred_element_type=F32).astype(BF16)
            return c

        lax.fori_loop(1, n_win, body, 0)
        return carry

    @pl.when(soff_ref[0, 0, n_exp * stride + j] > 0)
    def _():
        lax.fori_loop(0, n_exp, further_windows, 0)


def _route_scalars(soff, nb):
    b, e, _ = soff.shape
    return jnp.concatenate([soff[:, :, :nb + 1].reshape(b, 1, e * (nb + 1)),
                            soff[:, :1, FLAG_LANE0:FLAG_LANE0 + nb]], axis=-1)


def _gather(h_aug, soff, relw, key, cap):
    b, n, dw = h_aug.shape
    e = relw.shape[1]
    blk = min(ROUTE_BLOCK, n)
    nb = n // blk
    win, _ = _combine_geometry(cap)
    return pl.pallas_call(
        functools.partial(_gather_kernel, cap=cap, n_exp=e, nb=nb, win=win),
        grid=(b, nb),
        in_specs=[
            pl.BlockSpec((1, 1, e * (nb + 1) + nb), lambda bi, ji: (bi, 0, 0), memory_space=pltpu.SMEM),
            pl.BlockSpec((1, e, blk), lambda bi, ji: (bi, 0, ji)),
            pl.BlockSpec((1, e, blk), lambda bi, ji: (bi, 0, ji)),
            pl.BlockSpec((1, blk, dw), lambda bi, ji: (bi, ji, 0)),
        ],
        out_specs=pl.BlockSpec((1, e, cap, dw), lambda bi, ji: (bi, 0, 0, 0)),
        out_shape=jax.ShapeDtypeStruct((b, e, cap, dw), BF16),
        compiler_params=_cparams(("parallel", "arbitrary")),
        name="moe_gather",
    )(_route_scalars(soff, nb), relw, key, h_aug)


def _ffn_kernel(*refs, bt, n_sets, d):
    ins = refs[:n_sets]
    wg_ref, wu_ref, wd_ref = refs[n_sets:n_sets + 3]
    outs = refs[n_sets + 3:2 * n_sets + 3]
    wg_bf, wu_bf, wd_bf = refs[2 * n_sets + 3:]
    ei = pl.program_id(0)

    @pl.when(pl.program_id(1) == 0)
    def _():
        wg_bf[...] = wg_ref[0, 0].astype(BF16)
        wu_bf[...] = wu_ref[0, 0].astype(BF16)
        wd_bf[...] = wd_ref[0, 0].astype(BF16)

    def expert(rows):
        xs = rows[:, 0:d]
        tail = rows[:, d:d + LANES].astype(F32)
        lane = lax.broadcasted_iota(jnp.int32, tail.shape, 1)
        mine = (lane == ei) | (lane == ei + N_EXPERTS) | (lane == ei + 2 * N_EXPERTS)
        gate = jnp.sum(jnp.where(mine, tail, 0.0), axis=-1, keepdims=True)
        a = jnp.dot(xs, wg_bf[...], preferred_element_type=F32)
        u = jnp.dot(xs, wu_bf[...], preferred_element_type=F32)
        hid = (a * jax.nn.sigmoid(a) * u).astype(BF16)
        return (jnp.dot(hid, wd_bf[...], preferred_element_type=F32) * gate).astype(BF16)

    for xs_ref, y_ref in zip(ins, outs):
        cap, dw = xs_ref.shape[2], xs_ref.shape[3]
        if cap >= 256:
            for bi in range(bt):
                y_ref[bi, 0] = expert(xs_ref[bi, 0])
        else:
            y_ref[:, 0] = expert(xs_ref[:, 0].reshape(bt * cap, dw)).reshape(bt, cap, d)


def _ffn(sets, layer, w_gate, w_up, w_down):
    b, e, _, dw = sets[0].shape
    d = w_gate.shape[-2]
    ff = w_gate.shape[-1]
    bt = min(b, 8)
    assert b % bt == 0 and dw == d + LANES
    in_specs = [pl.BlockSpec((bt, 1, xs.shape[2], dw), lambda ei, ji: (ji, ei, 0, 0)) for xs in sets]
    out_specs = [pl.BlockSpec((bt, 1, xs.shape[2], d), lambda ei, ji: (ji, ei, 0, 0)) for xs in sets]
    out_shape = [jax.ShapeDtypeStruct(xs.shape[:3] + (d,), BF16) for xs in sets]
    in_specs += [
        pl.BlockSpec((1, 1, d, ff), lambda ei, ji: (layer, ei, 0, 0)),
        pl.BlockSpec((1, 1, d, ff), lambda ei, ji: (layer, ei, 0, 0)),
        pl.BlockSpec((1, 1, ff, d), lambda ei, ji: (layer, ei, 0, 0)),
    ]
    return pl.pallas_call(
        functools.partial(_ffn_kernel, bt=bt, n_sets=len(sets), d=d),
        grid=(e, b // bt),
        in_specs=in_specs,
        out_specs=tuple(out_specs),
        out_shape=tuple(out_shape),
        scratch_shapes=[pltpu.VMEM((d, ff), BF16), pltpu.VMEM((d, ff), BF16), pltpu.VMEM((ff, d), BF16)],
        compiler_params=_cparams(("parallel", "arbitrary")),
        name="moe_ffn",
    )(*sets, w_gate, w_up, w_down)


def _combine_kernel(soff_ref, relw_ref, key_ref, y_ref, xn_ref, mod_ref, o_ref, rhs_ref,
                    *, cap, d, n_exp, nb, win, grp):
    j = pl.program_id(1)
    stride = nb + 1
    blk = relw_ref.shape[-1]
    dn_t = (((0,), (0,)), ((), ()))

    def window_start(e):
        return _window_start(soff_ref, e, j, stride, cap, win)

    row = lax.broadcasted_iota(jnp.int32, (grp * win, blk), 0)
    acc = jnp.zeros((blk, d), F32)
    for g0 in range(0, n_exp, grp):
        slabs = []
        for q in range(grp):
            e = g0 + q
            _, start = window_start(e)
            rhs_ref[q * win:(q + 1) * win, :] = y_ref[0, e, pl.ds(pl.multiple_of(start, 16), win), :]
            slabs.append(jnp.broadcast_to(relw_ref[0, e:e + 1, :], (win, blk)))
        onehot = jnp.where(jnp.concatenate(slabs, axis=0) == row, 1.0, 0.0).astype(BF16)
        acc = acc + lax.dot_general(onehot, rhs_ref[...], dn_t, preferred_element_type=F32)
    o_ref[0] = acc

    row_w = lax.broadcasted_iota(jnp.int32, (win, blk), 0)

    def further_windows(e, carry):
        s, start0 = window_start(e)
        count = soff_ref[0, 0, e * stride + j + 1] - s
        n_win = (s - start0 + count + win - 1) // win

        def body(k, c):
            lo = start0 + k * win
            st = pl.multiple_of(jnp.minimum(lo, cap - win), 16)
            krow = key_ref[0, pl.ds(e, 1), :]
            onehot = jnp.where((krow - st == row_w) & (krow >= lo), 1.0, 0.0).astype(BF16)
            o_ref[0] = o_ref[0] + lax.dot_general(onehot, y_ref[0, e, pl.ds(st, win), :], dn_t,
                                                  preferred_element_type=F32)
            return c

        lax.fori_loop(1, n_win, body, 0)
        return carry

    @pl.when(soff_ref[0, 0, n_exp * stride + j] > 0)
    def _():
        lax.fori_loop(0, n_exp, further_windows, 0)

    o_ref[0] = xn_ref[0] + mod_ref[0][:, 5 * d:6 * d] * o_ref[0]


def _combine(y, soff, relw, key, xn, mod3, mod_row):
    b, e, cap, d = y.shape
    n = xn.shape[1]
    n6 = mod3.shape[-1]
    blk = min(ROUTE_BLOCK, n)
    nb = n // blk
    win, grp = _combine_geometry(cap)
    return pl.pallas_call(
        functools.partial(_combine_kernel, cap=cap, d=d, n_exp=e, nb=nb, win=win, grp=grp),
        grid=(b, nb),
        in_specs=[
            pl.BlockSpec((1, 1, e * (nb + 1) + nb), lambda bi, ji: (bi, 0, 0), memory_space=pltpu.SMEM),
            pl.BlockSpec((1, e, blk), lambda bi, ji: (bi, 0, ji)),
            pl.BlockSpec((1, e, blk), lambda bi, ji: (bi, 0, ji)),
            pl.BlockSpec((1, e, cap, d), lambda bi, ji: (bi, 0, 0, 0)),
            pl.BlockSpec((1, blk, d), lambda bi, ji: (bi, ji, 0)),
            pl.BlockSpec((1, 1, n6), lambda bi, ji: (mod_row(bi), 0, 0)),
        ],
        out_specs=pl.BlockSpec((1, blk, d), lambda bi, ji: (bi, ji, 0)),
        out_shape=jax.ShapeDtypeStruct((b, n, d), F32),
        scratch_shapes=[pltpu.VMEM((grp * win, d), BF16)],
        compiler_params=_cparams(("parallel", "arbitrary")),
        name="moe_combine",
    )(_route_scalars(soff, nb), relw, key, y, xn, mod3)


def _moe_dispatch(h_aug, aff_t):
    n = h_aug.shape[1]
    cap = EC_CAPACITY_FACTOR * n // N_EXPERTS
    key, relw, soff = _route(aff_t, cap)
    return _gather(h_aug, soff, relw, key, cap), (soff, relw, key)


def _rope_tables(n):
    t = jnp.arange(n, dtype=jnp.int32)
    row = (t // GRID_W).astype(F32)
    col = (t % GRID_W).astype(F32)
    inv = ROPE_THETA ** (-jnp.arange(0, ROPE_HALF, 2, dtype=F32) / ROPE_HALF)
    ar = row[:, None] * inv[None, :]
    ac = col[:, None] * inv[None, :]
    ang = jnp.concatenate([ar, ar, ac, ac], axis=-1)
    lane = jnp.arange(HEAD_DIM)
    sign = jnp.where((lane % ROPE_HALF) < ROPE_HALF // 2, -1.0, 1.0).astype(F32)
    return jnp.cos(ang), jnp.sin(ang) * sign[None, :]


def _block_diag(blocks):
    g, a, c = blocks.shape
    eye = jnp.eye(g, dtype=blocks.dtype)
    return (eye[:, None, :, None] * blocks[:, :, None, :]).reshape(g * a, g * c)


def kernel(x, c, ctx, c_ctx, ada_w, ada_b, norm1_g, norm2_g, w_in, w_fourier, w_pool, pool_scale,
           q_norm_g, k_norm_g, w_out, w_router, w_gate, w_up, w_down):
    b, n, d = x.shape
    lc = ctx.shape[1]
    depth = ada_w.shape[0]
    fw = d // 4

    rows = -(-(b + 1) // SUBLANES) * SUBLANES
    cond = jnp.zeros((rows, d), F32).at[:b].set(c).at[b].set(c_ctx)
    mod = _ada_mod(cond, ada_w, ada_b)

    def lat_row(bi):
        return bi

    def ctx_row(bi):
        return b

    norm1_g = norm1_g.reshape(depth, 1, d)
    norm2_g = norm2_g.reshape(depth, 1, d)
    q_norm_g = q_norm_g.reshape(depth, 1, HEAD_DIM)
    k_norm_g = k_norm_g.reshape(depth, 1, HEAD_DIM)
    pool_scale = pool_scale.reshape(depth, 1, d // 4)
    w_in_bf = w_in.astype(BF16)
    w_out_bf = w_out.astype(BF16)
    w_fourier_bf = w_fourier.astype(BF16)
    w_router_pad = jnp.zeros((depth, d, LANES), BF16).at[:, :, :N_EXPERTS].set(w_router.astype(BF16))
    w_pool_bd = jax.vmap(_block_diag)(w_pool).astype(BF16)

    cnt_t = _pool_counts(n, d // 4)
    cnt_c = _pool_counts(lc, d // 4)
    cos_t, sin_t = _rope_tables(n)
    cos_c, sin_c = _rope_tables(lc)
    cn, sn = _dft_tables(n)
    cnc, snc = _dft_tables(lc)
    hd = fw // N_FOURIER_HEADS
    cch, sch = _dft_tables(hd)
    cc_bd = _block_diag(jnp.broadcast_to(cch, (N_FOURIER_HEADS, hd, hd)))
    sc_bd = _block_diag(jnp.broadcast_to(sch, (N_FOURIER_HEADS, hd, hd)))

    for i in range(depth):
        last = i == depth - 1
        mod3 = mod[i].reshape(rows, 1, 6 * d)

        fx, px, qx, kx, vtx = _inproj(x, mod3, lat_row, i, norm1_g, w_in_bf, q_norm_g, k_norm_g,
                                      cos_t, sin_t, True)
        fc, pc, qc, kc, vtc = _inproj(ctx, mod3, ctx_row, i, norm1_g, w_in_bf, q_norm_g, k_norm_g,
                                      cos_c, sin_c, False)

        ax = _attention(qx, [kx, kc], [vtx, vtc])
        fox = _fourier(fx, i, w_fourier_bf, cn, sn, cc_bd, sc_bd)
        pox = _pool(px, cnt_t, i, w_pool_bd, pool_scale)
        xn, hx, affx = _outproj(fox, pox, ax, x, mod3, lat_row, i, norm2_g, w_out_bf, w_router_pad)
        xs, route_x = _moe_dispatch(hx, affx)

        if last:
            (yx,) = _ffn([xs], i, w_gate, w_up, w_down)
        else:
            ac = _attention(qc, [kc], [vtc])
            foc = _fourier(fc, i, w_fourier_bf, cnc, snc, cc_bd, sc_bd)
            poc = _pool(pc, cnt_c, i, w_pool_bd, pool_scale)
            cn_, hc, affc = _outproj(foc, poc, ac, ctx, mod3, ctx_row, i, norm2_g, w_out_bf, w_router_pad)
            xsc, route_c = _moe_dispatch(hc, affc)
            yx, yc = _ffn([xs, xsc], i, w_gate, w_up, w_down)
            ctx = _combine(yc, *route_c, cn_, mod3, ctx_row)
        x = _combine(yx, *route_x, xn, mod3, lat_row)
    return x
```
